```python
import math
import jax
import jax.numpy as jnp
from jax import lax
import numpy as np

D_MODEL = 2048
BATCH = 8
SEQ = 8192
DEPTH = 2

N_A_LAYERS = DEPTH // 2
N_B_LAYERS = DEPTH - N_A_LAYERS

GDN_QK_HEADS = 16
GDN_V_HEADS = 32
GDN_HEAD_DIM = 128
GDN_QK_DIM = GDN_QK_HEADS * GDN_HEAD_DIM
GDN_V_DIM = GDN_V_HEADS * GDN_HEAD_DIM
CONV_DIM = 2 * GDN_QK_DIM + GDN_V_DIM
GDN_PROJ = CONV_DIM + GDN_V_DIM + 2 * GDN_V_HEADS
CONV_K = 4
GDN_CHUNK = 64

FOX_HEADS = 16
FOX_KV_HEADS = 2
FOX_GROUP = FOX_HEADS // FOX_KV_HEADS
FOX_HEAD_DIM = 256
FOX_Q_DIM = FOX_HEADS * FOX_HEAD_DIM
FOX_KV_DIM = FOX_KV_HEADS * FOX_HEAD_DIM
KV_PROJ = 2 * FOX_KV_DIM + FOX_HEADS
Q_BLOCK = 128

FFN_HIDDEN = ((8 * D_MODEL // 3 + 255) // 256) * 256

NORM_EPS = 1e-6

kernel_name = "yoco_gdn_fox_adaln_trunk"


def rms_norm(x, w):
    xf = x.astype(jnp.float32)
    y = xf * lax.rsqrt(jnp.mean(xf * xf, axis=-1, keepdims=True) + NORM_EPS)
    return (y * w.astype(jnp.float32)).astype(x.dtype)


def modulate(h, shift, scale):
    return h * (1 + scale) + shift


def l2_normalize(x):
    xf = x.astype(jnp.float32)
    return xf * lax.rsqrt(jnp.sum(xf * xf, axis=-1, keepdims=True) + NORM_EPS)


def swiglu(h, w_in, w_out):
    gate, up = jnp.split(h @ w_in, 2, axis=-1)
    return (jax.nn.silu(gate) * up) @ w_out


def causal_conv(x, w):
    width = w.shape[0]
    length = x.shape[1]
    xp = jnp.pad(x, ((0, 0), (width - 1, 0), (0, 0)))
    return sum(xp[:, i:i + length] * w[i] for i in range(width))


def gated_delta_rule_chunked(q, k, v, g, beta):
    b, h, length, dk = q.shape
    dv = v.shape[-1]
    n = length // GDN_CHUNK
    blk = lambda t: t.reshape(b, h, n, GDN_CHUNK, *t.shape[3:])
    q = blk(q) * dk ** -0.5
    k = blk(k)
    v = blk(v)
    beta = blk(beta)
    g = jnp.cumsum(blk(g), axis=-1)
    causal = jnp.tril(jnp.ones((GDN_CHUNK, GDN_CHUNK), dtype=bool))
    strict = jnp.tril(jnp.ones((GDN_CHUNK, GDN_CHUNK), dtype=bool), k=-1)
    decay = jnp.exp(jnp.where(causal, g[..., :, None] - g[..., None, :], -jnp.inf))
    k_beta = k * beta[..., None]
    a_strict = jnp.where(strict, jnp.einsum('bhnid,bhnjd->bhnij', k_beta, k) * decay, 0.0)
    eye = jnp.eye(GDN_CHUNK, dtype=q.dtype)
    rhs = jnp.concatenate([v * beta[..., None], k_beta * jnp.exp(g)[..., None]], axis=-1)
    sol = lax.linalg.triangular_solve(a_strict + eye, rhs, left_side=True, lower=True,
                                      unit_diagonal=True)
    u, w = sol[..., :dv], sol[..., dv:]
    attn = jnp.where(causal, jnp.einsum('bhnid,bhnjd->bhnij', q, k) * decay, 0.0)
    g_last = g[..., -1]
    q_dec = q * jnp.exp(g)[..., None]
    k_dec = k * jnp.exp(g_last[..., None] - g)[..., None]

    def step(state, xs):
        q_c, k_c, u_c, w_c, attn_c, gl_c = xs
        v_new = u_c - jnp.einsum('bhik,bhkv->bhiv', w_c, state)
        o = jnp.einsum('bhik,bhkv->bhiv', q_c, state) + jnp.einsum('bhij,bhjv->bhiv', attn_c, v_new)
        state = state * jnp.exp(gl_c)[..., None, None] + jnp.einsum('bhik,bhiv->bhkv', k_c, v_new)
        return state, o

    xs = tuple(jnp.moveaxis(t, 2, 0) for t in (q_dec, k_dec, u, w, attn, g_last))
    s0 = jnp.zeros((b, h, dk, dv), jnp.float32)
    _, o = lax.scan(step, s0, xs)
    return jnp.moveaxis(o, 0, 2).reshape(b, h, length, dv)


def gated_deltanet(h, w_in, conv_w, a_log, dt_bias, norm_w, w_out):
    b, length, _ = h.shape
    qkv, z, beta_logit, a = jnp.split(
        h @ w_in, [CONV_DIM, CONV_DIM + GDN_V_DIM, CONV_DIM + GDN_V_DIM + GDN_V_HEADS], axis=-1)
    qkv = jax.nn.silu(causal_conv(qkv, conv_w))
    q, k, v = jnp.split(qkv, [GDN_QK_DIM, 2 * GDN_QK_DIM], axis=-1)
    rep = GDN_V_HEADS // GDN_QK_HEADS
    heads = lambda t, nh: t.reshape(b, length, nh, GDN_HEAD_DIM)
    q = jnp.repeat(l2_normalize(heads(q, GDN_QK_HEADS)), rep, axis=2)
    k = jnp.repeat(l2_normalize(heads(k, GDN_QK_HEADS)), rep, axis=2)
    v = heads(v, GDN_V_HEADS).astype(jnp.float32)
    beta = jax.nn.sigmoid(beta_logit.astype(jnp.float32))
    g = -jnp.exp(a_log.astype(jnp.float32)) * jax.nn.softplus(
        a.astype(jnp.float32) + dt_bias.astype(jnp.float32))
    tr = lambda t: jnp.swapaxes(t, 1, 2)
    o = gated_delta_rule_chunked(tr(q), tr(k), tr(v), tr(g), tr(beta))
    o = rms_norm(tr(o), norm_w) * jax.nn.silu(heads(z, GDN_V_HEADS).astype(jnp.float32))
    return o.reshape(b, length, GDN_V_DIM).astype(h.dtype) @ w_out


def shared_kv(x, cond, ada_w, ada_b, norm_w, w_kv, k_norm_w, forget_b):
    b, length, _ = x.shape
    shift, scale = (m[:, None, :] for m in jnp.split(cond @ ada_w + ada_b, 2, axis=-1))
    h = modulate(rms_norm(x, norm_w), shift, scale)
    k, v, f_logit = jnp.split(h @ w_kv, [FOX_KV_DIM, 2 * FOX_KV_DIM], axis=-1)
    k = rms_norm(k.reshape(b, length, FOX_KV_HEADS, FOX_HEAD_DIM), k_norm_w)
    v = v.reshape(b, length, FOX_KV_HEADS, FOX_HEAD_DIM)
    log_f = jax.nn.log_sigmoid(f_logit.astype(jnp.float32) + forget_b.astype(jnp.float32))
    f_cum = jnp.cumsum(log_f, axis=1).reshape(b, length, FOX_KV_HEADS, FOX_GROUP)
    return k, v, f_cum


def blocked_forgetting_softmax(q, k, v, f_cum):
    b, length, kvh, grp, hd = q.shape
    nb = length // Q_BLOCK
    qb = jnp.swapaxes(q.reshape(b, nb, Q_BLOCK, kvh, grp, hd), 0, 1)
    fb = jnp.swapaxes(f_cum.reshape(b, nb, Q_BLOCK, kvh, grp), 0, 1)
    f_k = jnp.transpose(f_cum, (0, 2, 3, 1))
    k_pos = jnp.arange(length)
    scale = hd ** -0.5

    def one_block(args):
        i, q_i, f_i = args
        s = jnp.einsum('bqhgd,bkhd->bhgqk', q_i, k, preferred_element_type=jnp.float32) * scale
        s = s + jnp.transpose(f_i, (0, 2, 3, 1))[..., None] - f_k[..., None, :]
        q_pos = i * Q_BLOCK + jnp.arange(Q_BLOCK)
        s = jnp.where(k_pos[None, :] <= q_pos[:, None], s, -jnp.inf)
        p = jax.nn.softmax(s, axis=-1).astype(v.dtype)
        return jnp.einsum('bhgqk,bkhd->bqhgd', p, v)

    o = lax.map(one_block, (jnp.arange(nb), qb, fb))
    return jnp.swapaxes(o, 0, 1).reshape(b, length, kvh, grp, hd)


def forgetting_attention(h, k, v, f_cum, w_in, q_norm_w, w_out):
    b, length, _ = h.shape
    q, gate = jnp.split(h @ w_in, 2, axis=-1)
    q = rms_norm(q.reshape(b, length, FOX_HEADS, FOX_HEAD_DIM), q_norm_w)
    q = q.reshape(b, length, FOX_KV_HEADS, FOX_GROUP, FOX_HEAD_DIM)
    o = blocked_forgetting_softmax(q, k, v, f_cum).reshape(b, length, FOX_Q_DIM)
    return (o * jax.nn.sigmoid(gate)) @ w_out


def _fwd_setup_inputs(seed: int = 0) -> dict:
    key = jax.random.key(seed)
    ks = iter(jax.random.split(key, 40))
    f32 = jnp.float32

    def dense(shape, fan_in, s=1.0):
        return (s * fan_in ** -0.5) * jax.random.normal(next(ks), shape, f32)

    def gain(shape):
        return 1.0 + 0.02 * jax.random.normal(next(ks), shape, f32)

    def small(shape, s):
        return s * jax.random.normal(next(ks), shape, f32)

    d = D_MODEL
    x = jax.random.normal(next(ks), (BATCH, SEQ, d), f32)
    c = jax.random.normal(next(ks), (BATCH, d), f32)
    ada_w = dense((DEPTH, d, 6 * d), d, 0.5)
    ada_b = small((DEPTH, 6 * d), 0.02)
    norm_mix = gain((DEPTH, d))
    norm_ffn = gain((DEPTH, d))
    ffn_w_in = dense((DEPTH, d, 2 * FFN_HIDDEN), d)
    ffn_w_out = dense((DEPTH, FFN_HIDDEN, d), FFN_HIDDEN)
    gdn_w_in = dense((N_A_LAYERS, d, GDN_PROJ), d)
    gdn_conv = dense((N_A_LAYERS, CONV_K, CONV_DIM), CONV_K)
    gdn_a_log = jnp.log(jax.random.uniform(next(ks), (N_A_LAYERS, GDN_V_HEADS), f32, 1.0, 16.0))
    dt = jnp.exp(jax.random.uniform(next(ks), (N_A_LAYERS, GDN_V_HEADS), f32,
                                    math.log(1e-3), math.log(1e-1)))
    gdn_dt_bias = dt + jnp.log(-jnp.expm1(-dt))
    gdn_norm = gain((N_A_LAYERS, GDN_HEAD_DIM))
    gdn_w_out = dense((N_A_LAYERS, GDN_V_DIM, d), GDN_V_DIM)
    kv_ada_w = dense((d, 2 * d), d, 0.5)
    kv_ada_b = small((2 * d,), 0.02)
    kv_norm = gain((d,))
    kv_w = dense((d, KV_PROJ), d)
    k_norm = gain((FOX_HEAD_DIM,))
    forget_b = jax.random.uniform(next(ks), (FOX_HEADS,), f32, 1.0, 6.0)
    fox_w_in = dense((N_B_LAYERS, d, 2 * FOX_Q_DIM), d)
    q_norm = gain((N_B_LAYERS, FOX_HEAD_DIM))
    fox_w_out = dense((N_B_LAYERS, FOX_Q_DIM, d), FOX_Q_DIM)
    out_ada_w = dense((d, 2 * d), d, 0.5)
    out_ada_b = small((2 * d,), 0.02)
    out_norm = gain((d,))
    return {"x": x, "c": c, "ada_w": ada_w, "ada_b": ada_b, "norm_mix": norm_mix,
            "norm_ffn": norm_ffn, "ffn_w_in": ffn_w_in, "ffn_w_out": ffn_w_out,
            "gdn_w_in": gdn_w_in, "gdn_conv": gdn_conv, "gdn_a_log": gdn_a_log,
            "gdn_dt_bias": gdn_dt_bias, "gdn_norm": gdn_norm, "gdn_w_out": gdn_w_out,
            "kv_ada_w": kv_ada_w, "kv_ada_b": kv_ada_b, "kv_norm": kv_norm, "kv_w": kv_w,
            "k_norm": k_norm, "forget_b": forget_b, "fox_w_in": fox_w_in, "q_norm": q_norm,
            "fox_w_out": fox_w_out, "out_ada_w": out_ada_w, "out_ada_b": out_ada_b,
            "out_norm": out_norm}


def _fwd_reference(x, c, ada_w, ada_b, norm_mix, norm_ffn, ffn_w_in, ffn_w_out, gdn_w_in, gdn_conv,
              gdn_a_log, gdn_dt_bias, gdn_norm, gdn_w_out, kv_ada_w, kv_ada_b, kv_norm, kv_w,
              k_norm, forget_b, fox_w_in, q_norm, fox_w_out, out_ada_w, out_ada_b, out_norm):
    cond = jax.nn.silu(c)
    k_sh = v_sh = f_sh = None
    for layer in range(DEPTH):
        sh_m, sc_m, g_m, sh_f, sc_f, g_f = (
            m[:, None, :] for m in jnp.split(cond @ ada_w[layer] + ada_b[layer], 6, axis=-1))
        h = modulate(rms_norm(x, norm_mix[layer]), sh_m, sc_m)
        if layer < N_A_LAYERS:
            y = gated_deltanet(h, gdn_w_in[layer], gdn_conv[layer], gdn_a_log[layer],
                               gdn_dt_bias[layer], gdn_norm[layer], gdn_w_out[layer])
        else:
            if layer == N_A_LAYERS:
                k_sh, v_sh, f_sh = shared_kv(x, cond, kv_ada_w, kv_ada_b, kv_norm, kv_w,
                                             k_norm, forget_b)
            j = layer - N_A_LAYERS
            y = forgetting_attention(h, k_sh, v_sh, f_sh, fox_w_in[j], q_norm[j], fox_w_out[j])
        x = x + g_m * y
        h = modulate(rms_norm(x, norm_ffn[layer]), sh_f, sc_f)
        x = x + g_f * swiglu(h, ffn_w_in[layer], ffn_w_out[layer])
    sh_o, sc_o = (m[:, None, :] for m in jnp.split(cond @ out_ada_w + out_ada_b, 2, axis=-1))
    return modulate(rms_norm(x, out_norm), sh_o, sc_o)


import jax as _jax
import jax.numpy as _jnp

TWIN_FORMAT = 'train_step'
FWD_PARAMS = ['x', 'c', 'ada_w', 'ada_b', 'norm_mix', 'norm_ffn', 'ffn_w_in', 'ffn_w_out', 'gdn_w_in', 'gdn_conv', 'gdn_a_log', 'gdn_dt_bias', 'gdn_norm', 'gdn_w_out', 'kv_ada_w', 'kv_ada_b', 'kv_norm', 'kv_w', 'k_norm', 'forget_b', 'fox_w_in', 'q_norm', 'fox_w_out', 'out_ada_w', 'out_ada_b', 'out_norm']
TWIN_WEIGHTS = ['ada_w', 'ada_b', 'norm_mix', 'norm_ffn', 'ffn_w_in', 'ffn_w_out', 'gdn_w_in', 'gdn_conv', 'gdn_a_log', 'gdn_dt_bias', 'gdn_norm', 'gdn_w_out', 'kv_ada_w', 'kv_ada_b', 'kv_norm', 'kv_w', 'k_norm', 'forget_b', 'fox_w_in', 'q_norm', 'fox_w_out', 'out_ada_w', 'out_ada_b', 'out_norm']
TWIN_DIFF_INPUT = 'x'
TWIN_INPUTS = ['x', 'c', 'ada_w', 'ada_b', 'norm_mix', 'norm_ffn', 'ffn_w_in', 'ffn_w_out', 'gdn_w_in', 'gdn_conv', 'gdn_a_log', 'gdn_dt_bias', 'gdn_norm', 'gdn_w_out', 'kv_ada_w', 'kv_ada_b', 'kv_norm', 'kv_w', 'k_norm', 'forget_b', 'fox_w_in', 'q_norm', 'fox_w_out', 'out_ada_w', 'out_ada_b', 'out_norm', 'loss_target', 'm_ada_w', 'm_ada_b', 'm_norm_mix', 'm_norm_ffn', 'm_ffn_w_in', 'm_ffn_w_out', 'm_gdn_w_in', 'm_gdn_conv', 'm_gdn_a_log', 'm_gdn_dt_bias', 'm_gdn_norm', 'm_gdn_w_out', 'm_kv_ada_w', 'm_kv_ada_b', 'm_kv_norm', 'm_kv_w', 'm_k_norm', 'm_forget_b', 'm_fox_w_in', 'm_q_norm', 'm_fox_w_out', 'm_out_ada_w', 'm_out_ada_b', 'm_out_norm', 'v_ada_w', 'v_ada_b', 'v_norm_mix', 'v_norm_ffn', 'v_ffn_w_in', 'v_ffn_w_out', 'v_gdn_w_in', 'v_gdn_conv', 'v_gdn_a_log', 'v_gdn_dt_bias', 'v_gdn_norm', 'v_gdn_w_out', 'v_kv_ada_w', 'v_kv_ada_b', 'v_kv_norm', 'v_kv_w', 'v_k_norm', 'v_forget_b', 'v_fox_w_in', 'v_q_norm', 'v_fox_w_out', 'v_out_ada_w', 'v_out_ada_b', 'v_out_norm']
TWIN_OUTPUTS = ['loss', 'grad_x', 'grad_ada_w', 'grad_ada_b', 'grad_norm_mix', 'grad_norm_ffn', 'grad_ffn_w_in', 'grad_ffn_w_out', 'grad_gdn_w_in', 'grad_gdn_conv', 'grad_gdn_a_log', 'grad_gdn_dt_bias', 'grad_gdn_norm', 'grad_gdn_w_out', 'grad_kv_ada_w', 'grad_kv_ada_b', 'grad_kv_norm', 'grad_kv_w', 'grad_k_norm', 'grad_forget_b', 'grad_fox_w_in', 'grad_q_norm', 'grad_fox_w_out', 'grad_out_ada_w', 'grad_out_ada_b', 'grad_out_norm', 'delta_ada_w', 'delta_ada_b', 'delta_norm_mix', 'delta_norm_ffn', 'delta_ffn_w_in', 'delta_ffn_w_out', 'delta_gdn_w_in', 'delta_gdn_conv', 'delta_gdn_a_log', 'delta_gdn_dt_bias', 'delta_gdn_norm', 'delta_gdn_w_out', 'delta_kv_ada_w', 'delta_kv_ada_b', 'delta_kv_norm', 'delta_kv_w', 'delta_k_norm', 'delta_forget_b', 'delta_fox_w_in', 'delta_q_norm', 'delta_fox_w_out', 'delta_out_ada_w', 'delta_out_ada_b', 'delta_out_norm', 'new_m_ada_w', 'new_m_ada_b', 'new_m_norm_mix', 'new_m_norm_ffn', 'new_m_ffn_w_in', 'new_m_ffn_w_out', 'new_m_gdn_w_in', 'new_m_gdn_conv', 'new_m_gdn_a_log', 'new_m_gdn_dt_bias', 'new_m_gdn_norm', 'new_m_gdn_w_out', 'new_m_kv_ada_w', 'new_m_kv_ada_b', 'new_m_kv_norm', 'new_m_kv_w', 'new_m_k_norm', 'new_m_forget_b', 'new_m_fox_w_in', 'new_m_q_norm', 'new_m_fox_w_out', 'new_m_out_ada_w', 'new_m_out_ada_b', 'new_m_out_norm', 'new_v_ada_w', 'new_v_ada_b', 'new_v_norm_mix', 'new_v_norm_ffn', 'new_v_ffn_w_in', 'new_v_ffn_w_out', 'new_v_gdn_w_in', 'new_v_gdn_conv', 'new_v_gdn_a_log', 'new_v_gdn_dt_bias', 'new_v_gdn_norm', 'new_v_gdn_w_out', 'new_v_kv_ada_w', 'new_v_kv_ada_b', 'new_v_kv_norm', 'new_v_kv_w', 'new_v_k_norm', 'new_v_forget_b', 'new_v_fox_w_in', 'new_v_q_norm', 'new_v_fox_w_out', 'new_v_out_ada_w', 'new_v_out_ada_b', 'new_v_out_norm']
TWIN_LEAF_KINDS = {'loss': 'loss', 'grad_x': 'grad_x', 'grad_ada_w': 'grad_w', 'grad_ada_b': 'grad_w', 'grad_norm_mix': 'grad_w', 'grad_norm_ffn': 'grad_w', 'grad_ffn_w_in': 'grad_w', 'grad_ffn_w_out': 'grad_w', 'grad_gdn_w_in': 'grad_w', 'grad_gdn_conv': 'grad_w', 'grad_gdn_a_log': 'grad_w', 'grad_gdn_dt_bias': 'grad_w', 'grad_gdn_norm': 'grad_w', 'grad_gdn_w_out': 'grad_w', 'grad_kv_ada_w': 'grad_w', 'grad_kv_ada_b': 'grad_w', 'grad_kv_norm': 'grad_w', 'grad_kv_w': 'grad_w', 'grad_k_norm': 'grad_w', 'grad_forget_b': 'grad_w', 'grad_fox_w_in': 'grad_w', 'grad_q_norm': 'grad_w', 'grad_fox_w_out': 'grad_w', 'grad_out_ada_w': 'grad_w', 'grad_out_ada_b': 'grad_w', 'grad_out_norm': 'grad_w', 'delta_ada_w': 'delta_w', 'delta_ada_b': 'delta_w', 'delta_norm_mix': 'delta_w', 'delta_norm_ffn': 'delta_w', 'delta_ffn_w_in': 'delta_w', 'delta_ffn_w_out': 'delta_w', 'delta_gdn_w_in': 'delta_w', 'delta_gdn_conv': 'delta_w', 'delta_gdn_a_log': 'delta_w', 'delta_gdn_dt_bias': 'delta_w', 'delta_gdn_norm': 'delta_w', 'delta_gdn_w_out': 'delta_w', 'delta_kv_ada_w': 'delta_w', 'delta_kv_ada_b': 'delta_w', 'delta_kv_norm': 'delta_w', 'delta_kv_w': 'delta_w', 'delta_k_norm': 'delta_w', 'delta_forget_b': 'delta_w', 'delta_fox_w_in': 'delta_w', 'delta_q_norm': 'delta_w', 'delta_fox_w_out': 'delta_w', 'delta_out_ada_w': 'delta_w', 'delta_out_ada_b': 'delta_w', 'delta_out_norm': 'delta_w', 'new_m_ada_w': 'new_m', 'new_m_ada_b': 'new_m', 'new_m_norm_mix': 'new_m', 'new_m_norm_ffn': 'new_m', 'new_m_ffn_w_in': 'new_m', 'new_m_ffn_w_out': 'new_m', 'new_m_gdn_w_in': 'new_m', 'new_m_gdn_conv': 'new_m', 'new_m_gdn_a_log': 'new_m', 'new_m_gdn_dt_bias': 'new_m', 'new_m_gdn_norm': 'new_m', 'new_m_gdn_w_out': 'new_m', 'new_m_kv_ada_w': 'new_m', 'new_m_kv_ada_b': 'new_m', 'new_m_kv_norm': 'new_m', 'new_m_kv_w': 'new_m', 'new_m_k_norm': 'new_m', 'new_m_forget_b': 'new_m', 'new_m_fox_w_in': 'new_m', 'new_m_q_norm': 'new_m', 'new_m_fox_w_out': 'new_m', 'new_m_out_ada_w': 'new_m', 'new_m_out_ada_b': 'new_m', 'new_m_out_norm': 'new_m', 'new_v_ada_w': 'new_v', 'new_v_ada_b': 'new_v', 'new_v_norm_mix': 'new_v', 'new_v_norm_ffn': 'new_v', 'new_v_ffn_w_in': 'new_v', 'new_v_ffn_w_out': 'new_v', 'new_v_gdn_w_in': 'new_v', 'new_v_gdn_conv': 'new_v', 'new_v_gdn_a_log': 'new_v', 'new_v_gdn_dt_bias': 'new_v', 'new_v_gdn_norm': 'new_v', 'new_v_gdn_w_out': 'new_v', 'new_v_kv_ada_w': 'new_v', 'new_v_kv_ada_b': 'new_v', 'new_v_kv_norm': 'new_v', 'new_v_kv_w': 'new_v', 'new_v_k_norm': 'new_v', 'new_v_forget_b': 'new_v', 'new_v_fox_w_in': 'new_v', 'new_v_q_norm': 'new_v', 'new_v_fox_w_out': 'new_v', 'new_v_out_ada_w': 'new_v', 'new_v_out_ada_b': 'new_v', 'new_v_out_norm': 'new_v'}


def _forward(args):
    return _fwd_reference(*[args[k] for k in FWD_PARAMS])


def _output_shape():
    def fwd():
        inp = _fwd_setup_inputs(0)
        return _fwd_reference(*[inp[k] for k in FWD_PARAMS])
    out = _jax.eval_shape(fwd)
    return out.shape, out.dtype

N_MICROBATCH = 1
ADAM_LR = 0.001
ADAM_B1 = 0.9
ADAM_B2 = 0.999
ADAM_EPS = 1e-08
ADAM_WD = 0.01
ADAM_STEP = 10
PER_EXAMPLE_BATCH_AXIS = {'x': 0, 'c': 0, 'loss_target': 0}
SHARED_INPUTS = []
_WEIGHT_DTYPES = {'ada_w': _jnp.float32, 'ada_b': _jnp.float32, 'norm_mix': _jnp.float32, 'norm_ffn': _jnp.float32, 'ffn_w_in': _jnp.float32, 'ffn_w_out': _jnp.float32, 'gdn_w_in': _jnp.float32, 'gdn_conv': _jnp.float32, 'gdn_a_log': _jnp.float32, 'gdn_dt_bias': _jnp.float32, 'gdn_norm': _jnp.float32, 'gdn_w_out': _jnp.float32, 'kv_ada_w': _jnp.float32, 'kv_ada_b': _jnp.float32, 'kv_norm': _jnp.float32, 'kv_w': _jnp.float32, 'k_norm': _jnp.float32, 'forget_b': _jnp.float32, 'fox_w_in': _jnp.float32, 'q_norm': _jnp.float32, 'fox_w_out': _jnp.float32, 'out_ada_w': _jnp.float32, 'out_ada_b': _jnp.float32, 'out_norm': _jnp.float32}
MOMENT_SCALE = {'ada_w': 2.922560e-01, 'ada_b': 5.123020e-01, 'norm_mix': 3.595777e-02, 'norm_ffn': 5.155278e-02, 'ffn_w_in': 3.369267e-02, 'ffn_w_out': 6.672852e-02, 'gdn_w_in': 4.192948e-02, 'gdn_conv': 5.557374e-02, 'gdn_a_log': 1.489733e-01, 'gdn_dt_bias': 1.487221e-01, 'gdn_norm': 6.628352e-01, 'gdn_w_out': 1.551780e-01, 'kv_ada_w': 2.106612e-01, 'kv_ada_b': 3.656859e-01, 'kv_norm': 2.044281e-02, 'kv_w': 1.940792e-01, 'k_norm': 1.762167e-02, 'forget_b': 1.559158e-01, 'fox_w_in': 6.597317e-03, 'q_norm': 1.756366e-02, 'fox_w_out': 1.475300e-01, 'out_ada_w': 6.752584e+00, 'out_ada_b': 2.307191e+01, 'out_norm': 3.601103e+01}


def _to_microbatches(a, axis):
    t = _jnp.moveaxis(a, axis, 0)
    t = t.reshape((N_MICROBATCH, t.shape[0] // N_MICROBATCH) + t.shape[1:])
    return _jnp.moveaxis(t, 1, axis + 1)


def setup_inputs(seed: int = 0) -> dict:
    inp = _fwd_setup_inputs(seed)
    key = _jax.random.fold_in(_jax.random.key(seed), 7919)
    shape, _ = _output_shape()
    out = dict(inp)
    out["loss_target"] = _jax.random.normal(_jax.random.fold_in(key, 0), shape, _jnp.float32)
    for i, name in enumerate(TWIN_WEIGHTS):
        w = inp[name].astype(_jnp.float32)
        if MOMENT_SCALE is None:
            s = _jnp.sqrt(_jnp.mean(_jnp.square(w)) + 1e-30)
        else:
            s = MOMENT_SCALE[name]
        km, kv = _jax.random.split(_jax.random.fold_in(key, i + 1))
        out[name] = w
        out["m_" + name] = s * _jax.random.normal(km, w.shape, _jnp.float32)
        out["v_" + name] = (s * s) * _jax.random.uniform(kv, w.shape, _jnp.float32, 0.5, 1.5)
    if N_MICROBATCH > 1:
        for name, axis in PER_EXAMPLE_BATCH_AXIS.items():
            out[name] = _to_microbatches(out[name], axis)
    return {'x': out['x'], 'c': out['c'], 'ada_w': out['ada_w'], 'ada_b': out['ada_b'], 'norm_mix': out['norm_mix'], 'norm_ffn': out['norm_ffn'], 'ffn_w_in': out['ffn_w_in'], 'ffn_w_out': out['ffn_w_out'], 'gdn_w_in': out['gdn_w_in'], 'gdn_conv': out['gdn_conv'], 'gdn_a_log': out['gdn_a_log'], 'gdn_dt_bias': out['gdn_dt_bias'], 'gdn_norm': out['gdn_norm'], 'gdn_w_out': out['gdn_w_out'], 'kv_ada_w': out['kv_ada_w'], 'kv_ada_b': out['kv_ada_b'], 'kv_norm': out['kv_norm'], 'kv_w': out['kv_w'], 'k_norm': out['k_norm'], 'forget_b': out['forget_b'], 'fox_w_in': out['fox_w_in'], 'q_norm': out['q_norm'], 'fox_w_out': out['fox_w_out'], 'out_ada_w': out['out_ada_w'], 'out_ada_b': out['out_ada_b'], 'out_norm': out['out_norm'], 'loss_target': out['loss_target'], 'm_ada_w': out['m_ada_w'], 'm_ada_b': out['m_ada_b'], 'm_norm_mix': out['m_norm_mix'], 'm_norm_ffn': out['m_norm_ffn'], 'm_ffn_w_in': out['m_ffn_w_in'], 'm_ffn_w_out': out['m_ffn_w_out'], 'm_gdn_w_in': out['m_gdn_w_in'], 'm_gdn_conv': out['m_gdn_conv'], 'm_gdn_a_log': out['m_gdn_a_log'], 'm_gdn_dt_bias': out['m_gdn_dt_bias'], 'm_gdn_norm': out['m_gdn_norm'], 'm_gdn_w_out': out['m_gdn_w_out'], 'm_kv_ada_w': out['m_kv_ada_w'], 'm_kv_ada_b': out['m_kv_ada_b'], 'm_kv_norm': out['m_kv_norm'], 'm_kv_w': out['m_kv_w'], 'm_k_norm': out['m_k_norm'], 'm_forget_b': out['m_forget_b'], 'm_fox_w_in': out['m_fox_w_in'], 'm_q_norm': out['m_q_norm'], 'm_fox_w_out': out['m_fox_w_out'], 'm_out_ada_w': out['m_out_ada_w'], 'm_out_ada_b': out['m_out_ada_b'], 'm_out_norm': out['m_out_norm'], 'v_ada_w': out['v_ada_w'], 'v_ada_b': out['v_ada_b'], 'v_norm_mix': out['v_norm_mix'], 'v_norm_ffn': out['v_norm_ffn'], 'v_ffn_w_in': out['v_ffn_w_in'], 'v_ffn_w_out': out['v_ffn_w_out'], 'v_gdn_w_in': out['v_gdn_w_in'], 'v_gdn_conv': out['v_gdn_conv'], 'v_gdn_a_log': out['v_gdn_a_log'], 'v_gdn_dt_bias': out['v_gdn_dt_bias'], 'v_gdn_norm': out['v_gdn_norm'], 'v_gdn_w_out': out['v_gdn_w_out'], 'v_kv_ada_w': out['v_kv_ada_w'], 'v_kv_ada_b': out['v_kv_ada_b'], 'v_kv_norm': out['v_kv_norm'], 'v_kv_w': out['v_kv_w'], 'v_k_norm': out['v_k_norm'], 'v_forget_b': out['v_forget_b'], 'v_fox_w_in': out['v_fox_w_in'], 'v_q_norm': out['v_q_norm'], 'v_fox_w_out': out['v_fox_w_out'], 'v_out_ada_w': out['v_out_ada_w'], 'v_out_ada_b': out['v_out_ada_b'], 'v_out_norm': out['v_out_norm']}


def _loss(weights, diff, rest, loss_target):
    with _jax.named_scope("forward"):
        args = {**rest, TWIN_DIFF_INPUT: diff, **{k: w.astype(_WEIGHT_DTYPES[k]) for k, w in weights.items()}}
        y = _forward(args)
    with _jax.named_scope("loss_head"):
        err = _jnp.square(y.astype(_jnp.float32) - loss_target)
        return 0.5 * _jnp.sum(_jnp.mean(err, axis=-1)) if err.ndim else 0.5 * err


def _adamw(w, g, m, v):
    m = ADAM_B1 * m + (1.0 - ADAM_B1) * g
    v = ADAM_B2 * v + (1.0 - ADAM_B2) * _jnp.square(g)
    m_hat = m / (1.0 - ADAM_B1 ** ADAM_STEP)
    v_hat = v / (1.0 - ADAM_B2 ** ADAM_STEP)
    delta = -ADAM_LR * (m_hat / (_jnp.sqrt(v_hat) + ADAM_EPS) + ADAM_WD * w)
    return delta, m, v


def reference(x, c, ada_w, ada_b, norm_mix, norm_ffn, ffn_w_in, ffn_w_out, gdn_w_in, gdn_conv, gdn_a_log, gdn_dt_bias, gdn_norm, gdn_w_out, kv_ada_w, kv_ada_b, kv_norm, kv_w, k_norm, forget_b, fox_w_in, q_norm, fox_w_out, out_ada_w, out_ada_b, out_norm, loss_target, m_ada_w, m_ada_b, m_norm_mix, m_norm_ffn, m_ffn_w_in, m_ffn_w_out, m_gdn_w_in, m_gdn_conv, m_gdn_a_log, m_gdn_dt_bias, m_gdn_norm, m_gdn_w_out, m_kv_ada_w, m_kv_ada_b, m_kv_norm, m_kv_w, m_k_norm, m_forget_b, m_fox_w_in, m_q_norm, m_fox_w_out, m_out_ada_w, m_out_ada_b, m_out_norm, v_ada_w, v_ada_b, v_norm_mix, v_norm_ffn, v_ffn_w_in, v_ffn_w_out, v_gdn_w_in, v_gdn_conv, v_gdn_a_log, v_gdn_dt_bias, v_gdn_norm, v_gdn_w_out, v_kv_ada_w, v_kv_ada_b, v_kv_norm, v_kv_w, v_k_norm, v_forget_b, v_fox_w_in, v_q_norm, v_fox_w_out, v_out_ada_w, v_out_ada_b, v_out_norm):
    given = dict(x=x, c=c, ada_w=ada_w, ada_b=ada_b, norm_mix=norm_mix, norm_ffn=norm_ffn, ffn_w_in=ffn_w_in, ffn_w_out=ffn_w_out, gdn_w_in=gdn_w_in, gdn_conv=gdn_conv, gdn_a_log=gdn_a_log, gdn_dt_bias=gdn_dt_bias, gdn_norm=gdn_norm, gdn_w_out=gdn_w_out, kv_ada_w=kv_ada_w, kv_ada_b=kv_ada_b, kv_norm=kv_norm, kv_w=kv_w, k_norm=k_norm, forget_b=forget_b, fox_w_in=fox_w_in, q_norm=q_norm, fox_w_out=fox_w_out, out_ada_w=out_ada_w, out_ada_b=out_ada_b, out_norm=out_norm, loss_target=loss_target, m_ada_w=m_ada_w, m_ada_b=m_ada_b, m_norm_mix=m_norm_mix, m_norm_ffn=m_norm_ffn, m_ffn_w_in=m_ffn_w_in, m_ffn_w_out=m_ffn_w_out, m_gdn_w_in=m_gdn_w_in, m_gdn_conv=m_gdn_conv, m_gdn_a_log=m_gdn_a_log, m_gdn_dt_bias=m_gdn_dt_bias, m_gdn_norm=m_gdn_norm, m_gdn_w_out=m_gdn_w_out, m_kv_ada_w=m_kv_ada_w, m_kv_ada_b=m_kv_ada_b, m_kv_norm=m_kv_norm, m_kv_w=m_kv_w, m_k_norm=m_k_norm, m_forget_b=m_forget_b, m_fox_w_in=m_fox_w_in, m_q_norm=m_q_norm, m_fox_w_out=m_fox_w_out, m_out_ada_w=m_out_ada_w, m_out_ada_b=m_out_ada_b, m_out_norm=m_out_norm, v_ada_w=v_ada_w, v_ada_b=v_ada_b, v_norm_mix=v_norm_mix, v_norm_ffn=v_norm_ffn, v_ffn_w_in=v_ffn_w_in, v_ffn_w_out=v_ffn_w_out, v_gdn_w_in=v_gdn_w_in, v_gdn_conv=v_gdn_conv, v_gdn_a_log=v_gdn_a_log, v_gdn_dt_bias=v_gdn_dt_bias, v_gdn_norm=v_gdn_norm, v_gdn_w_out=v_gdn_w_out, v_kv_ada_w=v_kv_ada_w, v_kv_ada_b=v_kv_ada_b, v_kv_norm=v_kv_norm, v_kv_w=v_kv_w, v_k_norm=v_k_norm, v_forget_b=v_forget_b, v_fox_w_in=v_fox_w_in, v_q_norm=v_q_norm, v_fox_w_out=v_fox_w_out, v_out_ada_w=v_out_ada_w, v_out_ada_b=v_out_ada_b, v_out_norm=v_out_norm)
    weights = {n: given[n] for n in TWIN_WEIGHTS}
    shared = {n: given[n] for n in SHARED_INPUTS}
    per_example = {n: given[n] for n in ['x', 'c']}
    grad_fn = _jax.value_and_grad(_loss, argnums=(0, 1))

    def one_microbatch(ex, loss_target):
        ex = dict(ex)
        diff = ex.pop(TWIN_DIFF_INPUT)
        return grad_fn(weights, diff, {**shared, **ex}, loss_target)

    if N_MICROBATCH == 1:
        loss, (grad_w, grad_x) = one_microbatch(per_example, given["loss_target"])
    else:
        def body(carry, xs):
            loss_sum, grad_sum = carry
            l_k, (gw_k, gx_k) = one_microbatch(xs[0], xs[1])
            with _jax.named_scope("update"):
                return (loss_sum + l_k, _jax.tree.map(_jnp.add, grad_sum, gw_k)), gx_k

        init = (_jnp.zeros((), _jnp.float32), _jax.tree.map(_jnp.zeros_like, weights))
        (loss, grad_w), grad_x = _jax.lax.scan(body, init, (per_example, given["loss_target"]))
    with _jax.named_scope("update"):
        delta_w, new_m, new_v = {}, {}, {}
        for n in TWIN_WEIGHTS:
            delta_w[n], new_m[n], new_v[n] = _adamw(weights[n], grad_w[n], given["m_" + n], given["v_" + n])
    return (loss, grad_x, *[grad_w[n] for n in TWIN_WEIGHTS], *[delta_w[n] for n in TWIN_WEIGHTS],
            *[new_m[n] for n in TWIN_WEIGHTS], *[new_v[n] for n in TWIN_WEIGHTS])
```

```python
import functools

import jax
import jax.numpy as jnp
from jax import lax
from jax.experimental import pallas as pl
from jax.experimental.pallas import tpu as pltpu

F32 = jnp.float32
BF16 = jnp.bfloat16
MESH = pl.DeviceIdType.MESH
N_DEV = 8

D_MODEL = 2048
SEQ = 8192
GDN_QK_HEADS = 16
GDN_V_HEADS = 32
GDN_HEAD_DIM = 128
GDN_QK_DIM = GDN_QK_HEADS * GDN_HEAD_DIM
GDN_V_DIM = GDN_V_HEADS * GDN_HEAD_DIM
CONV_DIM = 2 * GDN_QK_DIM + GDN_V_DIM
GDN_MAIN = CONV_DIM + GDN_V_DIM
GDN_PROJ = GDN_MAIN + 2 * GDN_V_HEADS
CONV_K = 4
GDN_CHUNK = 64
FOX_HEADS = 16
FOX_KV_HEADS = 2
FOX_GROUP = FOX_HEADS // FOX_KV_HEADS
FOX_HEAD_DIM = 256
FOX_Q_DIM = FOX_HEADS * FOX_HEAD_DIM
FOX_KV_DIM = FOX_KV_HEADS * FOX_HEAD_DIM
KV_PROJ = 2 * FOX_KV_DIM + FOX_HEADS
FFN_HIDDEN = 5632
NORM_EPS = 1e-6

ADAM_LR = 0.001
ADAM_B1 = 0.9
ADAM_B2 = 0.999
ADAM_EPS = 1e-08
ADAM_WD = 0.01
ADAM_STEP = 10

LANES = 128
VMEM_LIMIT = 48 * 1024 * 1024
PACK_COLS = 1024
ATT_BLOCK = 512
NEG = -1e30


def _pick(n, cands):
    for c in cands:
        if n % c == 0:
            return c
    return n


def _cparams(sem):
    return pltpu.CompilerParams(dimension_semantics=sem, vmem_limit_bytes=VMEM_LIMIT)


def _matmul(a, b, *, trans_a=False, trans_b=False, name):
    if trans_a:
        kdim, m = a.shape
    else:
        m, kdim = a.shape
    if trans_b:
        n, kb = b.shape
    else:
        kb, n = b.shape
    assert kdim == kb, (a.shape, b.shape)
    tm = _pick(m, (1024, 512, 256, 128))
    tn = _pick(n, (1408, 1024, 512, 256, 128))
    tk = _pick(kdim, (512, 256, 128))
    nk = kdim // tk
    dims = (((0 if trans_a else 1,), (1 if trans_b else 0,)), ((), ()))

    def body(a_ref, b_ref, o_ref, acc_ref):
        k = pl.program_id(2)

        @pl.when(k == 0)
        def _():
            acc_ref[...] = jnp.zeros_like(acc_ref)

        acc_ref[...] += lax.dot_general(a_ref[...].astype(BF16), b_ref[...].astype(BF16), dims,
                                        preferred_element_type=F32)

        @pl.when(k == nk - 1)
        def _():
            o_ref[...] = acc_ref[...]

    a_spec = (pl.BlockSpec((tk, tm), lambda i, j, k: (k, i)) if trans_a
              else pl.BlockSpec((tm, tk), lambda i, j, k: (i, k)))
    b_spec = (pl.BlockSpec((tn, tk), lambda i, j, k: (j, k)) if trans_b
              else pl.BlockSpec((tk, tn), lambda i, j, k: (k, j)))
    return pl.pallas_call(
        body, name=name,
        grid=(m // tm, n // tn, nk),
        in_specs=[a_spec, b_spec],
        out_specs=pl.BlockSpec((tm, tn), lambda i, j, k: (i, j)),
        out_shape=jax.ShapeDtypeStruct((m, n), F32),
        scratch_shapes=[pltpu.VMEM((tm, tn), F32)],
        compiler_params=_cparams(("parallel", "parallel", "arbitrary")),
    )(a, b)


def _make_mm(tag):
    @jax.custom_vjp
    def mm(a, w, carrier):
        return _matmul(a, w, name=f"mm_{tag}")

    def fwd(a, w, carrier):
        return _matmul(a, w, name=f"mm_{tag}"), (a, w)

    def bwd(res, g):
        a, w = res
        da = _matmul(g, w, trans_b=True, name=f"mm_{tag}_da")
        dw = _matmul(a, g, trans_a=True, name=f"mm_{tag}_dw")
        return da, jnp.zeros_like(w), dw

    mm.defvjp(fwd, bwd)
    return mm


_NT = (((1,), (1,)), ((), ()))
_FOX_SCALE = FOX_HEAD_DIM ** -0.5


def _fox_fwd_call(q, kb, vb, f_col, f_row):
    length = q.shape[0]
    t = ATT_BLOCK
    nq = length // t
    hd = FOX_HEAD_DIM

    def body(q_ref, k_ref, v_ref, fq_ref, fk_ref, o_ref, lse_ref, m_sc, l_sc, acc_sc):
        i = pl.program_id(1)
        qv = (q_ref[...] * _FOX_SCALE).astype(BF16)
        fq = fq_ref[0]
        m_sc[...] = jnp.full_like(m_sc, NEG)
        l_sc[...] = jnp.zeros_like(l_sc)
        acc_sc[...] = jnp.zeros_like(acc_sc)

        def block(j, masked):
            off = pl.multiple_of(j * t, t)
            kj = k_ref[pl.ds(off, t), :]
            vj = v_ref[pl.ds(off, t), :]
            s = lax.dot_general(qv, kj, _NT, preferred_element_type=F32)
            s = s + fq - fk_ref[0, :, pl.ds(off, t)]
            if masked:
                r = lax.broadcasted_iota(jnp.int32, (t, t), 0)
                c = lax.broadcasted_iota(jnp.int32, (t, t), 1)
                s = jnp.where(r >= c, s, NEG)
            m_old = m_sc[...]
            m_new = jnp.maximum(m_old, jnp.max(s, axis=1, keepdims=True))
            alpha = jnp.exp(m_old - m_new)
            p = jnp.exp(s - m_new)
            l_sc[...] = alpha * l_sc[...] + jnp.sum(p, axis=1, keepdims=True)
            acc_sc[...] = alpha * acc_sc[...] + jnp.dot(p.astype(BF16), vj, preferred_element_type=F32)
            m_sc[...] = m_new

        def loop_body(j, carry):
            block(j, False)
            return carry

        lax.fori_loop(0, i, loop_body, 0)
        block(i, True)
        o_ref[...] = acc_sc[...] / l_sc[...]
        lse_ref[0] = m_sc[...] + jnp.log(l_sc[...])

    return pl.pallas_call(
        body, name="fox_fwd",
        grid=(FOX_HEADS, nq),
        in_specs=[
            pl.BlockSpec((t, hd), lambda h, i: (i, h)),
            pl.BlockSpec((length, hd), lambda h, i: (0, h // FOX_GROUP)),
            pl.BlockSpec((length, hd), lambda h, i: (0, h // FOX_GROUP)),
            pl.BlockSpec((1, t, 1), lambda h, i: (h, i, 0)),
            pl.BlockSpec((1, 1, length), lambda h, i: (h, 0, 0)),
        ],
        out_specs=[
            pl.BlockSpec((t, hd), lambda h, i: (i, h)),
            pl.BlockSpec((1, t, 1), lambda h, i: (h, i, 0)),
        ],
        out_shape=[jax.ShapeDtypeStruct((length, FOX_Q_DIM), F32),
                   jax.ShapeDtypeStruct((FOX_HEADS, length, 1), F32)],
        scratch_shapes=[pltpu.VMEM((t, 1), F32), pltpu.VMEM((t, 1), F32), pltpu.VMEM((t, hd), F32)],
        compiler_params=_cparams(("parallel", "arbitrary")),
    )(q, kb, vb, f_col, f_row)


def _fox_dq_call(q, kb, vb, f_col, f_row, lse_col, do):
    length = q.shape[0]
    t = ATT_BLOCK
    nq = length // t
    hd = FOX_HEAD_DIM

    def body(q_ref, k_ref, v_ref, fq_ref, fk_ref, lse_ref, do_ref, dq_ref, dl_ref, acc_sc, dl_sc):
        i = pl.program_id(1)
        qv = (q_ref[...] * _FOX_SCALE).astype(BF16)
        dov = do_ref[...].astype(BF16)
        fq = fq_ref[0]
        lse = lse_ref[0]
        acc_sc[...] = jnp.zeros_like(acc_sc)
        dl_sc[...] = jnp.zeros_like(dl_sc)

        def p_dp(j, masked):
            off = pl.multiple_of(j * t, t)
            kj = k_ref[pl.ds(off, t), :]
            vj = v_ref[pl.ds(off, t), :]
            s = lax.dot_general(qv, kj, _NT, preferred_element_type=F32)
            s = s + fq - fk_ref[0, :, pl.ds(off, t)]
            if masked:
                r = lax.broadcasted_iota(jnp.int32, (t, t), 0)
                c = lax.broadcasted_iota(jnp.int32, (t, t), 1)
                s = jnp.where(r >= c, s, NEG)
            p = jnp.exp(s - lse)
            dp = lax.dot_general(dov, vj, _NT, preferred_element_type=F32)
            return p, dp, kj

        def row_term(j, masked):
            p, dp, _ = p_dp(j, masked)
            dl_sc[...] += jnp.sum(p * dp, axis=1, keepdims=True)

        def grad(j, masked):
            p, dp, kj = p_dp(j, masked)
            ds = p * (dp - dl_sc[...])
            acc_sc[...] += jnp.dot(ds.astype(BF16), kj, preferred_element_type=F32)

        def row_loop(j, carry):
            row_term(j, False)
            return carry

        def grad_loop(j, carry):
            grad(j, False)
            return carry

        lax.fori_loop(0, i, row_loop, 0)
        row_term(i, True)
        lax.fori_loop(0, i, grad_loop, 0)
        grad(i, True)
        dq_ref[...] = acc_sc[...] * _FOX_SCALE
        dl_ref[0] = dl_sc[...]

    col = pl.BlockSpec((1, t, 1), lambda h, i: (h, i, 0))
    return pl.pallas_call(
        body, name="fox_dq",
        grid=(FOX_HEADS, nq),
        in_specs=[
            pl.BlockSpec((t, hd), lambda h, i: (i, h)),
            pl.BlockSpec((length, hd), lambda h, i: (0, h // FOX_GROUP)),
            pl.BlockSpec((length, hd), lambda h, i: (0, h // FOX_GROUP)),
            col,
            pl.BlockSpec((1, 1, length), lambda h, i: (h, 0, 0)),
            col,
            pl.BlockSpec((t, hd), lambda h, i: (i, h)),
        ],
        out_specs=[pl.BlockSpec((t, hd), lambda h, i: (i, h)), col],
        out_shape=[jax.ShapeDtypeStruct((length, FOX_Q_DIM), F32),
                   jax.ShapeDtypeStruct((FOX_HEADS, length, 1), F32)],
        scratch_shapes=[pltpu.VMEM((t, hd), F32), pltpu.VMEM((t, 1), F32)],
        compiler_params=_cparams(("parallel", "arbitrary")),
    )(q, kb, vb, f_col, f_row, lse_col, do)


def _fox_dkv_call(qs, dob, kb, vb, f_col, f_row, lse_row, dl_row):
    length = qs.shape[0]
    t = ATT_BLOCK
    nq = length // t
    hd = FOX_HEAD_DIM

    def body(k_ref, v_ref, q_ref, do_ref, fk_ref, fq_ref, lse_ref, dl_ref,
             dk_ref, dv_ref, df_ref, dk_sc, dv_sc, df_sc):
        j = pl.program_id(1)
        kj = k_ref[...]
        vj = v_ref[...]
        fk = fk_ref[0]
        dk_sc[...] = jnp.zeros_like(dk_sc)
        dv_sc[...] = jnp.zeros_like(dv_sc)
        df_sc[...] = jnp.zeros_like(df_sc)

        def block(i, masked):
            off = pl.multiple_of(i * t, t)
            qi = q_ref[pl.ds(off, t), :]
            doi = do_ref[pl.ds(off, t), :]
            st = lax.dot_general(kj, qi, _NT, preferred_element_type=F32)
            st = st + fq_ref[0, :, pl.ds(off, t)] - fk
            if masked:
                r = lax.broadcasted_iota(jnp.int32, (t, t), 0)
                c = lax.broadcasted_iota(jnp.int32, (t, t), 1)
                st = jnp.where(c >= r, st, NEG)
            pt = jnp.exp(st - lse_ref[0, :, pl.ds(off, t)])
            dv_sc[...] += jnp.dot(pt.astype(BF16), doi, preferred_element_type=F32)
            dpt = lax.dot_general(vj, doi, _NT, preferred_element_type=F32)
            dst = pt * (dpt - dl_ref[0, :, pl.ds(off, t)])
            dk_sc[...] += jnp.dot(dst.astype(BF16), qi, preferred_element_type=F32)
            df_sc[...] += jnp.sum(dst, axis=1, keepdims=True)

        def loop_body(i, carry):
            block(i, False)
            return carry

        block(j, True)
        lax.fori_loop(j + 1, nq, loop_body, 0)
        dk_ref[...] = dk_sc[...]
        dv_ref[...] = dv_sc[...]
        df_ref[0] = -df_sc[...]

    row = pl.BlockSpec((1, 1, length), lambda h, j: (h, 0, 0))
    return pl.pallas_call(
        body, name="fox_dkv",
        grid=(FOX_HEADS, nq),
        in_specs=[
            pl.BlockSpec((t, hd), lambda h, j: (j, h // FOX_GROUP)),
            pl.BlockSpec((t, hd), lambda h, j: (j, h // FOX_GROUP)),
            pl.BlockSpec((length, hd), lambda h, j: (0, h)),
            pl.BlockSpec((length, hd), lambda h, j: (0, h)),
            pl.BlockSpec((1, t, 1), lambda h, j: (h, j, 0)),
            row, row, row,
        ],
        out_specs=[
            pl.BlockSpec((t, hd), lambda h, j: (j, h)),
            pl.BlockSpec((t, hd), lambda h, j: (j, h)),
            pl.BlockSpec((1, t, 1), lambda h, j: (h, j, 0)),
        ],
        out_shape=[jax.ShapeDtypeStruct((length, FOX_Q_DIM), F32),
                   jax.ShapeDtypeStruct((length, FOX_Q_DIM), F32),
                   jax.ShapeDtypeStruct((FOX_HEADS, length, 1), F32)],
        scratch_shapes=[pltpu.VMEM((t, hd), F32), pltpu.VMEM((t, hd), F32), pltpu.VMEM((t, 1), F32)],
        compiler_params=_cparams(("parallel", "arbitrary")),
    )(kb, vb, qs, dob, f_col, f_row, lse_row, dl_row)


@jax.custom_vjp
def _fox_attention(q, k, v, f):
    return _fox_attention_fwd(q, k, v, f)[0]


def _fox_attention_fwd(q, k, v, f):
    kb = k.astype(BF16)
    vb = v.astype(BF16)
    f_row = jnp.transpose(f)[:, None, :]
    f_col = jnp.transpose(f)[:, :, None]
    o, lse = _fox_fwd_call(q, kb, vb, f_col, f_row)
    return o, (q, kb, vb, f_col, f_row, lse)


def _fox_attention_bwd(res, do):
    q, kb, vb, f_col, f_row, lse = res
    length = q.shape[0]
    lse_row = jnp.transpose(lse, (0, 2, 1))
    dq, dl_col = _fox_dq_call(q, kb, vb, f_col, f_row, lse, do)
    dl_row = jnp.transpose(dl_col, (0, 2, 1))
    qs =(q * _FOX_SCALE).astype(BF16)
    dkh, dvh, dfk = _fox_dkv_call(qs, do.astype(BF16), kb, vb, f_col, f_row, lse_row, dl_row)
    dk = jnp.sum(dkh.reshape(length, FOX_KV_HEADS, FOX_GROUP, FOX_HEAD_DIM), axis=2).reshape(length, FOX_KV_DIM)
    dv = jnp.sum(dvh.reshape(length, FOX_KV_HEADS, FOX_GROUP, FOX_HEAD_DIM), axis=2).reshape(length, FOX_KV_DIM)
    df = jnp.transpose(dfk[:, :, 0])
    return dq, dk, dv, df


_fox_attention.defvjp(_fox_attention_fwd, _fox_attention_bwd)


def _peer(mx, my, mc, k):
    return (1 - mx if k & 4 else mx, 1 - my if k & 2 else my, 1 - mc if k & 1 else mc)


def _all_gather(x, *, name, in_vmem):
    space = pltpu.VMEM if in_vmem else pl.ANY

    def body(x_ref, out_ref, send_sems, recv_sems, local_sem):
        mx, my, mc = lax.axis_index("x"), lax.axis_index("y"), lax.axis_index("c")
        me = 4 * mx + 2 * my + mc
        mine = pltpu.make_async_copy(x_ref, out_ref.at[me], local_sem)
        mine.start()
        copies = []
        for k in range(1, N_DEV):
            cp = pltpu.make_async_remote_copy(
                src_ref=x_ref, dst_ref=out_ref.at[me],
                send_sem=send_sems.at[k - 1], recv_sem=recv_sems.at[k - 1],
                device_id=_peer(mx, my, mc, k), device_id_type=MESH)
            cp.start()
            copies.append(cp)
        for cp in copies:
            cp.wait()
        mine.wait()

    return pl.pallas_call(
        body, name=name,
        out_shape=jax.ShapeDtypeStruct((N_DEV,) + x.shape, x.dtype),
        in_specs=[pl.BlockSpec(memory_space=space)],
        out_specs=pl.BlockSpec(memory_space=space),
        scratch_shapes=[pltpu.SemaphoreType.DMA((N_DEV - 1,)), pltpu.SemaphoreType.DMA((N_DEV - 1,)),
                        pltpu.SemaphoreType.DMA],
    )(x)


def _all_to_all(x, *, name):
    def body(x_ref, out_ref, send_sems, recv_sems, local_sem):
        mx, my, mc = lax.axis_index("x"), lax.axis_index("y"), lax.axis_index("c")
        me = 4 * mx + 2 * my + mc
        mine = pltpu.make_async_copy(x_ref.at[me], out_ref.at[me], local_sem)
        mine.start()
        copies = []
        for k in range(1, N_DEV):
            px, py, pc = _peer(mx, my, mc, k)
            cp = pltpu.make_async_remote_copy(
                src_ref=x_ref.at[4 * px + 2 * py + pc], dst_ref=out_ref.at[me],
                send_sem=send_sems.at[k - 1], recv_sem=recv_sems.at[k - 1],
                device_id=(px, py, pc), device_id_type=MESH)
            cp.start()
            copies.append(cp)
        for cp in copies:
            cp.wait()
        mine.wait()

    return pl.pallas_call(
        body, name=name,
        out_shape=jax.ShapeDtypeStruct(x.shape, x.dtype),
        in_specs=[pl.BlockSpec(memory_space=pl.ANY)],
        out_specs=pl.BlockSpec(memory_space=pl.ANY),
        scratch_shapes=[pltpu.SemaphoreType.DMA((N_DEV - 1,)), pltpu.SemaphoreType.DMA((N_DEV - 1,)),
                        pltpu.SemaphoreType.DMA],
    )(x)


def _sum_slots(x, *, name):
    _, r, c = x.shape
    tr = _pick(r, (256, 128, 64, 32, 16, 8))

    def body(x_ref, o_ref):
        acc = x_ref[0]
        for s in range(1, N_DEV):
            acc = acc + x_ref[s]
        o_ref[...] = acc

    return pl.pallas_call(
        body, name=name,
        grid=(r // tr,),
        in_specs=[pl.BlockSpec((N_DEV, tr, c), lambda i: (0, i, 0))],
        out_specs=pl.BlockSpec((tr, c), lambda i: (i, 0)),
        out_shape=jax.ShapeDtypeStruct((r, c), F32),
        compiler_params=_cparams(("parallel",)),
    )(x)


def _adamw(w, g, m, v, *, name):
    shape = w.shape
    c = shape[-1]
    r = w.size // c
    w2, g2, m2, v2 = (t.reshape(r, c) for t in (w, g, m, v))
    tr = _pick(r, (128, 64, 32, 16, 8))

    def body(w_ref, g_ref, m_ref, v_ref, d_ref, mo_ref, vo_ref):
        gv = g_ref[...]
        mn = ADAM_B1 * m_ref[...] + (1.0 - ADAM_B1) * gv
        vn = ADAM_B2 * v_ref[...] + (1.0 - ADAM_B2) * (gv * gv)
        m_hat = mn / (1.0 - ADAM_B1 ** ADAM_STEP)
        v_hat = vn / (1.0 - ADAM_B2 ** ADAM_STEP)
        d_ref[...] = -ADAM_LR * (m_hat / (jnp.sqrt(v_hat) + ADAM_EPS) + ADAM_WD * w_ref[...])
        mo_ref[...] = mn
        vo_ref[...] = vn

    spec = pl.BlockSpec((tr, c), lambda i: (i, 0))
    outs = pl.pallas_call(
        body, name=name,
        grid=(r // tr,),
        in_specs=[spec] * 4,
        out_specs=[spec] * 3,
        out_shape=[jax.ShapeDtypeStruct((r, c), F32)] * 3,
        compiler_params=_cparams(("parallel",)),
    )(w2, g2, m2, v2)
    return tuple(o.reshape(shape) for o in outs)


def _rms_norm(x, w):
    y = x * lax.rsqrt(jnp.mean(x * x, axis=-1, keepdims=True) + NORM_EPS)
    return y * w


def _modulate(h, shift, scale):
    return h * (1 + scale) + shift


def _l2_normalize(x):
    return x * lax.rsqrt(jnp.sum(x * x, axis=-1, keepdims=True) + NORM_EPS)


def _causal_conv(x, w):
    width = w.shape[0]
    length = x.shape[0]
    xp = jnp.pad(x, ((width - 1, 0), (0, 0)))
    return sum(xp[i:i + length] * w[i] for i in range(width))


def _gated_delta_rule_chunked(q, k, v, g, beta):
    h, length, dk = q.shape
    dv = v.shape[-1]
    n = length // GDN_CHUNK
    blk = lambda t: t.reshape(h, n, GDN_CHUNK, *t.shape[2:])
    q = blk(q) * dk ** -0.5
    k = blk(k)
    v = blk(v)
    beta = blk(beta)
    g = jnp.cumsum(blk(g), axis=-1)
    causal = jnp.tril(jnp.ones((GDN_CHUNK, GDN_CHUNK), dtype=bool))
    strict = jnp.tril(jnp.ones((GDN_CHUNK, GDN_CHUNK), dtype=bool), k=-1)
    decay = jnp.exp(jnp.where(causal, g[..., :, None] - g[..., None, :], -jnp.inf))
    k_beta = k * beta[..., None]
    a_strict = jnp.where(strict, jnp.einsum('hnid,hnjd->hnij', k_beta, k) * decay, 0.0)
    eye = jnp.eye(GDN_CHUNK, dtype=q.dtype)
    rhs = jnp.concatenate([v * beta[..., None], k_beta * jnp.exp(g)[..., None]], axis=-1)
    sol = lax.linalg.triangular_solve(a_strict + eye, rhs, left_side=True, lower=True, unit_diagonal=True)
    u, w = sol[..., :dv], sol[..., dv:]
    attn = jnp.where(causal, jnp.einsum('hnid,hnjd->hnij', q, k) * decay, 0.0)
    g_last = g[..., -1]
    q_dec = q * jnp.exp(g)[..., None]
    k_dec = k * jnp.exp(g_last[..., None] - g)[..., None]

    def step(state, xs):
        q_c, k_c, u_c, w_c, attn_c, gl_c = xs
        v_new = u_c - jnp.einsum('hik,hkv->hiv', w_c, state)
        o = jnp.einsum('hik,hkv->hiv', q_c, state) + jnp.einsum('hij,hjv->hiv', attn_c, v_new)
        state = state * jnp.exp(gl_c)[..., None, None] + jnp.einsum('hik,hiv->hkv', k_c, v_new)
        return state, o

    xs = tuple(jnp.moveaxis(t, 1, 0) for t in (q_dec, k_dec, u, w, attn, g_last))
    s0 = jnp.zeros((h, dk, dv), F32)
    _, o = lax.scan(step, s0, xs)
    return jnp.moveaxis(o, 0, 1).reshape(h, length, dv)


_MM = {tag: _make_mm(tag) for tag in (
    "gdn_in", "gdn_ba", "gdn_out", "ffn_in0", "ffn_out0", "kv", "kv_f", "fox_in", "fox_out", "ffn_in1", "ffn_out1")}


def _swiglu(h, w_in, w_out, c_in, c_out, layer):
    gu = _MM[f"ffn_in{layer}"](h, w_in, c_in)
    gate, up = gu[:, :FFN_HIDDEN], gu[:, FFN_HIDDEN:]
    return _MM[f"ffn_out{layer}"](jax.nn.silu(gate) * up, w_out, c_out)


def _gated_deltanet(h, wts, car, conv_w, a_log, dt_bias, norm_w):
    length = h.shape[0]
    main = _MM["gdn_in"](h, wts["gdn_in"], car["gdn_in"])
    ba = _MM["gdn_ba"](h, wts["gdn_ba"], car["gdn_ba"])
    qkv, z = main[:, :CONV_DIM], main[:, CONV_DIM:]
    beta_logit, a = ba[:, :GDN_V_HEADS], ba[:, GDN_V_HEADS:2 * GDN_V_HEADS]
    qkv = jax.nn.silu(_causal_conv(qkv, conv_w))
    q, k, v = qkv[:, :GDN_QK_DIM], qkv[:, GDN_QK_DIM:2 * GDN_QK_DIM], qkv[:, 2 * GDN_QK_DIM:]
    rep = GDN_V_HEADS // GDN_QK_HEADS
    heads = lambda t, nh: t.reshape(length, nh, GDN_HEAD_DIM)
    q = jnp.repeat(_l2_normalize(heads(q, GDN_QK_HEADS)), rep, axis=1)
    k = jnp.repeat(_l2_normalize(heads(k, GDN_QK_HEADS)), rep, axis=1)
    v = heads(v, GDN_V_HEADS)
    beta = jax.nn.sigmoid(beta_logit)
    g = -jnp.exp(a_log) * jax.nn.softplus(a + dt_bias)
    tr = lambda t: jnp.swapaxes(t, 0, 1)
    o = _gated_delta_rule_chunked(tr(q), tr(k), tr(v), tr(g), tr(beta))
    o = _rms_norm(tr(o), norm_w) * jax.nn.silu(heads(z, GDN_V_HEADS))
    return _MM["gdn_out"](o.reshape(length, GDN_V_DIM), wts["gdn_out"], car["gdn_out"])


def _local_loss(x, mods, small, car, wts, target):
    split6 = lambda m: tuple(m[i * D_MODEL:(i + 1) * D_MODEL][None, :] for i in range(6))
    sh_m, sc_m, g_m, sh_f, sc_f, g_f = split6(mods["ada0"])
    h = _modulate(_rms_norm(x, small["norm_mix"][0]), sh_m, sc_m)
    y = _gated_deltanet(h, wts, car, small["gdn_conv"], small["gdn_a_log"][0], small["gdn_dt_bias"][0],
                        small["gdn_norm"][0])
    x = x + g_m * y
    h = _modulate(_rms_norm(x, small["norm_ffn"][0]), sh_f, sc_f)
    x = x + g_f * _swiglu(h, wts["ffn_in0"], wts["ffn_out0"], car["ffn_in0"], car["ffn_out0"], 0)
    length = x.shape[0]
    kv_shift, kv_scale = mods["kvada"][None, :D_MODEL], mods["kvada"][None, D_MODEL:]
    hk = _modulate(_rms_norm(x, small["kv_norm"]), kv_shift, kv_scale)
    kv = _MM["kv"](hk, wts["kv"], car["kv"])
    f_logit = _MM["kv_f"](hk, wts["kv_f"], car["kv_f"])[:, :FOX_HEADS]
    k_sh = _rms_norm(kv[:, :FOX_KV_DIM].reshape(length, FOX_KV_HEADS, FOX_HEAD_DIM), small["k_norm"])
    k_sh = k_sh.reshape(length, FOX_KV_DIM)
    v_sh = kv[:, FOX_KV_DIM:]
    f_cum = jnp.cumsum(jax.nn.log_sigmoid(f_logit + small["forget_b"]), axis=0)
    sh_m, sc_m, g_m, sh_f, sc_f, g_f = split6(mods["ada1"])
    h = _modulate(_rms_norm(x, small["norm_mix"][1]), sh_m, sc_m)
    qg = _MM["fox_in"](h, wts["fox_in"], car["fox_in"])
    q = _rms_norm(qg[:, :FOX_Q_DIM].reshape(length, FOX_HEADS, FOX_HEAD_DIM), small["q_norm"][0])
    o = _fox_attention(q.reshape(length, FOX_Q_DIM), k_sh, v_sh, f_cum)
    y = _MM["fox_out"](o * jax.nn.sigmoid(qg[:, FOX_Q_DIM:]), wts["fox_out"], car["fox_out"])
    x = x + g_m * y
    h = _modulate(_rms_norm(x, small["norm_ffn"][1]), sh_f, sc_f)
    x = x + g_f * _swiglu(h, wts["ffn_in1"], wts["ffn_out1"], car["ffn_in1"], car["ffn_out1"], 1)
    out = _modulate(_rms_norm(x, small["out_norm"]), mods["outada"][None, :D_MODEL], mods["outada"][None, D_MODEL:])
    err = jnp.square(out - target)
    return 0.5 * jnp.sum(jnp.mean(err, axis=-1))


def _pad_rows(a, rows):
    return jnp.pad(a, ((0, rows - a.shape[0]), (0, 0)))


def _pack_flat(parts, total):
    flat = jnp.concatenate([p.reshape(-1) for p in parts])
    return jnp.pad(flat, (0, total - flat.shape[0]))


_SMALL_NAMES = ("ada_b", "norm_mix", "norm_ffn", "gdn_a_log", "gdn_dt_bias", "gdn_norm", "kv_ada_b", "kv_norm",
                "k_norm", "forget_b", "q_norm", "out_ada_b", "out_norm")


def kernel(x, c, ada_w, ada_b, norm_mix, norm_ffn, ffn_w_in, ffn_w_out, gdn_w_in, gdn_conv, gdn_a_log, gdn_dt_bias, gdn_norm, gdn_w_out, kv_ada_w, kv_ada_b, kv_norm, kv_w, k_norm, forget_b, fox_w_in, q_norm, fox_w_out, out_ada_w, out_ada_b, out_norm, loss_target, m_ada_w, m_ada_b, m_norm_mix, m_norm_ffn, m_ffn_w_in, m_ffn_w_out, m_gdn_w_in, m_gdn_conv, m_gdn_a_log, m_gdn_dt_bias, m_gdn_norm, m_gdn_w_out, m_kv_ada_w, m_kv_ada_b, m_kv_norm, m_kv_w, m_k_norm, m_forget_b, m_fox_w_in, m_q_norm, m_fox_w_out, m_out_ada_w, m_out_ada_b, m_out_norm, v_ada_w, v_ada_b, v_norm_mix, v_norm_ffn, v_ffn_w_in, v_ffn_w_out, v_gdn_w_in, v_gdn_conv, v_gdn_a_log, v_gdn_dt_bias, v_gdn_norm, v_gdn_w_out, v_kv_ada_w, v_kv_ada_b, v_kv_norm, v_kv_w, v_k_norm, v_forget_b, v_fox_w_in, v_q_norm, v_fox_w_out, v_out_ada_w, v_out_ada_b, v_out_norm):
    w_in = dict(ada_w=ada_w, ada_b=ada_b, norm_mix=norm_mix, norm_ffn=norm_ffn, ffn_w_in=ffn_w_in, ffn_w_out=ffn_w_out, gdn_w_in=gdn_w_in, gdn_conv=gdn_conv, gdn_a_log=gdn_a_log, gdn_dt_bias=gdn_dt_bias, gdn_norm=gdn_norm, gdn_w_out=gdn_w_out, kv_ada_w=kv_ada_w, kv_ada_b=kv_ada_b, kv_norm=kv_norm, kv_w=kv_w, k_norm=k_norm, forget_b=forget_b, fox_w_in=fox_w_in, q_norm=q_norm, fox_w_out=fox_w_out, out_ada_w=out_ada_w, out_ada_b=out_ada_b, out_norm=out_norm)
    m_in = dict(ada_w=m_ada_w, ada_b=m_ada_b, norm_mix=m_norm_mix, norm_ffn=m_norm_ffn, ffn_w_in=m_ffn_w_in, ffn_w_out=m_ffn_w_out, gdn_w_in=m_gdn_w_in, gdn_conv=m_gdn_conv, gdn_a_log=m_gdn_a_log, gdn_dt_bias=m_gdn_dt_bias, gdn_norm=m_gdn_norm, gdn_w_out=m_gdn_w_out, kv_ada_w=m_kv_ada_w, kv_ada_b=m_kv_ada_b, kv_norm=m_kv_norm, kv_w=m_kv_w, k_norm=m_k_norm, forget_b=m_forget_b, fox_w_in=m_fox_w_in, q_norm=m_q_norm, fox_w_out=m_fox_w_out, out_ada_w=m_out_ada_w, out_ada_b=m_out_ada_b, out_norm=m_out_norm)
    v_in = dict(ada_w=v_ada_w, ada_b=v_ada_b, norm_mix=v_norm_mix, norm_ffn=v_norm_ffn, ffn_w_in=v_ffn_w_in, ffn_w_out=v_ffn_w_out, gdn_w_in=v_gdn_w_in, gdn_conv=v_gdn_conv, gdn_a_log=v_gdn_a_log, gdn_dt_bias=v_gdn_dt_bias, gdn_norm=v_gdn_norm, gdn_w_out=v_gdn_w_out, kv_ada_w=v_kv_ada_w, kv_ada_b=v_kv_ada_b, kv_norm=v_kv_norm, kv_w=v_kv_w, k_norm=v_k_norm, forget_b=v_forget_b, fox_w_in=v_fox_w_in, q_norm=v_q_norm, fox_w_out=v_fox_w_out, out_ada_w=v_out_ada_w, out_ada_b=v_out_ada_b, out_norm=v_out_norm)
    names = list(w_in)
    me = 4 * lax.axis_index("x") + 2 * lax.axis_index("y") + lax.axis_index("c")
    d = D_MODEL

    n_conv = CONV_K * (CONV_DIM // N_DEV)
    first = _pack_flat([c, gdn_conv], d + n_conv).reshape(-1, LANES)
    first_all = _all_gather(first, name="ag_cond_conv", in_vmem=True).reshape(N_DEV, d + n_conv)
    cond_all = jax.nn.silu(first_all[:, :d])
    conv_full = first_all[:, d:].reshape(N_DEV, CONV_K, CONV_DIM // N_DEV)
    conv_full = jnp.transpose(conv_full, (1, 0, 2)).reshape(CONV_K, CONV_DIM)

    ada_cat = jnp.concatenate([ada_w[0], ada_w[1], kv_ada_w, out_ada_w], axis=1)
    n_ada = ada_cat.shape[1]
    cond_pad = _pad_rows(cond_all, LANES)
    mods_part = _matmul(cond_pad, ada_cat, name="mm_ada")[:N_DEV]
    mods_all = _all_gather(mods_part.reshape(-1, LANES), name="ag_mods", in_vmem=True)
    mods_all = mods_all.reshape(N_DEV, N_DEV, n_ada)
    mine = lax.dynamic_index_in_dim(mods_all, me, axis=1, keepdims=False)
    s6 = 6 * d // N_DEV
    s2 = 2 * d // N_DEV
    mods = {
        "ada0": mine[:, :s6].reshape(-1) + ada_b[0],
        "ada1": mine[:, s6:2 * s6].reshape(-1) + ada_b[1],
        "kvada": mine[:, 2 * s6:2 * s6 + s2].reshape(-1) + kv_ada_b,
        "outada": mine[:, 2 * s6 + s2:].reshape(-1) + out_ada_b,
    }

    big = [ffn_w_in[0], ffn_w_in[1], ffn_w_out[0], ffn_w_out[1], gdn_w_in[0], gdn_w_out[0], kv_w, fox_w_in[0],
           fox_w_out[0]]
    col_sharded = [True, True, False, False, True, False, False, True, False]
    sizes = [b.size for b in big]
    pack_rows = -(-sum(sizes) // (PACK_COLS * 512)) * 512
    total = pack_rows * PACK_COLS
    packed = _pack_flat([b.astype(BF16) for b in big], total).reshape(pack_rows, PACK_COLS)
    gathered = _all_gather(packed, name="ag_weights", in_vmem=False).reshape(N_DEV, total)
    full = []
    off = 0
    for b, cs, sz in zip(big, col_sharded, sizes):
        blk = gathered[:, off:off + sz].reshape((N_DEV,) + b.shape)
        off += sz
        if cs:
            full.append(jnp.transpose(blk, (1, 0, 2)).reshape(b.shape[0], N_DEV * b.shape[1]))
        else:
            full.append(blk.reshape(N_DEV * b.shape[0], b.shape[1]))
    ffn_in0, ffn_in1, ffn_out0, ffn_out1, gdn_in_f, gdn_out_f, kv_f_full, fox_in_f, fox_out_f = full
    wts = {
        "ffn_in0": ffn_in0, "ffn_in1": ffn_in1, "ffn_out0": ffn_out0, "ffn_out1": ffn_out1,
        "gdn_in": gdn_in_f[:, :GDN_MAIN],
        "gdn_ba": jnp.pad(gdn_in_f[:, GDN_MAIN:], ((0, 0), (0, LANES - 2 * GDN_V_HEADS))),
        "gdn_out": gdn_out_f,
        "kv": kv_f_full[:, :2 * FOX_KV_DIM],
        "kv_f": jnp.pad(kv_f_full[:, 2 * FOX_KV_DIM:], ((0, 0), (0, LANES - FOX_HEADS))),
        "fox_in": fox_in_f, "fox_out": fox_out_f,
    }
    car = {k: jnp.zeros(w.shape, F32) for k, w in wts.items()}
    small = {n: w_in[n] for n in _SMALL_NAMES if n not in ("ada_b", "kv_ada_b", "out_ada_b")}
    small["gdn_conv"] = conv_full

    loss_fn = functools.partial(_local_loss, wts=wts, target=loss_target[0])
    loss_local, vjp = jax.vjp(loss_fn, x[0], mods, small, car)
    gx, gmods, gsmall, gcar = vjp(jnp.ones((), F32))
    loss = lax.psum(loss_local, ("x", "y", "c"))

    gsm = dict(gsmall)
    gsm["ada_b"] = jnp.stack([gmods["ada0"], gmods["ada1"]])
    gsm["kv_ada_b"] = gmods["kvada"]
    gsm["out_ada_b"] = gmods["outada"]
    small_sizes = [w_in[n].size for n in _SMALL_NAMES]
    n_small = sum(small_sizes) + CONV_K * CONV_DIM
    small_rows = -(-n_small // (LANES * 8)) * 8
    vec = _pack_flat([gsm[n] for n in _SMALL_NAMES] + [gsm["gdn_conv"]], small_rows * LANES)
    vec_all = _all_gather(vec.reshape(small_rows, LANES), name="ag_small_grads", in_vmem=True)
    vec_sum = _sum_slots(vec_all, name="sum_small_grads").reshape(-1)
    grads = {}
    off = 0
    for n, sz in zip(_SMALL_NAMES, small_sizes):
        grads[n] = vec_sum[off:off + sz].reshape(w_in[n].shape)
        off += sz
    conv_sum = vec_sum[off:off + CONV_K * CONV_DIM].reshape(CONV_K, N_DEV, CONV_DIM // N_DEV)
    grads["gdn_conv"] = lax.dynamic_index_in_dim(conv_sum, me, axis=1, keepdims=False)[None]

    vec_flat = vec_all.reshape(N_DEV, -1)
    d_ada = vec_flat[:, :2 * 6 * d].reshape(N_DEV, 2, N_DEV, s6)
    o_kv = sum(small_sizes[:6])
    d_kvada = vec_flat[:, o_kv:o_kv + 2 * d].reshape(N_DEV, N_DEV, s2)
    o_out = sum(small_sizes[:11])
    d_outada = vec_flat[:, o_out:o_out + 2 * d].reshape(N_DEV, N_DEV, s2)
    pick = lambda t, axis: lax.dynamic_index_in_dim(t, me, axis=axis, keepdims=False)
    d_cat = jnp.concatenate([pick(d_ada[:, 0], 1), pick(d_ada[:, 1], 1), pick(d_kvada, 1), pick(d_outada, 1)],
                            axis=1)
    g_ada_cat = _matmul(cond_pad, _pad_rows(d_cat, LANES), trans_a=True, name="mm_ada_dw")
    grads["ada_w"] = jnp.stack([g_ada_cat[:, :s6], g_ada_cat[:, s6:2 * s6]])
    grads["kv_ada_w"] = g_ada_cat[:, 2 * s6:2 * s6 + s2]
    grads["out_ada_w"] = g_ada_cat[:, 2 * s6 + s2:]

    g_gdn_in = jnp.concatenate([gcar["gdn_in"], gcar["gdn_ba"][:, :2 * GDN_V_HEADS]], axis=1)
    g_kv = jnp.concatenate([gcar["kv"], gcar["kv_f"][:, :FOX_HEADS]], axis=1)
    g_full = [gcar["ffn_in0"], gcar["ffn_in1"], gcar["ffn_out0"], gcar["ffn_out1"], g_gdn_in, gcar["gdn_out"], g_kv,
              gcar["fox_in"], gcar["fox_out"]]
    by_owner = []
    for g, b, cs in zip(g_full, big, col_sharded):
        if cs:
            t = jnp.transpose(g.reshape(b.shape[0], N_DEV, b.shape[1]), (1, 0, 2))
        else:
            t = g
        by_owner.append(t.reshape(N_DEV, b.size))
    by_owner.append(jnp.zeros((N_DEV, total - sum(sizes)), F32))
    send = jnp.concatenate(by_owner, axis=1).reshape(N_DEV, pack_rows, PACK_COLS)
    recv = _all_to_all(send, name="a2a_grads")
    g_mine = _sum_slots(recv, name="sum_grads").reshape(-1)
    shard_grads = []
    off = 0
    for b, sz in zip(big, sizes):
        shard_grads.append(g_mine[off:off + sz].reshape(b.shape))
        off += sz
    grads["ffn_w_in"] = jnp.stack(shard_grads[0:2])
    grads["ffn_w_out"] = jnp.stack(shard_grads[2:4])
    grads["gdn_w_in"] = shard_grads[4][None]
    grads["gdn_w_out"] = shard_grads[5][None]
    grads["kv_w"] = shard_grads[6]
    grads["fox_w_in"] = shard_grads[7][None]
    grads["fox_w_out"] = shard_grads[8][None]

    delta, new_m, new_v = {}, {}, {}
    for n in names:
        if n in _SMALL_NAMES:
            continue
        delta[n], new_m[n], new_v[n] = _adamw(w_in[n], grads[n], m_in[n], v_in[n], name=f"adamw_{n}")
    sm_rows = -(-sum(small_sizes) // (LANES * 8)) * 8
    pk = lambda src: _pack_flat([src[n] for n in _SMALL_NAMES], sm_rows * LANES).reshape(sm_rows, LANES)
    d_pk, m_pk, vn_pk = _adamw(pk(w_in), pk(grads), pk(m_in), pk(v_in), name="adamw_small")
    off = 0
    for n, sz in zip(_SMALL_NAMES, small_sizes):
        shp = w_in[n].shape
        delta[n] = d_pk.reshape(-1)[off:off + sz].reshape(shp)
        new_m[n] = m_pk.reshape(-1)[off:off + sz].reshape(shp)
        new_v[n] = vn_pk.reshape(-1)[off:off + sz].reshape(shp)
        off += sz

    return (loss, gx[None], *[grads[n] for n in names], *[delta[n] for n in names],
            *[new_m[n] for n in names], *[new_v[n] for n in names])
```

```python
import functools

import jax
import jax.numpy as jnp
from jax import lax
from jax.experimental import pallas as pl
from jax.experimental.pallas import tpu as pltpu

F32 = jnp.float32
BF16 = jnp.bfloat16
MESH = pl.DeviceIdType.MESH
N_DEV = 8

D_MODEL = 2048
SEQ = 8192
GDN_QK_HEADS = 16
GDN_V_HEADS = 32
GDN_HEAD_DIM = 128
GDN_QK_DIM = GDN_QK_HEADS * GDN_HEAD_DIM
GDN_V_DIM = GDN_V_HEADS * GDN_HEAD_DIM
CONV_DIM = 2 * GDN_QK_DIM + GDN_V_DIM
GDN_MAIN = CONV_DIM + GDN_V_DIM
GDN_PROJ = GDN_MAIN + 2 * GDN_V_HEADS
CONV_K = 4
GDN_CHUNK = 64
FOX_HEADS = 16
FOX_KV_HEADS = 2
FOX_GROUP = FOX_HEADS // FOX_KV_HEADS
FOX_HEAD_DIM = 256
FOX_Q_DIM = FOX_HEADS * FOX_HEAD_DIM
FOX_KV_DIM = FOX_KV_HEADS * FOX_HEAD_DIM
KV_PROJ = 2 * FOX_KV_DIM + FOX_HEADS
FFN_HIDDEN = 5632
NORM_EPS = 1e-6

ADAM_LR = 0.001
ADAM_B1 = 0.9
ADAM_B2 = 0.999
ADAM_EPS = 1e-08
ADAM_WD = 0.01
ADAM_STEP = 10

LANES = 128
VMEM_LIMIT = 56 * 1024 * 1024
PACK_COLS = 1024
ATT_BLOCK = 512
NEG = -1e30


def _pick(n, cands):
    for c in cands:
        if n % c == 0:
            return c
    return n


def _cparams(sem):
    return pltpu.CompilerParams(dimension_semantics=sem, vmem_limit_bytes=VMEM_LIMIT)


def _matmul(a, b, *, trans_a=False, trans_b=False, name):
    if trans_a:
        kdim, m = a.shape
    else:
        m, kdim = a.shape
    if trans_b:
        n, kb = b.shape
    else:
        kb, n = b.shape
    assert kdim == kb, (a.shape, b.shape)
    tm = _pick(m, (1024, 512, 256, 128))
    tn = _pick(n, (1408, 1024, 512, 256, 128))
    tk = _pick(kdim, (1024, 1408, 512, 256, 128))
    nk = kdim // tk
    dims = (((0 if trans_a else 1,), (1 if trans_b else 0,)), ((), ()))

    def body(a_ref, b_ref, o_ref, acc_ref):
        k = pl.program_id(2)

        @pl.when(k == 0)
        def _():
            acc_ref[...] = jnp.zeros_like(acc_ref)

        acc_ref[...] += lax.dot_general(a_ref[...].astype(BF16), b_ref[...].astype(BF16), dims,
                                        preferred_element_type=F32)

        @pl.when(k == nk - 1)
        def _():
            o_ref[...] = acc_ref[...]

    a_spec = (pl.BlockSpec((tk, tm), lambda i, j, k: (k, i)) if trans_a
              else pl.BlockSpec((tm, tk), lambda i, j, k: (i, k)))
    b_spec = (pl.BlockSpec((tn, tk), lambda i, j, k: (j, k)) if trans_b
              else pl.BlockSpec((tk, tn), lambda i, j, k: (k, j)))
    return pl.pallas_call(
        body, name=name,
        grid=(m // tm, n // tn, nk),
        in_specs=[a_spec, b_spec],
        out_specs=pl.BlockSpec((tm, tn), lambda i, j, k: (i, j)),
        out_shape=jax.ShapeDtypeStruct((m, n), F32),
        scratch_shapes=[pltpu.VMEM((tm, tn), F32)],
        compiler_params=_cparams(("parallel", "parallel", "arbitrary")),
    )(a, b)


def _make_mm(tag):
    @jax.custom_vjp
    def mm(a, w, carrier):
        return _matmul(a, w, name=f"mm_{tag}")

    def fwd(a, w, carrier):
        return _matmul(a, w, name=f"mm_{tag}"), (a, w)

    def bwd(res, g):
        a, w = res
        da = _matmul(g, w, trans_b=True, name=f"mm_{tag}_da")
        dw = _matmul(a, g, trans_a=True, name=f"mm_{tag}_dw")
        return da, jnp.zeros_like(w), dw

    mm.defvjp(fwd, bwd)
    return mm


_NT = (((1,), (1,)), ((), ()))
_FOX_SCALE = FOX_HEAD_DIM ** -0.5


def _fox_fwd_call(q, kb, vb, f_col, f_row):
    length = q.shape[0]
    t = ATT_BLOCK
    tq = 2 * t
    nq = length // tq
    hd = FOX_HEAD_DIM

    def body(q_ref, k_ref, v_ref, fq_ref, fk_ref, o_ref, lse_ref, qs_sc, m_sc, l_sc, acc_sc):
        i = pl.program_id(1)
        qs_sc[...] = (q_ref[...] * _FOX_SCALE).astype(BF16)
        m_sc[...] = jnp.full_like(m_sc, NEG)
        l_sc[...] = jnp.zeros_like(l_sc)
        acc_sc[...] = jnp.zeros_like(acc_sc)

        def block(j, modes):
            off = pl.multiple_of(j * t, t)
            kj = k_ref[pl.ds(off, t), :]
            vj = v_ref[pl.ds(off, t), :]
            fk = fk_ref[0, :, pl.ds(off, t)]
            for a in (0, 1):
                if modes[a] is None:
                    continue
                rows = pl.ds(a * t, t)
                s = lax.dot_general(qs_sc[rows, :], kj, _NT, preferred_element_type=F32)
                s = s + fq_ref[0, rows, :] - fk
                if modes[a]:
                    r = lax.broadcasted_iota(jnp.int32, (t, t), 0)
                    c = lax.broadcasted_iota(jnp.int32, (t, t), 1)
                    s = jnp.where(r >= c, s, NEG)
                m_old = m_sc[a]
                m_new = jnp.maximum(m_old, jnp.max(s, axis=1, keepdims=True))
                alpha = jnp.exp(m_old - m_new)
                p = jnp.exp(s - m_new)
                l_sc[a] = alpha * l_sc[a] + jnp.sum(p, axis=1, keepdims=True)
                acc_sc[a] = alpha * acc_sc[a] + jnp.dot(p.astype(BF16), vj, preferred_element_type=F32)
                m_sc[a] = m_new

        def loop_body(j, carry):
            block(j, (False, False))
            return carry

        lax.fori_loop(0, 2 * i, loop_body, 0)
        block(2 * i, (True, False))
        block(2 * i + 1, (None, True))
        for a in (0, 1):
            rows = pl.ds(a * t, t)
            o_ref[rows, :] = acc_sc[a] / l_sc[a]
            lse_ref[0, rows, :] = m_sc[a] + jnp.log(l_sc[a])

    return pl.pallas_call(
        body, name="fox_fwd",
        grid=(FOX_HEADS, nq),
        in_specs=[
            pl.BlockSpec((tq, hd), lambda h, i: (i, h)),
            pl.BlockSpec((length, hd), lambda h, i: (0, h // FOX_GROUP)),
            pl.BlockSpec((length, hd), lambda h, i: (0, h // FOX_GROUP)),
            pl.BlockSpec((1, tq, 1), lambda h, i: (h, i, 0)),
            pl.BlockSpec((1, 1, length), lambda h, i: (h, 0, 0)),
        ],
        out_specs=[
            pl.BlockSpec((tq, hd), lambda h, i: (i, h)),
            pl.BlockSpec((1, tq, 1), lambda h, i: (h, i, 0)),
        ],
        out_shape=[jax.ShapeDtypeStruct((length, FOX_Q_DIM), F32),
                   jax.ShapeDtypeStruct((FOX_HEADS, length, 1), F32)],
        scratch_shapes=[pltpu.VMEM((tq, hd), BF16), pltpu.VMEM((2, t, 1), F32), pltpu.VMEM((2, t, 1), F32),
                        pltpu.VMEM((2, t, hd), F32)],
        compiler_params=_cparams(("parallel", "arbitrary")),
    )(q, kb, vb, f_col, f_row)


def _fox_dq_call(q, kb, vb, f_col, f_row, lse_col, do):
    length = q.shape[0]
    t = ATT_BLOCK
    tq = 2 * t
    nq = length // tq
    hd = FOX_HEAD_DIM

    def body(q_ref, k_ref, v_ref, fq_ref, fk_ref, lse_ref, do_ref, dq_ref, dl_ref, qs_sc, do_sc, acc_sc, dl_sc):
        i = pl.program_id(1)
        qs_sc[...] = (q_ref[...] * _FOX_SCALE).astype(BF16)
        do_sc[...] = do_ref[...].astype(BF16)
        acc_sc[...] = jnp.zeros_like(acc_sc)
        dl_sc[...] = jnp.zeros_like(dl_sc)

        def block(j, modes, second_pass):
            off = pl.multiple_of(j * t, t)
            kj = k_ref[pl.ds(off, t), :]
            vj = v_ref[pl.ds(off, t), :]
            fk = fk_ref[0, :, pl.ds(off, t)]
            for a in (0, 1):
                if modes[a] is None:
                    continue
                rows = pl.ds(a * t, t)
                s = lax.dot_general(qs_sc[rows, :], kj, _NT, preferred_element_type=F32)
                s = s + fq_ref[0, rows, :] - fk
                if modes[a]:
                    r = lax.broadcasted_iota(jnp.int32, (t, t), 0)
                    c = lax.broadcasted_iota(jnp.int32, (t, t), 1)
                    s = jnp.where(r >= c, s, NEG)
                p = jnp.exp(s - lse_ref[0, rows, :])
                dp = lax.dot_general(do_sc[rows, :], vj, _NT, preferred_element_type=F32)
                if second_pass:
                    ds = p * (dp - dl_sc[a])
                    acc_sc[a] += jnp.dot(ds.astype(BF16), kj, preferred_element_type=F32)
                else:
                    dl_sc[a] += jnp.sum(p * dp, axis=1, keepdims=True)

        def sweep(second_pass):
            def loop_body(j, carry):
                block(j, (False, False), second_pass)
                return carry

            lax.fori_loop(0, 2 * i, loop_body, 0)
            block(2 * i, (True, False), second_pass)
            block(2 * i + 1, (None, True), second_pass)

        sweep(False)
        sweep(True)
        for a in (0, 1):
            rows = pl.ds(a * t, t)
            dq_ref[rows, :] = acc_sc[a] * _FOX_SCALE
            dl_ref[0, rows, :] = dl_sc[a]

    col = pl.BlockSpec((1, tq, 1), lambda h, i: (h, i, 0))
    return pl.pallas_call(
        body, name="fox_dq",
        grid=(FOX_HEADS, nq),
        in_specs=[
            pl.BlockSpec((tq, hd), lambda h, i: (i, h)),
            pl.BlockSpec((length, hd), lambda h, i: (0, h // FOX_GROUP)),
            pl.BlockSpec((length, hd), lambda h, i: (0, h // FOX_GROUP)),
            col,
            pl.BlockSpec((1, 1, length), lambda h, i: (h, 0, 0)),
            col,
            pl.BlockSpec((tq, hd), lambda h, i: (i, h)),
        ],
        out_specs=[pl.BlockSpec((tq, hd), lambda h, i: (i, h)), col],
        out_shape=[jax.ShapeDtypeStruct((length, FOX_Q_DIM), F32),
                   jax.ShapeDtypeStruct((FOX_HEADS, length, 1), F32)],
        scratch_shapes=[pltpu.VMEM((tq, hd), BF16), pltpu.VMEM((tq, hd), BF16), pltpu.VMEM((2, t, hd), F32),
                        pltpu.VMEM((2, t, 1), F32)],
        compiler_params=_cparams(("parallel", "arbitrary")),
    )(q, kb, vb, f_col, f_row, lse_col, do)


def _fox_dkv_call(qs, dob, kb, vb, f_col, f_row, lse_row, dl_row):
    length = qs.shape[0]
    t = ATT_BLOCK
    tk = 2 * t
    nq = length // t
    hd = FOX_HEAD_DIM

    def body(k_ref, v_ref, q_ref, do_ref, fk_ref, fq_ref, lse_ref, dl_ref,
             dk_ref, dv_ref, df_ref, dk_sc, dv_sc, df_sc):
        j = pl.program_id(1)
        dk_sc[...] = jnp.zeros_like(dk_sc)
        dv_sc[...] = jnp.zeros_like(dv_sc)
        df_sc[...] = jnp.zeros_like(df_sc)

        def block(i, modes):
            off = pl.multiple_of(i * t, t)
            qi = q_ref[pl.ds(off, t), :]
            doi = do_ref[pl.ds(off, t), :]
            fq = fq_ref[0, :, pl.ds(off, t)]
            lse = lse_ref[0, :, pl.ds(off, t)]
            dl = dl_ref[0, :, pl.ds(off, t)]
            for b in (0, 1):
                if modes[b] is None:
                    continue
                keys = pl.ds(b * t, t)
                st = lax.dot_general(k_ref[keys, :], qi, _NT, preferred_element_type=F32)
                st = st + fq - fk_ref[0, keys, :]
                if modes[b]:
                    r = lax.broadcasted_iota(jnp.int32, (t, t), 0)
                    c = lax.broadcasted_iota(jnp.int32, (t, t), 1)
                    st = jnp.where(c >= r, st, NEG)
                pt = jnp.exp(st - lse)
                dv_sc[b] += jnp.dot(pt.astype(BF16), doi, preferred_element_type=F32)
                dpt = lax.dot_general(v_ref[keys, :], doi, _NT, preferred_element_type=F32)
                dst = pt * (dpt - dl)
                dk_sc[b] += jnp.dot(dst.astype(BF16), qi, preferred_element_type=F32)
                df_sc[b] += jnp.sum(dst, axis=1, keepdims=True)

        def loop_body(i, carry):
            block(i, (False, False))
            return carry

        block(2 * j, (True, None))
        block(2 * j + 1, (False, True))
        lax.fori_loop(2 * j + 2, nq, loop_body, 0)
        for b in (0, 1):
            keys = pl.ds(b * t, t)
            dk_ref[keys, :] = dk_sc[b]
            dv_ref[keys, :] = dv_sc[b]
            df_ref[0, keys, :] = -df_sc[b]

    row = pl.BlockSpec((1, 1, length), lambda h, j: (h, 0, 0))
    return pl.pallas_call(
        body, name="fox_dkv",
        grid=(FOX_HEADS, length // tk),
        in_specs=[
            pl.BlockSpec((tk, hd), lambda h, j: (j, h // FOX_GROUP)),
            pl.BlockSpec((tk, hd), lambda h, j: (j, h // FOX_GROUP)),
            pl.BlockSpec((length, hd), lambda h, j: (0, h)),
            pl.BlockSpec((length, hd), lambda h, j: (0, h)),
            pl.BlockSpec((1, tk, 1), lambda h, j: (h, j, 0)),
            row, row, row,
        ],
        out_specs=[
            pl.BlockSpec((tk, hd), lambda h, j: (j, h)),
            pl.BlockSpec((tk, hd), lambda h, j: (j, h)),
            pl.BlockSpec((1, tk, 1), lambda h, j: (h, j, 0)),
        ],
        out_shape=[jax.ShapeDtypeStruct((length, FOX_Q_DIM), F32),
                   jax.ShapeDtypeStruct((length, FOX_Q_DIM), F32),
                   jax.ShapeDtypeStruct((FOX_HEADS, length, 1), F32)],
        scratch_shapes=[pltpu.VMEM((2, t, hd), F32), pltpu.VMEM((2, t, hd), F32), pltpu.VMEM((2, t, 1), F32)],
        compiler_params=_cparams(("parallel", "arbitrary")),
    )(kb, vb, qs, dob, f_col, f_row, lse_row, dl_row)


@jax.custom_vjp
def _fox_attention(q, k, v, f):
    return _fox_attention_fwd(q, k, v, f)[0]


def _fox_attention_fwd(q, k, v, f):
    kb = k.astype(BF16)
    vb = v.astype(BF16)
    f_row = jnp.transpose(f)[:, None, :]
    f_col = jnp.transpose(f)[:, :, None]
    o, lse = _fox_fwd_call(q, kb, vb, f_col, f_row)
    return o, (q, kb, vb, f_col, f_row, lse)


def _fox_attention_bwd(res, do):
    q, kb, vb, f_col, f_row, lse = res
    length = q.shape[0]
    lse_row = jnp.transpose(lse, (0, 2, 1))
    dq, dl_col = _fox_dq_call(q, kb, vb, f_col, f_row, lse, do)
    dl_row = jnp.transpose(dl_col, (0, 2, 1))
    qs =(q * _FOX_SCALE).astype(BF16)
    dkh, dvh, dfk = _fox_dkv_call(qs, do.astype(BF16), kb, vb, f_col, f_row, lse_row, dl_row)
    dk = jnp.sum(dkh.reshape(length, FOX_KV_HEADS, FOX_GROUP, FOX_HEAD_DIM), axis=2).reshape(length, FOX_KV_DIM)
    dv = jnp.sum(dvh.reshape(length, FOX_KV_HEADS, FOX_GROUP, FOX_HEAD_DIM), axis=2).reshape(length, FOX_KV_DIM)
    df = jnp.transpose(dfk[:, :, 0])
    return dq, dk, dv, df


_fox_attention.defvjp(_fox_attention_fwd, _fox_attention_bwd)


def _peer(mx, my, mc, k):
    return (1 - mx if k & 4 else mx, 1 - my if k & 2 else my, 1 - mc if k & 1 else mc)


def _all_gather(x, *, name, in_vmem):
    space = pltpu.VMEM if in_vmem else pl.ANY

    def body(x_ref, out_ref, send_sems, recv_sems, local_sem):
        mx, my, mc = lax.axis_index("x"), lax.axis_index("y"), lax.axis_index("c")
        me = 4 * mx + 2 * my + mc
        mine = pltpu.make_async_copy(x_ref, out_ref.at[me], local_sem)
        mine.start()
        copies = []
        for k in range(1, N_DEV):
            cp = pltpu.make_async_remote_copy(
                src_ref=x_ref, dst_ref=out_ref.at[me],
                send_sem=send_sems.at[k - 1], recv_sem=recv_sems.at[k - 1],
                device_id=_peer(mx, my, mc, k), device_id_type=MESH)
            cp.start()
            copies.append(cp)
        for cp in copies:
            cp.wait()
        mine.wait()

    return pl.pallas_call(
        body, name=name,
        out_shape=jax.ShapeDtypeStruct((N_DEV,) + x.shape, x.dtype),
        in_specs=[pl.BlockSpec(memory_space=space)],
        out_specs=pl.BlockSpec(memory_space=space),
        scratch_shapes=[pltpu.SemaphoreType.DMA((N_DEV - 1,)), pltpu.SemaphoreType.DMA((N_DEV - 1,)),
                        pltpu.SemaphoreType.DMA],
    )(x)


def _all_gather_by_chip(x, *, name):
    def body(x_ref, out_ref, send_sems, recv_sems, local_sem):
        mx, my, mc = lax.axis_index("x"), lax.axis_index("y"), lax.axis_index("c")
        me, sibling = (mx, my, mc), (mx, my, 1 - mc)
        chips = [(1 - mx, my), (mx, 1 - my), (1 - mx, 1 - my)]

        def slot(px, py, pc):
            return out_ref.at[4 * px + 2 * py + pc]

        def copy(k, block, to, src=None):
            return pltpu.make_async_remote_copy(
                src_ref=slot(*block) if src is None else src, dst_ref=slot(*block),
                send_sem=send_sems.at[k], recv_sem=recv_sems.at[k],
                device_id=to, device_id_type=MESH)

        mine = pltpu.make_async_copy(x_ref, slot(*me), local_sem)
        mine.start()
        first = [copy(0, me, sibling, src=x_ref)]
        first += [copy(1 + j, me, (*chip, mc), src=x_ref) for j, chip in enumerate(chips)]
        for cp in first:
            cp.start()
        passed = [copy(4 + j, (*chip, mc), sibling) for j, chip in enumerate(chips)]
        for j, chip in enumerate(chips):
            copy(1 + j, (*chip, mc), me).wait_recv()
            passed[j].start()
        copy(0, sibling, me).wait_recv()
        for j, chip in enumerate(chips):
            copy(4 + j, (*chip, 1 - mc), me).wait_recv()
        for cp in first + passed:
            cp.wait_send()
        mine.wait()

    return pl.pallas_call(
        body, name=name,
        out_shape=jax.ShapeDtypeStruct((N_DEV,) + x.shape, x.dtype),
        in_specs=[pl.BlockSpec(memory_space=pl.ANY)],
        out_specs=pl.BlockSpec(memory_space=pl.ANY),
        scratch_shapes=[pltpu.SemaphoreType.DMA((N_DEV - 1,)), pltpu.SemaphoreType.DMA((N_DEV - 1,)),
                        pltpu.SemaphoreType.DMA],
    )(x)


def _exchange_sibling(x, *, name):
    n = x.shape[0]

    def body(x_ref, out_ref, send_sems, recv_sems):
        sibling = (lax.axis_index("x"), lax.axis_index("y"), 1 - lax.axis_index("c"))
        copies = [pltpu.make_async_remote_copy(
            src_ref=x_ref.at[q], dst_ref=out_ref.at[q], send_sem=send_sems.at[q], recv_sem=recv_sems.at[q],
            device_id=sibling, device_id_type=MESH) for q in range(n)]
        for cp in copies:
            cp.start()
        for cp in copies:
            cp.wait()

    return pl.pallas_call(
        body, name=name,
        out_shape=jax.ShapeDtypeStruct(x.shape, x.dtype),
        in_specs=[pl.BlockSpec(memory_space=pl.ANY)],
        out_specs=pl.BlockSpec(memory_space=pl.ANY),
        scratch_shapes=[pltpu.SemaphoreType.DMA((n,)), pltpu.SemaphoreType.DMA((n,))],
    )(x)


def _exchange_chips(x, *, name):
    n_chip = x.shape[0]

    def body(x_ref, out_ref, send_sems, recv_sems, local_sem):
        mx, my, mc = lax.axis_index("x"), lax.axis_index("y"), lax.axis_index("c")
        my_chip = 2 * mx + my
        mine = pltpu.make_async_copy(x_ref.at[my_chip], out_ref.at[my_chip], local_sem)
        mine.start()
        copies = []
        for j, (px, py) in enumerate([(1 - mx, my), (mx, 1 - my), (1 - mx, 1 - my)]):
            cp = pltpu.make_async_remote_copy(
                src_ref=x_ref.at[2 * px + py], dst_ref=out_ref.at[my_chip],
                send_sem=send_sems.at[j], recv_sem=recv_sems.at[j],
                device_id=(px, py, mc), device_id_type=MESH)
            cp.start()
            copies.append(cp)
        for cp in copies:
            cp.wait()
        mine.wait()

    return pl.pallas_call(
        body, name=name,
        out_shape=jax.ShapeDtypeStruct(x.shape, x.dtype),
        in_specs=[pl.BlockSpec(memory_space=pl.ANY)],
        out_specs=pl.BlockSpec(memory_space=pl.ANY),
        scratch_shapes=[pltpu.SemaphoreType.DMA((n_chip - 1,)), pltpu.SemaphoreType.DMA((n_chip - 1,)),
                        pltpu.SemaphoreType.DMA],
    )(x)


def _add_pair(a, b, *, name):
    n, r, c = a.shape
    tr = _pick(r, (512, 256, 128, 64, 32, 16))

    def body(a_ref, b_ref, o_ref):
        o_ref[...] = (a_ref[...].astype(F32) + b_ref[...].astype(F32)).astype(o_ref.dtype)

    spec = pl.BlockSpec((1, tr, c), lambda q, i: (q, i, 0))
    return pl.pallas_call(
        body, name=name,
        grid=(n, r // tr),
        in_specs=[spec, spec],
        out_specs=spec,
        out_shape=jax.ShapeDtypeStruct(a.shape, a.dtype),
        compiler_params=_cparams(("parallel", "parallel")),
    )(a, b)


def _sum_slots(x, *, name):
    n, r, c = x.shape
    tr = _pick(r, (256, 128, 64, 32, 16, 8))

    def body(x_ref, o_ref):
        acc = x_ref[0].astype(F32)
        for s in range(1, n):
            acc = acc + x_ref[s].astype(F32)
        o_ref[...] = acc

    return pl.pallas_call(
        body, name=name,
        grid=(r // tr,),
        in_specs=[pl.BlockSpec((n, tr, c), lambda i: (0, i, 0))],
        out_specs=pl.BlockSpec((tr, c), lambda i: (i, 0)),
        out_shape=jax.ShapeDtypeStruct((r, c), F32),
        compiler_params=_cparams(("parallel",)),
    )(x)


def _adamw(w, g, m, v, *, name):
    shape = w.shape
    c = shape[-1]
    r = w.size // c
    w2, g2, m2, v2 = (t.reshape(r, c) for t in (w, g, m, v))
    tr = _pick(r, (128, 64, 32, 16, 8))

    def body(w_ref, g_ref, m_ref, v_ref, d_ref, mo_ref, vo_ref):
        gv = g_ref[...]
        mn = ADAM_B1 * m_ref[...] + (1.0 - ADAM_B1) * gv
        vn = ADAM_B2 * v_ref[...] + (1.0 - ADAM_B2) * (gv * gv)
        m_hat = mn / (1.0 - ADAM_B1 ** ADAM_STEP)
        v_hat = vn / (1.0 - ADAM_B2 ** ADAM_STEP)
        d_ref[...] = -ADAM_LR * (m_hat / (jnp.sqrt(v_hat) + ADAM_EPS) + ADAM_WD * w_ref[...])
        mo_ref[...] = mn
        vo_ref[...] = vn

    spec = pl.BlockSpec((tr, c), lambda i: (i, 0))
    outs = pl.pallas_call(
        body, name=name,
        grid=(r // tr,),
        in_specs=[spec] * 4,
        out_specs=[spec] * 3,
        out_shape=[jax.ShapeDtypeStruct((r, c), F32)] * 3,
        compiler_params=_cparams(("parallel",)),
    )(w2, g2, m2, v2)
    return tuple(o.reshape(shape) for o in outs)


GDN_HEADS_PER_STEP = 8
_TN = (((0,), (0,)), ((), ()))


def _gdn_scan_specs(qd, u, attn, reverse):
    h, n, c, dk = qd.shape
    dv = u.shape[-1]
    g = GDN_HEADS_PER_STEP
    idx = (lambda hb, i: (hb, n - 1 - i, 0, 0)) if reverse else (lambda hb, i: (hb, i, 0, 0))
    return dict(
        qk=pl.BlockSpec((g, 1, c, dk), idx), uv=pl.BlockSpec((g, 1, c, dv), idx),
        attn=pl.BlockSpec((g, 1, c, c), idx), lane=pl.BlockSpec((g, 1, 1, dv), idx),
        state=pl.BlockSpec((g, 1, dk, dv), idx), grid=(h // g, n))


def _gdn_scan_fwd_call(qd, kd, u, w, attn, egl):
    h, n, c, dk = qd.shape
    dv = u.shape[-1]
    sp = _gdn_scan_specs(qd, u, attn, reverse=False)

    def body(q_ref, k_ref, u_ref, w_ref, a_ref, e_ref, o_ref, st_ref, s_sc):
        @pl.when(pl.program_id(1) == 0)
        def _():
            s_sc[...] = jnp.zeros_like(s_sc)

        for g in range(GDN_HEADS_PER_STEP):
            s = s_sc[g]
            st_ref[g, 0] = s
            sb = s.astype(BF16)
            v_new = u_ref[g, 0] - jnp.dot(w_ref[g, 0].astype(BF16), sb, preferred_element_type=F32)
            vb = v_new.astype(BF16)
            o_ref[g, 0] = (jnp.dot(q_ref[g, 0].astype(BF16), sb, preferred_element_type=F32)
                           + jnp.dot(a_ref[g, 0].astype(BF16), vb, preferred_element_type=F32))
            s_sc[g] = s * e_ref[g, 0] + lax.dot_general(k_ref[g, 0].astype(BF16), vb, _TN,
                                                        preferred_element_type=F32)

    return pl.pallas_call(
        body, name="gdn_scan_fwd",
        grid=sp["grid"],
        in_specs=[sp["qk"], sp["qk"], sp["uv"], sp["qk"], sp["attn"], sp["lane"]],
        out_specs=[sp["uv"], sp["state"]],
        out_shape=[jax.ShapeDtypeStruct((h, n, c, dv), F32), jax.ShapeDtypeStruct((h, n, dk, dv), F32)],
        scratch_shapes=[pltpu.VMEM((GDN_HEADS_PER_STEP, dk, dv), F32)],
        compiler_params=_cparams(("parallel", "arbitrary")),
    )(qd, kd, u, w, attn, egl)


def _gdn_scan_bwd_call(qd, kd, u, w, attn, egl, states, do):
    h, n, c, dk = qd.shape
    dv = u.shape[-1]
    sp = _gdn_scan_specs(qd, u, attn, reverse=True)

    def body(q_ref, k_ref, u_ref, w_ref, a_ref, e_ref, st_ref, do_ref,
             dq_ref, dk_ref, du_ref, dw_ref, da_ref, de_ref, ds_sc):
        @pl.when(pl.program_id(1) == 0)
        def _():
            ds_sc[...] = jnp.zeros_like(ds_sc)

        for g in range(GDN_HEADS_PER_STEP):
            s = st_ref[g, 0]
            sb = s.astype(BF16)
            ds = ds_sc[g]
            dsb = ds.astype(BF16)
            qb, kb, wb, ab = (r[g, 0].astype(BF16) for r in (q_ref, k_ref, w_ref, a_ref))
            dob = do_ref[g, 0].astype(BF16)
            vb = (u_ref[g, 0] - jnp.dot(wb, sb, preferred_element_type=F32)).astype(BF16)
            dv_new = (lax.dot_general(ab, dob, _TN, preferred_element_type=F32)
                      + jnp.dot(kb, dsb, preferred_element_type=F32))
            dvb = dv_new.astype(BF16)
            dq_ref[g, 0] = lax.dot_general(dob, sb, _NT, preferred_element_type=F32)
            da_ref[g, 0] = lax.dot_general(dob, vb, _NT, preferred_element_type=F32)
            dk_ref[g, 0] = lax.dot_general(vb, dsb, _NT, preferred_element_type=F32)
            du_ref[g, 0] = dv_new
            dw_ref[g, 0] = -lax.dot_general(dvb, sb, _NT, preferred_element_type=F32)
            de_ref[g, 0] = jnp.sum(ds * s, axis=0, keepdims=True)
            ds_sc[g] = (ds * e_ref[g, 0] + lax.dot_general(qb, dob, _TN, preferred_element_type=F32)
                        - lax.dot_general(wb, dvb, _TN, preferred_element_type=F32))

    return pl.pallas_call(
        body, name="gdn_scan_bwd",
        grid=sp["grid"],
        in_specs=[sp["qk"], sp["qk"], sp["uv"], sp["qk"], sp["attn"], sp["lane"], sp["state"], sp["uv"]],
        out_specs=[sp["qk"], sp["qk"], sp["uv"], sp["qk"], sp["attn"], sp["lane"]],
        out_shape=[jax.ShapeDtypeStruct((h, n, c, dk), F32), jax.ShapeDtypeStruct((h, n, c, dk), F32),
                   jax.ShapeDtypeStruct((h, n, c, dv), F32), jax.ShapeDtypeStruct((h, n, c, dk), F32),
                   jax.ShapeDtypeStruct((h, n, c, c), F32), jax.ShapeDtypeStruct((h, n, 1, dv), F32)],
        scratch_shapes=[pltpu.VMEM((GDN_HEADS_PER_STEP, dk, dv), F32)],
        compiler_params=_cparams(("parallel", "arbitrary")),
    )(qd, kd, u, w, attn, egl, states, do)


GDN_CHUNKS_PER_STEP = 8
GDN_DIAG = 16
_HI = lax.Precision.HIGHEST


def _tri_solve_fwd_call(a, rhs):
    h, n, c, _ = a.shape
    r = rhs.shape[-1]
    g_n = GDN_CHUNKS_PER_STEP
    dot = functools.partial(jnp.dot, precision=_HI, preferred_element_type=F32)

    def body(a_ref, rhs_ref, sol_ref, t_ref, x_sc):
        ri = lax.broadcasted_iota(jnp.int32, (c, c), 0)
        ci = lax.broadcasted_iota(jnp.int32, (c, c), 1)
        eye = (ri == ci).astype(F32)
        diag_blocks = (ri // GDN_DIAG) == (ci // GDN_DIAG)
        for g in range(g_n):
            av = a_ref[0, g]
            d1 = jnp.where(diag_blocks, av, 0.0)
            d2 = dot(d1, d1)
            d4 = dot(d2, d2)
            d8 = dot(d4, d4)
            t_diag = dot(dot(dot(eye - d1, eye + d2), eye + d4), eye + d8)
            x_sc[g] = jnp.zeros((c, c), F32)
            for b in range(c // GDN_DIAG):
                rows = pl.ds(b * GDN_DIAG, GDN_DIAG)
                resid = eye - dot(av, x_sc[g])
                x_sc[g, rows, :] = dot(t_diag[b * GDN_DIAG:(b + 1) * GDN_DIAG, :], resid)
            t_inv = x_sc[g]
            t_ref[0, g] = t_inv
            sol_ref[0, g] = dot(t_inv, rhs_ref[0, g])

    a_spec = pl.BlockSpec((1, g_n, c, c), lambda i, j: (i, j, 0, 0))
    r_spec = pl.BlockSpec((1, g_n, c, r), lambda i, j: (i, j, 0, 0))
    return pl.pallas_call(
        body, name="gdn_tri_solve",
        grid=(h, n // g_n),
        in_specs=[a_spec, r_spec],
        out_specs=[r_spec, a_spec],
        out_shape=[jax.ShapeDtypeStruct(rhs.shape, F32), jax.ShapeDtypeStruct(a.shape, F32)],
        scratch_shapes=[pltpu.VMEM((g_n, c, c), F32)],
        compiler_params=_cparams(("parallel", "parallel")),
    )(a, rhs)


def _tri_solve_bwd_call(t_inv, sol, dsol):
    h, n, c, _ = t_inv.shape
    r = sol.shape[-1]
    g_n = GDN_CHUNKS_PER_STEP

    def body(t_ref, sol_ref, dsol_ref, drhs_ref, da_ref):
        for g in range(g_n):
            d_rhs = lax.dot_general(t_ref[0, g], dsol_ref[0, g], _TN, precision=_HI, preferred_element_type=F32)
            drhs_ref[0, g] = d_rhs
            da_ref[0, g] = -lax.dot_general(d_rhs, sol_ref[0, g], _NT, precision=_HI, preferred_element_type=F32)

    a_spec = pl.BlockSpec((1, g_n, c, c), lambda i, j: (i, j, 0, 0))
    r_spec = pl.BlockSpec((1, g_n, c, r), lambda i, j: (i, j, 0, 0))
    return pl.pallas_call(
        body, name="gdn_tri_solve_bwd",
        grid=(h, n // g_n),
        in_specs=[a_spec, r_spec, r_spec],
        out_specs=[r_spec, a_spec],
        out_shape=[jax.ShapeDtypeStruct(sol.shape, F32), jax.ShapeDtypeStruct(t_inv.shape, F32)],
        compiler_params=_cparams(("parallel", "parallel")),
    )(t_inv, sol, dsol)


@jax.custom_vjp
def _tri_solve(a, rhs):
    return _tri_solve_fwd_call(a, rhs)[0]


def _tri_solve_fwd(a, rhs):
    sol, t_inv = _tri_solve_fwd_call(a, rhs)
    return sol, (t_inv, sol)


def _tri_solve_bwd(res, dsol):
    t_inv, sol = res
    d_rhs, d_a = _tri_solve_bwd_call(t_inv, sol, dsol)
    return d_a, d_rhs


_tri_solve.defvjp(_tri_solve_fwd, _tri_solve_bwd)


@jax.custom_vjp
def _gdn_scan(qd, kd, u, w, attn, g_last):
    return _gdn_scan_fwd(qd, kd, u, w, attn, g_last)[0]


def _gdn_scan_fwd(qd, kd, u, w, attn, g_last):
    egl = jnp.broadcast_to(jnp.exp(g_last)[:, :, None, None], g_last.shape + (1, u.shape[-1]))
    o, states = _gdn_scan_fwd_call(qd, kd, u, w, attn, egl)
    return o, (qd, kd, u, w, attn, egl, states)


def _gdn_scan_bwd(res, do):
    qd, kd, u, w, attn, egl, states = res
    dq, dk, du, dw, da, de = _gdn_scan_bwd_call(qd, kd, u, w, attn, egl, states, do)
    dgl = jnp.sum(de[:, :, 0, :], axis=-1) * egl[:, :, 0, 0]
    return dq, dk, du, dw, da, dgl


_gdn_scan.defvjp(_gdn_scan_fwd, _gdn_scan_bwd)


def _rms_norm(x, w):
    y = x * lax.rsqrt(jnp.mean(x * x, axis=-1, keepdims=True) + NORM_EPS)
    return y * w


def _modulate(h, shift, scale):
    return h * (1 + scale) + shift


def _l2_normalize(x):
    return x * lax.rsqrt(jnp.sum(x * x, axis=-1, keepdims=True) + NORM_EPS)


def _causal_conv(x, w):
    width = w.shape[0]
    length = x.shape[0]
    xp = jnp.pad(x, ((width - 1, 0), (0, 0)))
    return sum(xp[i:i + length] * w[i] for i in range(width))


def _gated_delta_rule_chunked(q, k, v, g, beta):
    h, length, dk = q.shape
    dv = v.shape[-1]
    n = length // GDN_CHUNK
    blk = lambda t: t.reshape(h, n, GDN_CHUNK, *t.shape[2:])
    q = blk(q) * dk ** -0.5
    k = blk(k)
    v = blk(v)
    beta = blk(beta)
    g = jnp.cumsum(blk(g), axis=-1)
    causal = jnp.tril(jnp.ones((GDN_CHUNK, GDN_CHUNK), dtype=bool))
    strict = jnp.tril(jnp.ones((GDN_CHUNK, GDN_CHUNK), dtype=bool), k=-1)
    decay = jnp.exp(jnp.where(causal, g[..., :, None] - g[..., None, :], -jnp.inf))
    k_beta = k * beta[..., None]
    a_strict = jnp.where(strict, jnp.einsum('hnid,hnjd->hnij', k_beta, k) * decay, 0.0)
    rhs = jnp.concatenate([v * beta[..., None], k_beta * jnp.exp(g)[..., None]], axis=-1)
    sol = _tri_solve(a_strict, rhs)
    u, w = sol[..., :dv], sol[..., dv:]
    attn = jnp.where(causal, jnp.einsum('hnid,hnjd->hnij', q, k) * decay, 0.0)
    g_last = g[..., -1]
    q_dec = q * jnp.exp(g)[..., None]
    k_dec = k * jnp.exp(g_last[..., None] - g)[..., None]

    return _gdn_scan(q_dec, k_dec, u, w, attn, g_last).reshape(h, length, dv)


_MM = {tag: _make_mm(tag) for tag in (
    "gdn_in", "gdn_ba", "gdn_out", "ffn_in0", "ffn_out0", "kv", "kv_f", "fox_in", "fox_out", "ffn_in1", "ffn_out1")}


def _swiglu(h, w_in, w_out, c_in, c_out, layer):
    gu = _MM[f"ffn_in{layer}"](h, w_in, c_in)
    gate, up = gu[:, :FFN_HIDDEN], gu[:, FFN_HIDDEN:]
    return _MM[f"ffn_out{layer}"](jax.nn.silu(gate) * up, w_out, c_out)


def _gated_deltanet(h, wts, car, conv_w, a_log, dt_bias, norm_w):
    length = h.shape[0]
    main = _MM["gdn_in"](h, wts["gdn_in"], car["gdn_in"])
    ba = _MM["gdn_ba"](h, wts["gdn_ba"], car["gdn_ba"])
    qkv, z = main[:, :CONV_DIM], main[:, CONV_DIM:]
    beta_logit, a = ba[:, :GDN_V_HEADS], ba[:, GDN_V_HEADS:2 * GDN_V_HEADS]
    qkv = jax.nn.silu(_causal_conv(qkv, conv_w))
    q, k, v = qkv[:, :GDN_QK_DIM], qkv[:, GDN_QK_DIM:2 * GDN_QK_DIM], qkv[:, 2 * GDN_QK_DIM:]
    rep = GDN_V_HEADS // GDN_QK_HEADS
    heads = lambda t, nh: t.reshape(length, nh, GDN_HEAD_DIM)
    q = jnp.repeat(_l2_normalize(heads(q, GDN_QK_HEADS)), rep, axis=1)
    k = jnp.repeat(_l2_normalize(heads(k, GDN_QK_HEADS)), rep, axis=1)
    v = heads(v, GDN_V_HEADS)
    beta = jax.nn.sigmoid(beta_logit)
    g = -jnp.exp(a_log) * jax.nn.softplus(a + dt_bias)
    tr = lambda t: jnp.swapaxes(t, 0, 1)
    o = _gated_delta_rule_chunked(tr(q), tr(k), tr(v), tr(g), tr(beta))
    o = _rms_norm(tr(o), norm_w) * jax.nn.silu(heads(z, GDN_V_HEADS))
    return _MM["gdn_out"](o.reshape(length, GDN_V_DIM), wts["gdn_out"], car["gdn_out"])


def _local_loss(x, mods, small, car, wts, target):
    split6 = lambda m: tuple(m[i * D_MODEL:(i + 1) * D_MODEL][None, :] for i in range(6))
    sh_m, sc_m, g_m, sh_f, sc_f, g_f = split6(mods["ada0"])
    h = _modulate(_rms_norm(x, small["norm_mix"][0]), sh_m, sc_m)
    y = _gated_deltanet(h, wts, car, small["gdn_conv"], small["gdn_a_log"][0], small["gdn_dt_bias"][0],
                        small["gdn_norm"][0])
    x = x + g_m * y
    h = _modulate(_rms_norm(x, small["norm_ffn"][0]), sh_f, sc_f)
    x = x + g_f * _swiglu(h, wts["ffn_in0"], wts["ffn_out0"], car["ffn_in0"], car["ffn_out0"], 0)
    length = x.shape[0]
    kv_shift, kv_scale = mods["kvada"][None, :D_MODEL], mods["kvada"][None, D_MODEL:]
    hk = _modulate(_rms_norm(x, small["kv_norm"]), kv_shift, kv_scale)
    kv = _MM["kv"](hk, wts["kv"], car["kv"])
    f_logit = _MM["kv_f"](hk, wts["kv_f"], car["kv_f"])[:, :FOX_HEADS]
    k_sh = _rms_norm(kv[:, :FOX_KV_DIM].reshape(length, FOX_KV_HEADS, FOX_HEAD_DIM), small["k_norm"])
    k_sh = k_sh.reshape(length, FOX_KV_DIM)
    v_sh = kv[:, FOX_KV_DIM:]
    f_cum = jnp.cumsum(jax.nn.log_sigmoid(f_logit + small["forget_b"]), axis=0)
    sh_m, sc_m, g_m, sh_f, sc_f, g_f = split6(mods["ada1"])
    h = _modulate(_rms_norm(x, small["norm_mix"][1]), sh_m, sc_m)
    qg = _MM["fox_in"](h, wts["fox_in"], car["fox_in"])
    q = _rms_norm(qg[:, :FOX_Q_DIM].reshape(length, FOX_HEADS, FOX_HEAD_DIM), small["q_norm"][0])
    o = _fox_attention(q.reshape(length, FOX_Q_DIM), k_sh, v_sh, f_cum)
    y = _MM["fox_out"](o * jax.nn.sigmoid(qg[:, FOX_Q_DIM:]), wts["fox_out"], car["fox_out"])
    x = x + g_m * y
    h = _modulate(_rms_norm(x, small["norm_ffn"][1]), sh_f, sc_f)
    x = x + g_f * _swiglu(h, wts["ffn_in1"], wts["ffn_out1"], car["ffn_in1"], car["ffn_out1"], 1)
    out = _modulate(_rms_norm(x, small["out_norm"]), mods["outada"][None, :D_MODEL], mods["outada"][None, D_MODEL:])
    err = jnp.square(out - target)
    return 0.5 * jnp.sum(jnp.mean(err, axis=-1))


def _pad_rows(a, rows):
    return jnp.pad(a, ((0, rows - a.shape[0]), (0, 0)))


def _pack_flat(parts, total):
    flat = jnp.concatenate([p.reshape(-1) for p in parts])
    return jnp.pad(flat, (0, total - flat.shape[0]))


_SMALL_NAMES = ("ada_b", "norm_mix", "norm_ffn", "gdn_a_log", "gdn_dt_bias", "gdn_norm", "kv_ada_b", "kv_norm",
                "k_norm", "forget_b", "q_norm", "out_ada_b", "out_norm")


def kernel(x, c, ada_w, ada_b, norm_mix, norm_ffn, ffn_w_in, ffn_w_out, gdn_w_in, gdn_conv, gdn_a_log, gdn_dt_bias, gdn_norm, gdn_w_out, kv_ada_w, kv_ada_b, kv_norm, kv_w, k_norm, forget_b, fox_w_in, q_norm, fox_w_out, out_ada_w, out_ada_b, out_norm, loss_target, m_ada_w, m_ada_b, m_norm_mix, m_norm_ffn, m_ffn_w_in, m_ffn_w_out, m_gdn_w_in, m_gdn_conv, m_gdn_a_log, m_gdn_dt_bias, m_gdn_norm, m_gdn_w_out, m_kv_ada_w, m_kv_ada_b, m_kv_norm, m_kv_w, m_k_norm, m_forget_b, m_fox_w_in, m_q_norm, m_fox_w_out, m_out_ada_w, m_out_ada_b, m_out_norm, v_ada_w, v_ada_b, v_norm_mix, v_norm_ffn, v_ffn_w_in, v_ffn_w_out, v_gdn_w_in, v_gdn_conv, v_gdn_a_log, v_gdn_dt_bias, v_gdn_norm, v_gdn_w_out, v_kv_ada_w, v_kv_ada_b, v_kv_norm, v_kv_w, v_k_norm, v_forget_b, v_fox_w_in, v_q_norm, v_fox_w_out, v_out_ada_w, v_out_ada_b, v_out_norm):
    w_in = dict(ada_w=ada_w, ada_b=ada_b, norm_mix=norm_mix, norm_ffn=norm_ffn, ffn_w_in=ffn_w_in, ffn_w_out=ffn_w_out, gdn_w_in=gdn_w_in, gdn_conv=gdn_conv, gdn_a_log=gdn_a_log, gdn_dt_bias=gdn_dt_bias, gdn_norm=gdn_norm, gdn_w_out=gdn_w_out, kv_ada_w=kv_ada_w, kv_ada_b=kv_ada_b, kv_norm=kv_norm, kv_w=kv_w, k_norm=k_norm, forget_b=forget_b, fox_w_in=fox_w_in, q_norm=q_norm, fox_w_out=fox_w_out, out_ada_w=out_ada_w, out_ada_b=out_ada_b, out_norm=out_norm)
    m_in = dict(ada_w=m_ada_w, ada_b=m_ada_b, norm_mix=m_norm_mix, norm_ffn=m_norm_ffn, ffn_w_in=m_ffn_w_in, ffn_w_out=m_ffn_w_out, gdn_w_in=m_gdn_w_in, gdn_conv=m_gdn_conv, gdn_a_log=m_gdn_a_log, gdn_dt_bias=m_gdn_dt_bias, gdn_norm=m_gdn_norm, gdn_w_out=m_gdn_w_out, kv_ada_w=m_kv_ada_w, kv_ada_b=m_kv_ada_b, kv_norm=m_kv_norm, kv_w=m_kv_w, k_norm=m_k_norm, forget_b=m_forget_b, fox_w_in=m_fox_w_in, q_norm=m_q_norm, fox_w_out=m_fox_w_out, out_ada_w=m_out_ada_w, out_ada_b=m_out_ada_b, out_norm=m_out_norm)
    v_in = dict(ada_w=v_ada_w, ada_b=v_ada_b, norm_mix=v_norm_mix, norm_ffn=v_norm_ffn, ffn_w_in=v_ffn_w_in, ffn_w_out=v_ffn_w_out, gdn_w_in=v_gdn_w_in, gdn_conv=v_gdn_conv, gdn_a_log=v_gdn_a_log, gdn_dt_bias=v_gdn_dt_bias, gdn_norm=v_gdn_norm, gdn_w_out=v_gdn_w_out, kv_ada_w=v_kv_ada_w, kv_ada_b=v_kv_ada_b, kv_norm=v_kv_norm, kv_w=v_kv_w, k_norm=v_k_norm, forget_b=v_forget_b, fox_w_in=v_fox_w_in, q_norm=v_q_norm, fox_w_out=v_fox_w_out, out_ada_w=v_out_ada_w, out_ada_b=v_out_ada_b, out_norm=v_out_norm)
    names = list(w_in)
    me = 4 * lax.axis_index("x") + 2 * lax.axis_index("y") + lax.axis_index("c")
    d = D_MODEL

    n_conv = CONV_K * (CONV_DIM // N_DEV)
    first = _pack_flat([c, gdn_conv], d + n_conv).reshape(-1, LANES)
    first_all = _all_gather(first, name="ag_cond_conv", in_vmem=True).reshape(N_DEV, d + n_conv)
    cond_all = jax.nn.silu(first_all[:, :d])
    conv_full = first_all[:, d:].reshape(N_DEV, CONV_K, CONV_DIM // N_DEV)
    conv_full = jnp.transpose(conv_full, (1, 0, 2)).reshape(CONV_K, CONV_DIM)

    ada_cat = jnp.concatenate([ada_w[0], ada_w[1], kv_ada_w, out_ada_w], axis=1)
    n_ada = ada_cat.shape[1]
    cond_pad = _pad_rows(cond_all, LANES)
    mods_part = _matmul(cond_pad, ada_cat, name="mm_ada")[:N_DEV]
    mods_all = _all_gather(mods_part.reshape(-1, LANES), name="ag_mods", in_vmem=True)
    mods_all = mods_all.reshape(N_DEV, N_DEV, n_ada)
    mine = lax.dynamic_index_in_dim(mods_all, me, axis=1, keepdims=False)
    s6 = 6 * d // N_DEV
    s2 = 2 * d // N_DEV
    mods = {
        "ada0": mine[:, :s6].reshape(-1) + ada_b[0],
        "ada1": mine[:, s6:2 * s6].reshape(-1) + ada_b[1],
        "kvada": mine[:, 2 * s6:2 * s6 + s2].reshape(-1) + kv_ada_b,
        "outada": mine[:, 2 * s6 + s2:].reshape(-1) + out_ada_b,
    }

    big = [ffn_w_in[0], ffn_w_in[1], ffn_w_out[0], ffn_w_out[1], gdn_w_in[0], gdn_w_out[0], kv_w, fox_w_in[0],
           fox_w_out[0]]
    col_sharded = [True, True, False, False, True, False, False, True, False]
    sizes = [b.size for b in big]
    pack_rows = -(-sum(sizes) // (PACK_COLS * 512)) * 512
    total = pack_rows * PACK_COLS
    packed = _pack_flat([b.astype(BF16) for b in big], total).reshape(pack_rows, PACK_COLS)
    gathered = _all_gather_by_chip(packed, name="ag_weights").reshape(N_DEV, total)
    full = []
    off = 0
    for b, cs, sz in zip(big, col_sharded, sizes):
        blk = gathered[:, off:off + sz].reshape((N_DEV,) + b.shape)
        off += sz
        if cs:
            full.append(jnp.transpose(blk, (1, 0, 2)).reshape(b.shape[0], N_DEV * b.shape[1]))
        else:
            full.append(blk.reshape(N_DEV * b.shape[0], b.shape[1]))
    ffn_in0, ffn_in1, ffn_out0, ffn_out1, gdn_in_f, gdn_out_f, kv_f_full, fox_in_f, fox_out_f = full
    wts = {
        "ffn_in0": ffn_in0, "ffn_in1": ffn_in1, "ffn_out0": ffn_out0, "ffn_out1": ffn_out1,
        "gdn_in": gdn_in_f[:, :GDN_MAIN],
        "gdn_ba": jnp.pad(gdn_in_f[:, GDN_MAIN:], ((0, 0), (0, LANES - 2 * GDN_V_HEADS))),
        "gdn_out": gdn_out_f,
        "kv": kv_f_full[:, :2 * FOX_KV_DIM],
        "kv_f": jnp.pad(kv_f_full[:, 2 * FOX_KV_DIM:], ((0, 0), (0, LANES - FOX_HEADS))),
        "fox_in": fox_in_f, "fox_out": fox_out_f,
    }
    car = {k: jnp.zeros(w.shape, F32) for k, w in wts.items()}
    small = {n: w_in[n] for n in _SMALL_NAMES if n not in ("ada_b", "kv_ada_b", "out_ada_b")}
    small["gdn_conv"] = conv_full

    loss_fn = functools.partial(_local_loss, wts=wts, target=loss_target[0])
    loss_local, vjp = jax.vjp(loss_fn, x[0], mods, small, car)
    gx, gmods, gsmall, gcar = vjp(jnp.ones((), F32))
    loss = lax.psum(loss_local, ("x", "y", "c"))

    gsm = dict(gsmall)
    gsm["ada_b"] = jnp.stack([gmods["ada0"], gmods["ada1"]])
    gsm["kv_ada_b"] = gmods["kvada"]
    gsm["out_ada_b"] = gmods["outada"]
    small_sizes = [w_in[n].size for n in _SMALL_NAMES]
    n_small = sum(small_sizes) + CONV_K * CONV_DIM
    small_rows = -(-n_small // (LANES * 8)) * 8
    vec = _pack_flat([gsm[n] for n in _SMALL_NAMES] + [gsm["gdn_conv"]], small_rows * LANES)
    vec_all = _all_gather(vec.reshape(small_rows, LANES), name="ag_small_grads", in_vmem=True)
    vec_sum = _sum_slots(vec_all, name="sum_small_grads").reshape(-1)
    grads = {}
    off = 0
    for n, sz in zip(_SMALL_NAMES, small_sizes):
        grads[n] = vec_sum[off:off + sz].reshape(w_in[n].shape)
        off += sz
    conv_sum = vec_sum[off:off + CONV_K * CONV_DIM].reshape(CONV_K, N_DEV, CONV_DIM // N_DEV)
    grads["gdn_conv"] = lax.dynamic_index_in_dim(conv_sum, me, axis=1, keepdims=False)[None]

    vec_flat = vec_all.reshape(N_DEV, -1)
    d_ada = vec_flat[:, :2 * 6 * d].reshape(N_DEV, 2, N_DEV, s6)
    o_kv = sum(small_sizes[:6])
    d_kvada = vec_flat[:, o_kv:o_kv + 2 * d].reshape(N_DEV, N_DEV, s2)
    o_out = sum(small_sizes[:11])
    d_outada = vec_flat[:, o_out:o_out + 2 * d].reshape(N_DEV, N_DEV, s2)
    pick = lambda t, axis: lax.dynamic_index_in_dim(t, me, axis=axis, keepdims=False)
    d_cat = jnp.concatenate([pick(d_ada[:, 0], 1), pick(d_ada[:, 1], 1), pick(d_kvada, 1), pick(d_outada, 1)],
                            axis=1)
    g_ada_cat = _matmul(cond_pad, _pad_rows(d_cat, LANES), trans_a=True, name="mm_ada_dw")
    grads["ada_w"] = jnp.stack([g_ada_cat[:, :s6], g_ada_cat[:, s6:2 * s6]])
    grads["kv_ada_w"] = g_ada_cat[:, 2 * s6:2 * s6 + s2]
    grads["out_ada_w"] = g_ada_cat[:, 2 * s6 + s2:]

    g_gdn_in = jnp.concatenate([gcar["gdn_in"], gcar["gdn_ba"][:, :2 * GDN_V_HEADS]], axis=1)
    g_kv = jnp.concatenate([gcar["kv"], gcar["kv_f"][:, :FOX_HEADS]], axis=1)
    g_full = [gcar["ffn_in0"], gcar["ffn_in1"], gcar["ffn_out0"], gcar["ffn_out1"], g_gdn_in, gcar["gdn_out"], g_kv,
              gcar["fox_in"], gcar["fox_out"]]
    by_owner = []
    for g, b, cs in zip(g_full, big, col_sharded):
        if cs:
            t = jnp.transpose(g.reshape(b.shape[0], N_DEV, b.shape[1]), (1, 0, 2))
        else:
            t = g
        by_owner.append(t.reshape(N_DEV, b.size).astype(BF16))
    by_owner.append(jnp.zeros((N_DEV, total - sum(sizes)), BF16))
    n_chip = N_DEV // 2
    send = jnp.concatenate(by_owner, axis=1).reshape(n_chip, 2, pack_rows, PACK_COLS)
    my_c = lax.axis_index("c")
    for_my_core = lax.dynamic_index_in_dim(send, my_c, axis=1, keepdims=False)
    for_sibling_core = lax.dynamic_index_in_dim(send, 1 - my_c, axis=1, keepdims=False)
    from_sibling = _exchange_sibling(for_sibling_core, name="rs_sibling")
    pair_sum = _add_pair(for_my_core, from_sibling, name="rs_pair_sum")
    by_chip = _exchange_chips(pair_sum, name="rs_chips")
    g_mine = _sum_slots(by_chip, name="sum_grads").reshape(-1)
    shard_grads = []
    off = 0
    for b, sz in zip(big, sizes):
        shard_grads.append(g_mine[off:off + sz].reshape(b.shape))
        off += sz
    grads["ffn_w_in"] = jnp.stack(shard_grads[0:2])
    grads["ffn_w_out"] = jnp.stack(shard_grads[2:4])
    grads["gdn_w_in"] = shard_grads[4][None]
    grads["gdn_w_out"] = shard_grads[5][None]
    grads["kv_w"] = shard_grads[6]
    grads["fox_w_in"] = shard_grads[7][None]
    grads["fox_w_out"] = shard_grads[8][None]

    delta, new_m, new_v = {}, {}, {}
    for n in names:
        if n in _SMALL_NAMES:
            continue
        delta[n], new_m[n], new_v[n] = _adamw(w_in[n], grads[n], m_in[n], v_in[n], name=f"adamw_{n}")
    sm_rows = -(-sum(small_sizes) // (LANES * 8)) * 8
    pk = lambda src: _pack_flat([src[n] for n in _SMALL_NAMES], sm_rows * LANES).reshape(sm_rows, LANES)
    d_pk, m_pk, vn_pk = _adamw(pk(w_in), pk(grads), pk(m_in), pk(v_in), name="adamw_small")
    off = 0
    for n, sz in zip(_SMALL_NAMES, small_sizes):
        shp = w_in[n].shape
        delta[n] = d_pk.reshape(-1)[off:off + sz].reshape(shp)
        new_m[n] = m_pk.reshape(-1)[off:off + sz].reshape(shp)
        new_v[n] = vn_pk.reshape(-1)[off:off + sz].reshape(shp)
        off += sz

    return (loss, gx[None], *[grads[n] for n in names], *[delta[n] for n in names],
            *[new_m[n] for n in names], *[new_v[n] for n in names])
```

```python
import functools

import jax
import jax.numpy as jnp
from jax import lax
from jax.experimental import pallas as pl
from jax.experimental.pallas import tpu as pltpu

F32 = jnp.float32
BF16 = jnp.bfloat16
MESH = pl.DeviceIdType.MESH
N_DEV = 8

D_MODEL = 2048
SEQ = 8192
GDN_QK_HEADS = 16
GDN_V_HEADS = 32
GDN_HEAD_DIM = 128
GDN_QK_DIM = GDN_QK_HEADS * GDN_HEAD_DIM
GDN_V_DIM = GDN_V_HEADS * GDN_HEAD_DIM
CONV_DIM = 2 * GDN_QK_DIM + GDN_V_DIM
GDN_MAIN = CONV_DIM + GDN_V_DIM
GDN_PROJ = GDN_MAIN + 2 * GDN_V_HEADS
CONV_K = 4
GDN_CHUNK = 64
FOX_HEADS = 16
FOX_KV_HEADS = 2
FOX_GROUP = FOX_HEADS // FOX_KV_HEADS
FOX_HEAD_DIM = 256
FOX_Q_DIM = FOX_HEADS * FOX_HEAD_DIM
FOX_KV_DIM = FOX_KV_HEADS * FOX_HEAD_DIM
KV_PROJ = 2 * FOX_KV_DIM + FOX_HEADS
FFN_HIDDEN = 5632
NORM_EPS = 1e-6

ADAM_LR = 0.001
ADAM_B1 = 0.9
ADAM_B2 = 0.999
ADAM_EPS = 1e-08
ADAM_WD = 0.01
ADAM_STEP = 10

LANES = 128
VMEM_LIMIT = 56 * 1024 * 1024
PACK_COLS = 1024
ATT_BLOCK = 512
NEG = -1e30


def _pick(n, cands):
    for c in cands:
        if n % c == 0:
            return c
    return n


def _cparams(sem):
    return pltpu.CompilerParams(dimension_semantics=sem, vmem_limit_bytes=VMEM_LIMIT)


def _matmul(a, b, *, trans_a=False, trans_b=False, b_dev_blocks=False, out_dev_blocks=False, out_dtype=F32, name):
    if trans_a:
        kdim, m = a.shape
    else:
        m, kdim = a.shape
    if b_dev_blocks:
        n_dev, rows_b, per_dev = b.shape
        n, kb = (rows_b, n_dev * per_dev) if trans_b else (n_dev * per_dev, rows_b)
    elif trans_b:
        n, kb = b.shape
    else:
        kb, n = b.shape
    assert kdim == kb, (a.shape, b.shape)
    tm = _pick(m, (1024, 512, 256, 128))
    tn = _pick(n // N_DEV if (out_dev_blocks or (b_dev_blocks and not trans_b)) else n, (1408, 1024, 512, 256, 128))
    tk = _pick(kdim // N_DEV if (b_dev_blocks and trans_b) else kdim, (1024, 1408, 512, 256, 128))
    nk = kdim // tk
    dims = (((0 if trans_a else 1,), (1 if trans_b else 0,)), ((), ()))

    def body(a_ref, b_ref, o_ref, acc_ref):
        k = pl.program_id(2)

        @pl.when(k == 0)
        def _():
            acc_ref[...] = jnp.zeros_like(acc_ref)

        acc_ref[...] += lax.dot_general(a_ref[...].astype(BF16), b_ref[...].astype(BF16), dims,
                                        preferred_element_type=F32)

        @pl.when(k == nk - 1)
        def _():
            o_ref[...] = acc_ref[...].astype(o_ref.dtype)

    a_spec = (pl.BlockSpec((tk, tm), lambda i, j, k: (k, i)) if trans_a
              else pl.BlockSpec((tm, tk), lambda i, j, k: (i, k)))
    if b_dev_blocks and trans_b:
        per = kdim // N_DEV // tk
        b_spec = pl.BlockSpec((None, tn, tk), lambda i, j, k: (k // per, j, k % per))
    elif b_dev_blocks:
        per = n // N_DEV // tn
        b_spec = pl.BlockSpec((None, tk, tn), lambda i, j, k: (j // per, k, j % per))
    elif trans_b:
        b_spec = pl.BlockSpec((tn, tk), lambda i, j, k: (j, k))
    else:
        b_spec = pl.BlockSpec((tk, tn), lambda i, j, k: (k, j))
    if out_dev_blocks:
        per_o = n // N_DEV // tn
        out_spec = pl.BlockSpec((None, tm, tn), lambda i, j, k: (j // per_o, i, j % per_o))
        out_shape = jax.ShapeDtypeStruct((N_DEV, m, n // N_DEV), out_dtype)
    else:
        out_spec = pl.BlockSpec((tm, tn), lambda i, j, k: (i, j))
        out_shape = jax.ShapeDtypeStruct((m, n), out_dtype)
    return pl.pallas_call(
        body, name=name,
        grid=(m // tm, n // tn, nk),
        in_specs=[a_spec, b_spec],
        out_specs=out_spec,
        out_shape=out_shape,
        scratch_shapes=[pltpu.VMEM((tm, tn), F32)],
        compiler_params=_cparams(("parallel", "parallel", "arbitrary")),
    )(a, b)


def _make_mm(tag, dev_blocks=False):
    @jax.custom_vjp
    def mm(a, w, carrier):
        return _matmul(a, w, b_dev_blocks=dev_blocks, name=f"mm_{tag}")

    def fwd(a, w, carrier):
        return _matmul(a, w, b_dev_blocks=dev_blocks, name=f"mm_{tag}"), (a, w)

    def bwd(res, g):
        a, w = res
        da = _matmul(g, w, trans_b=True, b_dev_blocks=dev_blocks, name=f"mm_{tag}_da")
        dw = _matmul(a, g, trans_a=True, out_dev_blocks=dev_blocks, out_dtype=BF16, name=f"mm_{tag}_dw")
        return da, jnp.zeros_like(w), dw

    mm.defvjp(fwd, bwd)
    return mm


_NT = (((1,), (1,)), ((), ()))
_FOX_SCALE = FOX_HEAD_DIM ** -0.5


def _fox_fwd_call(q, kb, vb, f_col, f_row):
    length = q.shape[0]
    t = ATT_BLOCK
    tq = 2 * t
    nq = length // tq
    hd = FOX_HEAD_DIM

    def body(q_ref, k_ref, v_ref, fq_ref, fk_ref, o_ref, lse_ref, qs_sc, m_sc, l_sc, acc_sc):
        i = pl.program_id(1)
        qs_sc[...] = (q_ref[...] * _FOX_SCALE).astype(BF16)
        m_sc[...] = jnp.full_like(m_sc, NEG)
        l_sc[...] = jnp.zeros_like(l_sc)
        acc_sc[...] = jnp.zeros_like(acc_sc)

        def block(j, modes):
            off = pl.multiple_of(j * t, t)
            kj = k_ref[pl.ds(off, t), :]
            vj = v_ref[pl.ds(off, t), :]
            fk = fk_ref[0, :, pl.ds(off, t)]
            for a in (0, 1):
                if modes[a] is None:
                    continue
                rows = pl.ds(a * t, t)
                s = lax.dot_general(qs_sc[rows, :], kj, _NT, preferred_element_type=F32)
                s = s + fq_ref[0, rows, :] - fk
                if modes[a]:
                    r = lax.broadcasted_iota(jnp.int32, (t, t), 0)
                    c = lax.broadcasted_iota(jnp.int32, (t, t), 1)
                    s = jnp.where(r >= c, s, NEG)
                m_old = m_sc[a]
                m_new = jnp.maximum(m_old, jnp.max(s, axis=1, keepdims=True))
                alpha = jnp.exp(m_old - m_new)
                p = jnp.exp(s - m_new)
                l_sc[a] = alpha * l_sc[a] + jnp.sum(p, axis=1, keepdims=True)
                acc_sc[a] = alpha * acc_sc[a] + jnp.dot(p.astype(BF16), vj, preferred_element_type=F32)
                m_sc[a] = m_new

        def loop_body(j, carry):
            block(j, (False, False))
            return carry

        lax.fori_loop(0, 2 * i, loop_body, 0)
        block(2 * i, (True, False))
        block(2 * i + 1, (None, True))
        for a in (0, 1):
            rows = pl.ds(a * t, t)
            o_ref[rows, :] = acc_sc[a] / l_sc[a]
            lse_ref[0, rows, :] = m_sc[a] + jnp.log(l_sc[a])

    return pl.pallas_call(
        body, name="fox_fwd",
        grid=(FOX_HEADS, nq),
        in_specs=[
            pl.BlockSpec((tq, hd), lambda h, i: (i, h)),
            pl.BlockSpec((length, hd), lambda h, i: (0, h // FOX_GROUP)),
            pl.BlockSpec((length, hd), lambda h, i: (0, h // FOX_GROUP)),
            pl.BlockSpec((1, tq, 1), lambda h, i: (h, i, 0)),
            pl.BlockSpec((1, 1, length), lambda h, i: (h, 0, 0)),
        ],
        out_specs=[
            pl.BlockSpec((tq, hd), lambda h, i: (i, h)),
            pl.BlockSpec((1, tq, 1), lambda h, i: (h, i, 0)),
        ],
        out_shape=[jax.ShapeDtypeStruct((length, FOX_Q_DIM), F32),
                   jax.ShapeDtypeStruct((FOX_HEADS, length, 1), F32)],
        scratch_shapes=[pltpu.VMEM((tq, hd), BF16), pltpu.VMEM((2, t, 1), F32), pltpu.VMEM((2, t, 1), F32),
                        pltpu.VMEM((2, t, hd), F32)],
        compiler_params=_cparams(("parallel", "arbitrary")),
    )(q, kb, vb, f_col, f_row)


def _fox_dq_call(q, kb, vb, f_col, f_row, lse_col, do):
    length = q.shape[0]
    t = ATT_BLOCK
    tq = 2 * t
    nq = length // tq
    hd = FOX_HEAD_DIM

    def body(q_ref, k_ref, v_ref, fq_ref, fk_ref, lse_ref, do_ref, dq_ref, dl_ref, qs_sc, do_sc, acc_sc, dl_sc):
        i = pl.program_id(1)
        qs_sc[...] = (q_ref[...] * _FOX_SCALE).astype(BF16)
        do_sc[...] = do_ref[...].astype(BF16)
        acc_sc[...] = jnp.zeros_like(acc_sc)
        dl_sc[...] = jnp.zeros_like(dl_sc)

        def block(j, modes, second_pass):
            off = pl.multiple_of(j * t, t)
            kj = k_ref[pl.ds(off, t), :]
            vj = v_ref[pl.ds(off, t), :]
            fk = fk_ref[0, :, pl.ds(off, t)]
            for a in (0, 1):
                if modes[a] is None:
                    continue
                rows = pl.ds(a * t, t)
                s = lax.dot_general(qs_sc[rows, :], kj, _NT, preferred_element_type=F32)
                s = s + fq_ref[0, rows, :] - fk
                if modes[a]:
                    r = lax.broadcasted_iota(jnp.int32, (t, t), 0)
                    c = lax.broadcasted_iota(jnp.int32, (t, t), 1)
                    s = jnp.where(r >= c, s, NEG)
                p = jnp.exp(s - lse_ref[0, rows, :])
                dp = lax.dot_general(do_sc[rows, :], vj, _NT, preferred_element_type=F32)
                if second_pass:
                    ds = p * (dp - dl_sc[a])
                    acc_sc[a] += jnp.dot(ds.astype(BF16), kj, preferred_element_type=F32)
                else:
                    dl_sc[a] += jnp.sum(p * dp, axis=1, keepdims=True)

        def sweep(second_pass):
            def loop_body(j, carry):
                block(j, (False, False), second_pass)
                return carry

            lax.fori_loop(0, 2 * i, loop_body, 0)
            block(2 * i, (True, False), second_pass)
            block(2 * i + 1, (None, True), second_pass)

        sweep(False)
        sweep(True)
        for a in (0, 1):
            rows = pl.ds(a * t, t)
            dq_ref[rows, :] = acc_sc[a] * _FOX_SCALE
            dl_ref[0, rows, :] = dl_sc[a]

    col = pl.BlockSpec((1, tq, 1), lambda h, i: (h, i, 0))
    return pl.pallas_call(
        body, name="fox_dq",
        grid=(FOX_HEADS, nq),
        in_specs=[
            pl.BlockSpec((tq, hd), lambda h, i: (i, h)),
            pl.BlockSpec((length, hd), lambda h, i: (0, h // FOX_GROUP)),
            pl.BlockSpec((length, hd), lambda h, i: (0, h // FOX_GROUP)),
            col,
            pl.BlockSpec((1, 1, length), lambda h, i: (h, 0, 0)),
            col,
            pl.BlockSpec((tq, hd), lambda h, i: (i, h)),
        ],
        out_specs=[pl.BlockSpec((tq, hd), lambda h, i: (i, h)), col],
        out_shape=[jax.ShapeDtypeStruct((length, FOX_Q_DIM), F32),
                   jax.ShapeDtypeStruct((FOX_HEADS, length, 1), F32)],
        scratch_shapes=[pltpu.VMEM((tq, hd), BF16), pltpu.VMEM((tq, hd), BF16), pltpu.VMEM((2, t, hd), F32),
                        pltpu.VMEM((2, t, 1), F32)],
        compiler_params=_cparams(("parallel", "arbitrary")),
    )(q, kb, vb, f_col, f_row, lse_col, do)


def _fox_dkv_call(qs, dob, kb, vb, f_col, f_row, lse_row, dl_row):
    length = qs.shape[0]
    t = ATT_BLOCK
    tk = 2 * t
    nq = length // t
    hd = FOX_HEAD_DIM

    def body(k_ref, v_ref, q_ref, do_ref, fk_ref, fq_ref, lse_ref, dl_ref,
             dk_ref, dv_ref, df_ref, dk_sc, dv_sc, df_sc):
        j = pl.program_id(1)
        dk_sc[...] = jnp.zeros_like(dk_sc)
        dv_sc[...] = jnp.zeros_like(dv_sc)
        df_sc[...] = jnp.zeros_like(df_sc)

        def block(i, modes):
            off = pl.multiple_of(i * t, t)
            qi = q_ref[pl.ds(off, t), :]
            doi = do_ref[pl.ds(off, t), :]
            fq = fq_ref[0, :, pl.ds(off, t)]
            lse = lse_ref[0, :, pl.ds(off, t)]
            dl = dl_ref[0, :, pl.ds(off, t)]
            for b in (0, 1):
                if modes[b] is None:
                    continue
                keys = pl.ds(b * t, t)
                st = lax.dot_general(k_ref[keys, :], qi, _NT, preferred_element_type=F32)
                st = st + fq - fk_ref[0, keys, :]
                if modes[b]:
                    r = lax.broadcasted_iota(jnp.int32, (t, t), 0)
                    c = lax.broadcasted_iota(jnp.int32, (t, t), 1)
                    st = jnp.where(c >= r, st, NEG)
                pt = jnp.exp(st - lse)
                dv_sc[b] += jnp.dot(pt.astype(BF16), doi, preferred_element_type=F32)
                dpt = lax.dot_general(v_ref[keys, :], doi, _NT, preferred_element_type=F32)
                dst = pt * (dpt - dl)
                dk_sc[b] += jnp.dot(dst.astype(BF16), qi, preferred_element_type=F32)
                df_sc[b] += jnp.sum(dst, axis=1, keepdims=True)

        def loop_body(i, carry):
            block(i, (False, False))
            return carry

        block(2 * j, (True, None))
        block(2 * j + 1, (False, True))
        lax.fori_loop(2 * j + 2, nq, loop_body, 0)
        for b in (0, 1):
            keys = pl.ds(b * t, t)
            dk_ref[keys, :] = dk_sc[b]
            dv_ref[keys, :] = dv_sc[b]
            df_ref[0, keys, :] = -df_sc[b]

    row = pl.BlockSpec((1, 1, length), lambda h, j: (h, 0, 0))
    return pl.pallas_call(
        body, name="fox_dkv",
        grid=(FOX_HEADS, length // tk),
        in_specs=[
            pl.BlockSpec((tk, hd), lambda h, j: (j, h // FOX_GROUP)),
            pl.BlockSpec((tk, hd), lambda h, j: (j, h // FOX_GROUP)),
            pl.BlockSpec((length, hd), lambda h, j: (0, h)),
            pl.BlockSpec((length, hd), lambda h, j: (0, h)),
            pl.BlockSpec((1, tk, 1), lambda h, j: (h, j, 0)),
            row, row, row,
        ],
        out_specs=[
            pl.BlockSpec((tk, hd), lambda h, j: (j, h)),
            pl.BlockSpec((tk, hd), lambda h, j: (j, h)),
            pl.BlockSpec((1, tk, 1), lambda h, j: (h, j, 0)),
        ],
        out_shape=[jax.ShapeDtypeStruct((length, FOX_Q_DIM), F32),
                   jax.ShapeDtypeStruct((length, FOX_Q_DIM), F32),
                   jax.ShapeDtypeStruct((FOX_HEADS, length, 1), F32)],
        scratch_shapes=[pltpu.VMEM((2, t, hd), F32), pltpu.VMEM((2, t, hd), F32), pltpu.VMEM((2, t, 1), F32)],
        compiler_params=_cparams(("parallel", "arbitrary")),
    )(kb, vb, qs, dob, f_col, f_row, lse_row, dl_row)


@jax.custom_vjp
def _fox_attention(q, k, v, f):
    return _fox_attention_fwd(q, k, v, f)[0]


def _fox_attention_fwd(q, k, v, f):
    kb = k.astype(BF16)
    vb = v.astype(BF16)
    f_row = jnp.transpose(f)[:, None, :]
    f_col = jnp.transpose(f)[:, :, None]
    o, lse = _fox_fwd_call(q, kb, vb, f_col, f_row)
    return o, (q, kb, vb, f_col, f_row, lse)


def _fox_attention_bwd(res, do):
    q, kb, vb, f_col, f_row, lse = res
    length = q.shape[0]
    lse_row = jnp.transpose(lse, (0, 2, 1))
    dq, dl_col = _fox_dq_call(q, kb, vb, f_col, f_row, lse, do)
    dl_row = jnp.transpose(dl_col, (0, 2, 1))
    qs =(q * _FOX_SCALE).astype(BF16)
    dkh, dvh, dfk = _fox_dkv_call(qs, do.astype(BF16), kb, vb, f_col, f_row, lse_row, dl_row)
    dk = jnp.sum(dkh.reshape(length, FOX_KV_HEADS, FOX_GROUP, FOX_HEAD_DIM), axis=2).reshape(length, FOX_KV_DIM)
    dv = jnp.sum(dvh.reshape(length, FOX_KV_HEADS, FOX_GROUP, FOX_HEAD_DIM), axis=2).reshape(length, FOX_KV_DIM)
    df = jnp.transpose(dfk[:, :, 0])
    return dq, dk, dv, df


_fox_attention.defvjp(_fox_attention_fwd, _fox_attention_bwd)


def _peer(mx, my, mc, k):
    return (1 - mx if k & 4 else mx, 1 - my if k & 2 else my, 1 - mc if k & 1 else mc)


def _all_gather(x, *, name, in_vmem):
    space = pltpu.VMEM if in_vmem else pl.ANY

    def body(x_ref, out_ref, send_sems, recv_sems, local_sem):
        mx, my, mc = lax.axis_index("x"), lax.axis_index("y"), lax.axis_index("c")
        me = 4 * mx + 2 * my + mc
        mine = pltpu.make_async_copy(x_ref, out_ref.at[me], local_sem)
        mine.start()
        copies = []
        for k in range(1, N_DEV):
            cp = pltpu.make_async_remote_copy(
                src_ref=x_ref, dst_ref=out_ref.at[me],
                send_sem=send_sems.at[k - 1], recv_sem=recv_sems.at[k - 1],
                device_id=_peer(mx, my, mc, k), device_id_type=MESH)
            cp.start()
            copies.append(cp)
        for cp in copies:
            cp.wait()
        mine.wait()

    return pl.pallas_call(
        body, name=name,
        out_shape=jax.ShapeDtypeStruct((N_DEV,) + x.shape, x.dtype),
        in_specs=[pl.BlockSpec(memory_space=space)],
        out_specs=pl.BlockSpec(memory_space=space),
        scratch_shapes=[pltpu.SemaphoreType.DMA((N_DEV - 1,)), pltpu.SemaphoreType.DMA((N_DEV - 1,)),
                        pltpu.SemaphoreType.DMA],
    )(x)


N_CHIP = N_DEV // 2
_HBM = pl.BlockSpec(memory_space=pl.ANY)


def _all_gather_by_chip(xs, *, name):
    n_arr = len(xs)
    per = N_DEV - 1

    def body(*refs):
        x_refs, out_refs = refs[:n_arr], refs[n_arr:2 * n_arr]
        send_sems, recv_sems, local_sems = refs[2 * n_arr:]
        mx, my, mc = lax.axis_index("x"), lax.axis_index("y"), lax.axis_index("c")
        me, sibling = (mx, my, mc), (mx, my, 1 - mc)
        chips = [(1 - mx, my), (mx, 1 - my), (1 - mx, 1 - my)]

        def copy(a, k, block, to, from_input=False):
            px, py, pc = block
            slot = out_refs[a].at[4 * px + 2 * py + pc]
            return pltpu.make_async_remote_copy(
                src_ref=x_refs[a] if from_input else slot, dst_ref=slot,
                send_sem=send_sems.at[a * per + k], recv_sem=recv_sems.at[a * per + k],
                device_id=to, device_id_type=MESH)

        mine = [pltpu.make_async_copy(x_refs[a], out_refs[a].at[4 * mx + 2 * my + mc], local_sems.at[a])
                for a in range(n_arr)]
        for cp in mine:
            cp.start()
        sent = []
        for a in range(n_arr):
            sent.append(copy(a, 0, me, sibling, from_input=True))
            sent += [copy(a, 1 + j, me, (*chip, mc), from_input=True) for j, chip in enumerate(chips)]
        for cp in sent:
            cp.start()
        for a in range(n_arr):
            for j, chip in enumerate(chips):
                copy(a, 1 + j, (*chip, mc), me).wait_recv()
                passed = copy(a, 4 + j, (*chip, mc), sibling)
                passed.start()
                sent.append(passed)
        for a in range(n_arr):
            copy(a, 0, sibling, me).wait_recv()
            for j, chip in enumerate(chips):
                copy(a, 4 + j, (*chip, 1 - mc), me).wait_recv()
        for cp in sent:
            cp.wait_send()
        for cp in mine:
            cp.wait()

    return pl.pallas_call(
        body, name=name,
        out_shape=[jax.ShapeDtypeStruct((N_DEV,) + x.shape, x.dtype) for x in xs],
        in_specs=[_HBM] * n_arr,
        out_specs=[_HBM] * n_arr,
        scratch_shapes=[pltpu.SemaphoreType.DMA((n_arr * per,)), pltpu.SemaphoreType.DMA((n_arr * per,)),
                        pltpu.SemaphoreType.DMA((n_arr,))],
    )(*xs)


def _exchange_sibling(xs, *, name):
    n_arr = len(xs)

    def body(*refs):
        x_refs, out_refs = refs[:n_arr], refs[n_arr:2 * n_arr]
        send_sems, recv_sems = refs[2 * n_arr:]
        mc = lax.axis_index("c")
        sibling = (lax.axis_index("x"), lax.axis_index("y"), 1 - mc)
        copies = [pltpu.make_async_remote_copy(
            src_ref=x_refs[a].at[q, 1 - mc], dst_ref=out_refs[a].at[q],
            send_sem=send_sems.at[a * N_CHIP + q], recv_sem=recv_sems.at[a * N_CHIP + q],
            device_id=sibling, device_id_type=MESH) for a in range(n_arr) for q in range(N_CHIP)]
        for cp in copies:
            cp.start()
        for cp in copies:
            cp.wait()

    return pl.pallas_call(
        body, name=name,
        out_shape=[jax.ShapeDtypeStruct((N_CHIP,) + x.shape[2:], x.dtype) for x in xs],
        in_specs=[_HBM] * n_arr,
        out_specs=[_HBM] * n_arr,
        scratch_shapes=[pltpu.SemaphoreType.DMA((n_arr * N_CHIP,)), pltpu.SemaphoreType.DMA((n_arr * N_CHIP,))],
    )(*xs)


def _exchange_chips(xs, *, name):
    n_arr = len(xs)
    per = N_CHIP - 1

    def body(*refs):
        x_refs, out_refs = refs[:n_arr], refs[n_arr:2 * n_arr]
        send_sems, recv_sems, local_sems = refs[2 * n_arr:]
        mx, my, mc = lax.axis_index("x"), lax.axis_index("y"), lax.axis_index("c")
        my_chip = 2 * mx + my
        copies = []
        for a in range(n_arr):
            mine = pltpu.make_async_copy(x_refs[a].at[my_chip], out_refs[a].at[my_chip], local_sems.at[a])
            mine.start()
            copies.append(mine)
            for j, (px, py) in enumerate([(1 - mx, my), (mx, 1 - my), (1 - mx, 1 - my)]):
                cp = pltpu.make_async_remote_copy(
                    src_ref=x_refs[a].at[2 * px + py], dst_ref=out_refs[a].at[my_chip],
                    send_sem=send_sems.at[a * per + j], recv_sem=recv_sems.at[a * per + j],
                    device_id=(px, py, mc), device_id_type=MESH)
                cp.start()
                copies.append(cp)
        for cp in copies:
            cp.wait()

    return pl.pallas_call(
        body, name=name,
        out_shape=[jax.ShapeDtypeStruct(x.shape, x.dtype) for x in xs],
        in_specs=[_HBM] * n_arr,
        out_specs=[_HBM] * n_arr,
        scratch_shapes=[pltpu.SemaphoreType.DMA((n_arr * per,)), pltpu.SemaphoreType.DMA((n_arr * per,)),
                        pltpu.SemaphoreType.DMA((n_arr,))],
    )(*xs)


def _add_pair(a, b, *, name):
    n, r, c = a.shape
    tr = _pick(r, (512, 256, 128, 64, 32, 16))

    def body(a_ref, b_ref, o_ref):
        o_ref[...] = (a_ref[...].astype(F32) + b_ref[...].astype(F32)).astype(o_ref.dtype)

    spec = pl.BlockSpec((1, tr, c), lambda q, i: (q, i, 0))
    return pl.pallas_call(
        body, name=name,
        grid=(n, r // tr),
        in_specs=[spec, spec],
        out_specs=spec,
        out_shape=jax.ShapeDtypeStruct(a.shape, a.dtype),
        compiler_params=_cparams(("parallel", "parallel")),
    )(a, b)


def _sum_slots(x, *, name):
    n, r, c = x.shape
    tr = _pick(r, (256, 128, 64, 32, 16, 8))

    def body(x_ref, o_ref):
        acc = x_ref[0].astype(F32)
        for s in range(1, n):
            acc = acc + x_ref[s].astype(F32)
        o_ref[...] = acc

    return pl.pallas_call(
        body, name=name,
        grid=(r // tr,),
        in_specs=[pl.BlockSpec((n, tr, c), lambda i: (0, i, 0))],
        out_specs=pl.BlockSpec((tr, c), lambda i: (i, 0)),
        out_shape=jax.ShapeDtypeStruct((r, c), F32),
        compiler_params=_cparams(("parallel",)),
    )(x)


def _adamw(w, g, m, v, *, name):
    shape = w.shape
    c = shape[-1]
    r = w.size // c
    w2, g2, m2, v2 = (t.reshape(r, c) for t in (w, g, m, v))
    tr = _pick(r, (128, 64, 32, 16, 8))

    def body(w_ref, g_ref, m_ref, v_ref, d_ref, mo_ref, vo_ref):
        gv = g_ref[...]
        mn = ADAM_B1 * m_ref[...] + (1.0 - ADAM_B1) * gv
        vn = ADAM_B2 * v_ref[...] + (1.0 - ADAM_B2) * (gv * gv)
        m_hat = mn / (1.0 - ADAM_B1 ** ADAM_STEP)
        v_hat = vn / (1.0 - ADAM_B2 ** ADAM_STEP)
        d_ref[...] = -ADAM_LR * (m_hat / (jnp.sqrt(v_hat) + ADAM_EPS) + ADAM_WD * w_ref[...])
        mo_ref[...] = mn
        vo_ref[...] = vn

    spec = pl.BlockSpec((tr, c), lambda i: (i, 0))
    outs = pl.pallas_call(
        body, name=name,
        grid=(r // tr,),
        in_specs=[spec] * 4,
        out_specs=[spec] * 3,
        out_shape=[jax.ShapeDtypeStruct((r, c), F32)] * 3,
        compiler_params=_cparams(("parallel",)),
    )(w2, g2, m2, v2)
    return tuple(o.reshape(shape) for o in outs)


GDN_HEADS_PER_STEP = 8
_TN = (((0,), (0,)), ((), ()))


def _gdn_scan_specs(qd, u, attn, reverse):
    h, n, c, dk = qd.shape
    dv = u.shape[-1]
    g = GDN_HEADS_PER_STEP
    idx = (lambda hb, i: (hb, n - 1 - i, 0, 0)) if reverse else (lambda hb, i: (hb, i, 0, 0))
    return dict(
        qk=pl.BlockSpec((g, 1, c, dk), idx), uv=pl.BlockSpec((g, 1, c, dv), idx),
        attn=pl.BlockSpec((g, 1, c, c), idx), lane=pl.BlockSpec((g, 1, 1, dv), idx),
        state=pl.BlockSpec((g, 1, dk, dv), idx), grid=(h // g, n))


def _gdn_scan_fwd_call(qd, kd, u, w, attn, egl):
    h, n, c, dk = qd.shape
    dv = u.shape[-1]
    sp = _gdn_scan_specs(qd, u, attn, reverse=False)

    def body(q_ref, k_ref, u_ref, w_ref, a_ref, e_ref, o_ref, st_ref, s_sc):
        @pl.when(pl.program_id(1) == 0)
        def _():
            s_sc[...] = jnp.zeros_like(s_sc)

        for g in range(GDN_HEADS_PER_STEP):
            s = s_sc[g]
            st_ref[g, 0] = s
            sb = s.astype(BF16)
            v_new = u_ref[g, 0] - jnp.dot(w_ref[g, 0].astype(BF16), sb, preferred_element_type=F32)
            vb = v_new.astype(BF16)
            o_ref[g, 0] = (jnp.dot(q_ref[g, 0].astype(BF16), sb, preferred_element_type=F32)
                           + jnp.dot(a_ref[g, 0].astype(BF16), vb, preferred_element_type=F32))
            s_sc[g] = s * e_ref[g, 0] + lax.dot_general(k_ref[g, 0].astype(BF16), vb, _TN,
                                                        preferred_element_type=F32)

    return pl.pallas_call(
        body, name="gdn_scan_fwd",
        grid=sp["grid"],
        in_specs=[sp["qk"], sp["qk"], sp["uv"], sp["qk"], sp["attn"], sp["lane"]],
        out_specs=[sp["uv"], sp["state"]],
        out_shape=[jax.ShapeDtypeStruct((h, n, c, dv), F32), jax.ShapeDtypeStruct((h, n, dk, dv), F32)],
        scratch_shapes=[pltpu.VMEM((GDN_HEADS_PER_STEP, dk, dv), F32)],
        compiler_params=_cparams(("parallel", "arbitrary")),
    )(qd, kd, u, w, attn, egl)


def _gdn_scan_bwd_call(qd, kd, u, w, attn, egl, states, do):
    h, n, c, dk = qd.shape
    dv = u.shape[-1]
    sp = _gdn_scan_specs(qd, u, attn, reverse=True)

    def body(q_ref, k_ref, u_ref, w_ref, a_ref, e_ref, st_ref, do_ref,
             dq_ref, dk_ref, du_ref, dw_ref, da_ref, de_ref, ds_sc):
        @pl.when(pl.program_id(1) == 0)
        def _():
            ds_sc[...] = jnp.zeros_like(ds_sc)

        for g in range(GDN_HEADS_PER_STEP):
            s = st_ref[g, 0]
            sb = s.astype(BF16)
            ds = ds_sc[g]
            dsb = ds.astype(BF16)
            qb, kb, wb, ab = (r[g, 0].astype(BF16) for r in (q_ref, k_ref, w_ref, a_ref))
            dob = do_ref[g, 0].astype(BF16)
            vb = (u_ref[g, 0] - jnp.dot(wb, sb, preferred_element_type=F32)).astype(BF16)
            dv_new = (lax.dot_general(ab, dob, _TN, preferred_element_type=F32)
                      + jnp.dot(kb, dsb, preferred_element_type=F32))
            dvb = dv_new.astype(BF16)
            dq_ref[g, 0] = lax.dot_general(dob, sb, _NT, preferred_element_type=F32)
            da_ref[g, 0] = lax.dot_general(dob, vb, _NT, preferred_element_type=F32)
            dk_ref[g, 0] = lax.dot_general(vb, dsb, _NT, preferred_element_type=F32)
            du_ref[g, 0] = dv_new
            dw_ref[g, 0] = -lax.dot_general(dvb, sb, _NT, preferred_element_type=F32)
            de_ref[g, 0] = jnp.sum(ds * s, axis=0, keepdims=True)
            ds_sc[g] = (ds * e_ref[g, 0] + lax.dot_general(qb, dob, _TN, preferred_element_type=F32)
                        - lax.dot_general(wb, dvb, _TN, preferred_element_type=F32))

    return pl.pallas_call(
        body, name="gdn_scan_bwd",
        grid=sp["grid"],
        in_specs=[sp["qk"], sp["qk"], sp["uv"], sp["qk"], sp["attn"], sp["lane"], sp["state"], sp["uv"]],
        out_specs=[sp["qk"], sp["qk"], sp["uv"], sp["qk"], sp["attn"], sp["lane"]],
        out_shape=[jax.ShapeDtypeStruct((h, n, c, dk), F32), jax.ShapeDtypeStruct((h, n, c, dk), F32),
                   jax.ShapeDtypeStruct((h, n, c, dv), F32), jax.ShapeDtypeStruct((h, n, c, dk), F32),
                   jax.ShapeDtypeStruct((h, n, c, c), F32), jax.ShapeDtypeStruct((h, n, 1, dv), F32)],
        scratch_shapes=[pltpu.VMEM((GDN_HEADS_PER_STEP, dk, dv), F32)],
        compiler_params=_cparams(("parallel", "arbitrary")),
    )(qd, kd, u, w, attn, egl, states, do)


GDN_CHUNKS_PER_STEP = 8
GDN_DIAG = 16


def _tri_solve_fwd_call(a, rhs):
    h, n, c, _ = a.shape
    r = rhs.shape[-1]
    g_n = GDN_CHUNKS_PER_STEP

    def dot(x, y):
        return jnp.dot(x.astype(BF16), y.astype(BF16), preferred_element_type=F32)

    def body(a_ref, rhs_ref, sol_ref, t_ref):
        ri = lax.broadcasted_iota(jnp.int32, (c, c), 0)
        ci = lax.broadcasted_iota(jnp.int32, (c, c), 1)
        eye = (ri == ci).astype(F32)
        diag_blocks = (ri // GDN_DIAG) == (ci // GDN_DIAG)
        for g in range(g_n):
            av = a_ref[0, g]
            d1 = jnp.where(diag_blocks, av, 0.0)
            d2 = dot(d1, d1)
            d4 = dot(d2, d2)
            d8 = dot(d4, d4)
            t_diag = dot(dot(dot(eye - d1, eye + d2), eye + d4), eye + d8)
            x = jnp.where(ri < GDN_DIAG, t_diag, 0.0)
            for b in range(1, c // GDN_DIAG):
                solved = dot(t_diag, eye - dot(av, x))
                x = jnp.where(ri // GDN_DIAG == b, solved, x)
            t_ref[0, g] = x
            sol_ref[0, g] = dot(x, rhs_ref[0, g])

    a_spec = pl.BlockSpec((1, g_n, c, c), lambda i, j: (i, j, 0, 0))
    r_spec = pl.BlockSpec((1, g_n, c, r), lambda i, j: (i, j, 0, 0))
    return pl.pallas_call(
        body, name="gdn_tri_solve",
        grid=(h, n // g_n),
        in_specs=[a_spec, r_spec],
        out_specs=[r_spec, a_spec],
        out_shape=[jax.ShapeDtypeStruct(rhs.shape, F32), jax.ShapeDtypeStruct(a.shape, F32)],
        compiler_params=_cparams(("parallel", "parallel")),
    )(a, rhs)


def _tri_solve_bwd_call(t_inv, sol, dsol):
    h, n, c, _ = t_inv.shape
    r = sol.shape[-1]
    g_n = GDN_CHUNKS_PER_STEP

    def body(t_ref, sol_ref, dsol_ref, drhs_ref, da_ref):
        for g in range(g_n):
            d_rhs = lax.dot_general(t_ref[0, g].astype(BF16), dsol_ref[0, g].astype(BF16), _TN,
                                    preferred_element_type=F32)
            drhs_ref[0, g] = d_rhs
            da_ref[0, g] = -lax.dot_general(d_rhs.astype(BF16), sol_ref[0, g].astype(BF16), _NT,
                                            preferred_element_type=F32)

    a_spec = pl.BlockSpec((1, g_n, c, c), lambda i, j: (i, j, 0, 0))
    r_spec = pl.BlockSpec((1, g_n, c, r), lambda i, j: (i, j, 0, 0))
    return pl.pallas_call(
        body, name="gdn_tri_solve_bwd",
        grid=(h, n // g_n),
        in_specs=[a_spec, r_spec, r_spec],
        out_specs=[r_spec, a_spec],
        out_shape=[jax.ShapeDtypeStruct(sol.shape, F32), jax.ShapeDtypeStruct(t_inv.shape, F32)],
        compiler_params=_cparams(("parallel", "parallel")),
    )(t_inv, sol, dsol)


@jax.custom_vjp
def _tri_solve(a, rhs):
    return _tri_solve_fwd_call(a, rhs)[0]


def _tri_solve_fwd(a, rhs):
    sol, t_inv = _tri_solve_fwd_call(a, rhs)
    return sol, (t_inv, sol)


def _tri_solve_bwd(res, dsol):
    t_inv, sol = res
    d_rhs, d_a = _tri_solve_bwd_call(t_inv, sol, dsol)
    return d_a, d_rhs


_tri_solve.defvjp(_tri_solve_fwd, _tri_solve_bwd)


@jax.custom_vjp
def _gdn_scan(qd, kd, u, w, attn, g_last):
    return _gdn_scan_fwd(qd, kd, u, w, attn, g_last)[0]


def _gdn_scan_fwd(qd, kd, u, w, attn, g_last):
    egl = jnp.broadcast_to(jnp.exp(g_last)[:, :, None, None], g_last.shape + (1, u.shape[-1]))
    o, states = _gdn_scan_fwd_call(qd, kd, u, w, attn, egl)
    return o, (qd, kd, u, w, attn, egl, states)


def _gdn_scan_bwd(res, do):
    qd, kd, u, w, attn, egl, states = res
    dq, dk, du, dw, da, de = _gdn_scan_bwd_call(qd, kd, u, w, attn, egl, states, do)
    dgl = jnp.sum(de[:, :, 0, :], axis=-1) * egl[:, :, 0, 0]
    return dq, dk, du, dw, da, dgl


_gdn_scan.defvjp(_gdn_scan_fwd, _gdn_scan_bwd)


def _rms_norm(x, w):
    y = x * lax.rsqrt(jnp.mean(x * x, axis=-1, keepdims=True) + NORM_EPS)
    return y * w


def _modulate(h, shift, scale):
    return h * (1 + scale) + shift


def _l2_normalize(x):
    return x * lax.rsqrt(jnp.sum(x * x, axis=-1, keepdims=True) + NORM_EPS)


def _causal_conv(x, w):
    width = w.shape[0]
    length = x.shape[0]
    xp = jnp.pad(x, ((width - 1, 0), (0, 0)))
    return sum(xp[i:i + length] * w[i] for i in range(width))


def _gated_delta_rule_chunked(q, k, v, g, beta):
    h, length, dk = q.shape
    dv = v.shape[-1]
    n = length // GDN_CHUNK
    blk = lambda t: t.reshape(h, n, GDN_CHUNK, *t.shape[2:])
    q = blk(q) * dk ** -0.5
    k = blk(k)
    v = blk(v)
    beta = blk(beta)
    g = jnp.cumsum(blk(g), axis=-1)
    causal = jnp.tril(jnp.ones((GDN_CHUNK, GDN_CHUNK), dtype=bool))
    strict = jnp.tril(jnp.ones((GDN_CHUNK, GDN_CHUNK), dtype=bool), k=-1)
    decay = jnp.exp(jnp.where(causal, g[..., :, None] - g[..., None, :], -jnp.inf))
    k_beta = k * beta[..., None]
    a_strict = jnp.where(strict, jnp.einsum('hnid,hnjd->hnij', k_beta, k) * decay, 0.0)
    rhs = jnp.concatenate([v * beta[..., None], k_beta * jnp.exp(g)[..., None]], axis=-1)
    sol = _tri_solve(a_strict, rhs)
    u, w = sol[..., :dv], sol[..., dv:]
    attn = jnp.where(causal, jnp.einsum('hnid,hnjd->hnij', q, k) * decay, 0.0)
    g_last = g[..., -1]
    q_dec = q * jnp.exp(g)[..., None]
    k_dec = k * jnp.exp(g_last[..., None] - g)[..., None]

    return _gdn_scan(q_dec, k_dec, u, w, attn, g_last).reshape(h, length, dv)


_MM = {tag: _make_mm(tag) for tag in ("gdn_in", "gdn_ba", "gdn_out", "ffn_out0", "kv", "kv_f", "fox_out", "ffn_out1")}
_MM.update({tag: _make_mm(tag, dev_blocks=True) for tag in ("ffn_in0", "ffn_in1", "fox_in")})


def _swiglu(h, w_in, w_out, c_in, c_out, layer):
    gu = _MM[f"ffn_in{layer}"](h, w_in, c_in)
    gate, up = gu[:, :FFN_HIDDEN], gu[:, FFN_HIDDEN:]
    return _MM[f"ffn_out{layer}"](jax.nn.silu(gate) * up, w_out, c_out)


def _gated_deltanet(h, wts, car, conv_w, a_log, dt_bias, norm_w):
    length = h.shape[0]
    main = _MM["gdn_in"](h, wts["gdn_in"], car["gdn_in"])
    ba = _MM["gdn_ba"](h, wts["gdn_ba"], car["gdn_ba"])
    qkv, z = main[:, :CONV_DIM], main[:, CONV_DIM:]
    beta_logit, a = ba[:, :GDN_V_HEADS], ba[:, GDN_V_HEADS:2 * GDN_V_HEADS]
    qkv = jax.nn.silu(_causal_conv(qkv, conv_w))
    q, k, v = qkv[:, :GDN_QK_DIM], qkv[:, GDN_QK_DIM:2 * GDN_QK_DIM], qkv[:, 2 * GDN_QK_DIM:]
    rep = GDN_V_HEADS // GDN_QK_HEADS
    heads = lambda t, nh: t.reshape(length, nh, GDN_HEAD_DIM)
    q = jnp.repeat(_l2_normalize(heads(q, GDN_QK_HEADS)), rep, axis=1)
    k = jnp.repeat(_l2_normalize(heads(k, GDN_QK_HEADS)), rep, axis=1)
    v = heads(v, GDN_V_HEADS)
    beta = jax.nn.sigmoid(beta_logit)
    g = -jnp.exp(a_log) * jax.nn.softplus(a + dt_bias)
    tr = lambda t: jnp.swapaxes(t, 0, 1)
    o = _gated_delta_rule_chunked(tr(q), tr(k), tr(v), tr(g), tr(beta))
    o = _rms_norm(tr(o), norm_w) * jax.nn.silu(heads(z, GDN_V_HEADS))
    return _MM["gdn_out"](o.reshape(length, GDN_V_DIM), wts["gdn_out"], car["gdn_out"])


def _local_loss(x, mods, small, car, wts, target):
    split6 = lambda m: tuple(m[i * D_MODEL:(i + 1) * D_MODEL][None, :] for i in range(6))
    sh_m, sc_m, g_m, sh_f, sc_f, g_f = split6(mods["ada0"])
    h = _modulate(_rms_norm(x, small["norm_mix"][0]), sh_m, sc_m)
    y = _gated_deltanet(h, wts, car, small["gdn_conv"], small["gdn_a_log"][0], small["gdn_dt_bias"][0],
                        small["gdn_norm"][0])
    x = x + g_m * y
    h = _modulate(_rms_norm(x, small["norm_ffn"][0]), sh_f, sc_f)
    x = x + g_f * _swiglu(h, wts["ffn_in0"], wts["ffn_out0"], car["ffn_in0"], car["ffn_out0"], 0)
    length = x.shape[0]
    kv_shift, kv_scale = mods["kvada"][None, :D_MODEL], mods["kvada"][None, D_MODEL:]
    hk = _modulate(_rms_norm(x, small["kv_norm"]), kv_shift, kv_scale)
    kv = _MM["kv"](hk, wts["kv"], car["kv"])
    f_logit = _MM["kv_f"](hk, wts["kv_f"], car["kv_f"])[:, :FOX_HEADS]
    k_sh = _rms_norm(kv[:, :FOX_KV_DIM].reshape(length, FOX_KV_HEADS, FOX_HEAD_DIM), small["k_norm"])
    k_sh = k_sh.reshape(length, FOX_KV_DIM)
    v_sh = kv[:, FOX_KV_DIM:]
    f_cum = jnp.cumsum(jax.nn.log_sigmoid(f_logit + small["forget_b"]), axis=0)
    sh_m, sc_m, g_m, sh_f, sc_f, g_f = split6(mods["ada1"])
    h = _modulate(_rms_norm(x, small["norm_mix"][1]), sh_m, sc_m)
    qg = _MM["fox_in"](h, wts["fox_in"], car["fox_in"])
    q = _rms_norm(qg[:, :FOX_Q_DIM].reshape(length, FOX_HEADS, FOX_HEAD_DIM), small["q_norm"][0])
    o = _fox_attention(q.reshape(length, FOX_Q_DIM), k_sh, v_sh, f_cum)
    y = _MM["fox_out"](o * jax.nn.sigmoid(qg[:, FOX_Q_DIM:]), wts["fox_out"], car["fox_out"])
    x = x + g_m * y
    h = _modulate(_rms_norm(x, small["norm_ffn"][1]), sh_f, sc_f)
    x = x + g_f * _swiglu(h, wts["ffn_in1"], wts["ffn_out1"], car["ffn_in1"], car["ffn_out1"], 1)
    out = _modulate(_rms_norm(x, small["out_norm"]), mods["outada"][None, :D_MODEL], mods["outada"][None, D_MODEL:])
    err = jnp.square(out - target)
    return 0.5 * jnp.sum(jnp.mean(err, axis=-1))


def _pad_rows(a, rows):
    return jnp.pad(a, ((0, rows - a.shape[0]), (0, 0)))


def _pack_flat(parts, total):
    flat = jnp.concatenate([p.reshape(-1) for p in parts])
    return jnp.pad(flat, (0, total - flat.shape[0]))


_SMALL_NAMES = ("ada_b", "norm_mix", "norm_ffn", "gdn_a_log", "gdn_dt_bias", "gdn_norm", "kv_ada_b", "kv_norm",
                "k_norm", "forget_b", "q_norm", "out_ada_b", "out_norm")


def kernel(x, c, ada_w, ada_b, norm_mix, norm_ffn, ffn_w_in, ffn_w_out, gdn_w_in, gdn_conv, gdn_a_log, gdn_dt_bias, gdn_norm, gdn_w_out, kv_ada_w, kv_ada_b, kv_norm, kv_w, k_norm, forget_b, fox_w_in, q_norm, fox_w_out, out_ada_w, out_ada_b, out_norm, loss_target, m_ada_w, m_ada_b, m_norm_mix, m_norm_ffn, m_ffn_w_in, m_ffn_w_out, m_gdn_w_in, m_gdn_conv, m_gdn_a_log, m_gdn_dt_bias, m_gdn_norm, m_gdn_w_out, m_kv_ada_w, m_kv_ada_b, m_kv_norm, m_kv_w, m_k_norm, m_forget_b, m_fox_w_in, m_q_norm, m_fox_w_out, m_out_ada_w, m_out_ada_b, m_out_norm, v_ada_w, v_ada_b, v_norm_mix, v_norm_ffn, v_ffn_w_in, v_ffn_w_out, v_gdn_w_in, v_gdn_conv, v_gdn_a_log, v_gdn_dt_bias, v_gdn_norm, v_gdn_w_out, v_kv_ada_w, v_kv_ada_b, v_kv_norm, v_kv_w, v_k_norm, v_forget_b, v_fox_w_in, v_q_norm, v_fox_w_out, v_out_ada_w, v_out_ada_b, v_out_norm):
    w_in = dict(ada_w=ada_w, ada_b=ada_b, norm_mix=norm_mix, norm_ffn=norm_ffn, ffn_w_in=ffn_w_in, ffn_w_out=ffn_w_out, gdn_w_in=gdn_w_in, gdn_conv=gdn_conv, gdn_a_log=gdn_a_log, gdn_dt_bias=gdn_dt_bias, gdn_norm=gdn_norm, gdn_w_out=gdn_w_out, kv_ada_w=kv_ada_w, kv_ada_b=kv_ada_b, kv_norm=kv_norm, kv_w=kv_w, k_norm=k_norm, forget_b=forget_b, fox_w_in=fox_w_in, q_norm=q_norm, fox_w_out=fox_w_out, out_ada_w=out_ada_w, out_ada_b=out_ada_b, out_norm=out_norm)
    m_in = dict(ada_w=m_ada_w, ada_b=m_ada_b, norm_mix=m_norm_mix, norm_ffn=m_norm_ffn, ffn_w_in=m_ffn_w_in, ffn_w_out=m_ffn_w_out, gdn_w_in=m_gdn_w_in, gdn_conv=m_gdn_conv, gdn_a_log=m_gdn_a_log, gdn_dt_bias=m_gdn_dt_bias, gdn_norm=m_gdn_norm, gdn_w_out=m_gdn_w_out, kv_ada_w=m_kv_ada_w, kv_ada_b=m_kv_ada_b, kv_norm=m_kv_norm, kv_w=m_kv_w, k_norm=m_k_norm, forget_b=m_forget_b, fox_w_in=m_fox_w_in, q_norm=m_q_norm, fox_w_out=m_fox_w_out, out_ada_w=m_out_ada_w, out_ada_b=m_out_ada_b, out_norm=m_out_norm)
    v_in = dict(ada_w=v_ada_w, ada_b=v_ada_b, norm_mix=v_norm_mix, norm_ffn=v_norm_ffn, ffn_w_in=v_ffn_w_in, ffn_w_out=v_ffn_w_out, gdn_w_in=v_gdn_w_in, gdn_conv=v_gdn_conv, gdn_a_log=v_gdn_a_log, gdn_dt_bias=v_gdn_dt_bias, gdn_norm=v_gdn_norm, gdn_w_out=v_gdn_w_out, kv_ada_w=v_kv_ada_w, kv_ada_b=v_kv_ada_b, kv_norm=v_kv_norm, kv_w=v_kv_w, k_norm=v_k_norm, forget_b=v_forget_b, fox_w_in=v_fox_w_in, q_norm=v_q_norm, fox_w_out=v_fox_w_out, out_ada_w=v_out_ada_w, out_ada_b=v_out_ada_b, out_norm=v_out_norm)
    names = list(w_in)
    me = 4 * lax.axis_index("x") + 2 * lax.axis_index("y") + lax.axis_index("c")
    d = D_MODEL

    n_conv = CONV_K * (CONV_DIM // N_DEV)
    first = _pack_flat([c, gdn_conv], d + n_conv).reshape(-1, LANES)
    first_all = _all_gather(first, name="ag_cond_conv", in_vmem=True).reshape(N_DEV, d + n_conv)
    cond_all = jax.nn.silu(first_all[:, :d])
    conv_full = first_all[:, d:].reshape(N_DEV, CONV_K, CONV_DIM // N_DEV)
    conv_full = jnp.transpose(conv_full, (1, 0, 2)).reshape(CONV_K, CONV_DIM)

    ada_cat = jnp.concatenate([ada_w[0], ada_w[1], kv_ada_w, out_ada_w], axis=1)
    n_ada = ada_cat.shape[1]
    cond_pad = _pad_rows(cond_all, LANES)
    mods_part = _matmul(cond_pad, ada_cat, name="mm_ada")[:N_DEV]
    mods_all = _all_gather(mods_part.reshape(-1, LANES), name="ag_mods", in_vmem=True)
    mods_all = mods_all.reshape(N_DEV, N_DEV, n_ada)
    mine = lax.dynamic_index_in_dim(mods_all, me, axis=1, keepdims=False)
    s6 = 6 * d // N_DEV
    s2 = 2 * d // N_DEV
    mods = {
        "ada0": mine[:, :s6].reshape(-1) + ada_b[0],
        "ada1": mine[:, s6:2 * s6].reshape(-1) + ada_b[1],
        "kvada": mine[:, 2 * s6:2 * s6 + s2].reshape(-1) + kv_ada_b,
        "outada": mine[:, 2 * s6 + s2:].reshape(-1) + out_ada_b,
    }

    big = [ffn_w_in[0], ffn_w_in[1], ffn_w_out[0], ffn_w_out[1], gdn_w_in[0], gdn_w_out[0], kv_w, fox_w_in[0],
           fox_w_out[0]]
    g_ffn_in0, g_ffn_in1, g_ffn_out0, g_ffn_out1, g_gdn_in, g_gdn_out, g_kv, g_fox_in, g_fox_out = (
        _all_gather_by_chip([b.astype(BF16) for b in big], name="ag_weights"))
    rows = lambda t: t.reshape(N_DEV * t.shape[1], t.shape[2])
    per_gdn = GDN_PROJ // N_DEV
    last_main = GDN_MAIN - (N_DEV - 1) * per_gdn
    gdn_main = jnp.concatenate([g_gdn_in[dv] for dv in range(N_DEV - 1)] + [g_gdn_in[N_DEV - 1][:, :last_main]],
                               axis=1)
    gdn_ba = jnp.pad(g_gdn_in[N_DEV - 1][:, last_main:], ((0, 0), (0, LANES - 2 * GDN_V_HEADS)))
    kv_full = rows(g_kv)
    wts = {
        "ffn_in0": g_ffn_in0, "ffn_in1": g_ffn_in1, "ffn_out0": rows(g_ffn_out0), "ffn_out1": rows(g_ffn_out1),
        "gdn_in": gdn_main, "gdn_ba": gdn_ba, "gdn_out": rows(g_gdn_out),
        "kv": kv_full[:, :2 * FOX_KV_DIM],
        "kv_f": jnp.pad(kv_full[:, 2 * FOX_KV_DIM:], ((0, 0), (0, LANES - FOX_HEADS))),
        "fox_in": g_fox_in, "fox_out": rows(g_fox_out),
    }
    car = {k: jnp.zeros(w.shape, BF16) for k, w in wts.items()}
    small = {n: w_in[n] for n in _SMALL_NAMES if n not in ("ada_b", "kv_ada_b", "out_ada_b")}
    small["gdn_conv"] = conv_full

    loss_fn = functools.partial(_local_loss, wts=wts, target=loss_target[0])
    loss_local, vjp = jax.vjp(loss_fn, x[0], mods, small, car)
    gx, gmods, gsmall, gcar = vjp(jnp.ones((), F32))
    loss = lax.psum(loss_local, ("x", "y", "c"))

    gsm = dict(gsmall)
    gsm["ada_b"] = jnp.stack([gmods["ada0"], gmods["ada1"]])
    gsm["kv_ada_b"] = gmods["kvada"]
    gsm["out_ada_b"] = gmods["outada"]
    small_sizes = [w_in[n].size for n in _SMALL_NAMES]
    n_small = sum(small_sizes) + CONV_K * CONV_DIM
    small_rows = -(-n_small // (LANES * 8)) * 8
    vec = _pack_flat([gsm[n] for n in _SMALL_NAMES] + [gsm["gdn_conv"]], small_rows * LANES)
    vec_all = _all_gather(vec.reshape(small_rows, LANES), name="ag_small_grads", in_vmem=True)
    vec_sum = _sum_slots(vec_all, name="sum_small_grads").reshape(-1)
    grads = {}
    off = 0
    for n, sz in zip(_SMALL_NAMES, small_sizes):
        grads[n] = vec_sum[off:off + sz].reshape(w_in[n].shape)
        off += sz
    conv_sum = vec_sum[off:off + CONV_K * CONV_DIM].reshape(CONV_K, N_DEV, CONV_DIM // N_DEV)
    grads["gdn_conv"] = lax.dynamic_index_in_dim(conv_sum, me, axis=1, keepdims=False)[None]

    vec_flat = vec_all.reshape(N_DEV, -1)
    d_ada = vec_flat[:, :2 * 6 * d].reshape(N_DEV, 2, N_DEV, s6)
    o_kv = sum(small_sizes[:6])
    d_kvada = vec_flat[:, o_kv:o_kv + 2 * d].reshape(N_DEV, N_DEV, s2)
    o_out = sum(small_sizes[:11])
    d_outada = vec_flat[:, o_out:o_out + 2 * d].reshape(N_DEV, N_DEV, s2)
    pick = lambda t, axis: lax.dynamic_index_in_dim(t, me, axis=axis, keepdims=False)
    d_cat = jnp.concatenate([pick(d_ada[:, 0], 1), pick(d_ada[:, 1], 1), pick(d_kvada, 1), pick(d_outada, 1)],
                            axis=1)
    g_ada_cat = _matmul(cond_pad, _pad_rows(d_cat, LANES), trans_a=True, name="mm_ada_dw")
    grads["ada_w"] = jnp.stack([g_ada_cat[:, :s6], g_ada_cat[:, s6:2 * s6]])
    grads["kv_ada_w"] = g_ada_cat[:, 2 * s6:2 * s6 + s2]
    grads["out_ada_w"] = g_ada_cat[:, 2 * s6 + s2:]

    d_main, d_ba = gcar["gdn_in"], gcar["gdn_ba"]
    d_gdn_in = jnp.stack(
        [d_main[:, dv * per_gdn:(dv + 1) * per_gdn] for dv in range(N_DEV - 1)]
        + [jnp.concatenate([d_main[:, (N_DEV - 1) * per_gdn:], d_ba[:, :2 * GDN_V_HEADS]], axis=1)])
    d_kv = jnp.concatenate([gcar["kv"], gcar["kv_f"][:, :FOX_HEADS]], axis=1)
    owner_rows = lambda t: t.reshape(N_DEV, t.shape[0] // N_DEV, t.shape[1])
    by_owner = [gcar["ffn_in0"], gcar["ffn_in1"], owner_rows(gcar["ffn_out0"]), owner_rows(gcar["ffn_out1"]),
                d_gdn_in, owner_rows(gcar["gdn_out"]), owner_rows(d_kv), gcar["fox_in"],
                owner_rows(gcar["fox_out"])]
    by_chip_core = [t.reshape((N_CHIP, 2) + t.shape[1:]) for t in by_owner]
    my_c = lax.axis_index("c")
    from_sibling = _exchange_sibling(by_chip_core, name="rs_sibling")
    shard_grads = []
    pair_sums = []
    for idx, (t, got) in enumerate(zip(by_chip_core, from_sibling)):
        for_my_core = lax.dynamic_index_in_dim(t, my_c, axis=1, keepdims=False)
        pair_sums.append(_add_pair(for_my_core, got, name=f"rs_pair_sum{idx}"))
    by_chip = _exchange_chips(pair_sums, name="rs_chips")
    for idx, t in enumerate(by_chip):
        shard_grads.append(_sum_slots(t, name=f"sum_grads{idx}"))
    grads["ffn_w_in"] = jnp.stack(shard_grads[0:2])
    grads["ffn_w_out"] = jnp.stack(shard_grads[2:4])
    grads["gdn_w_in"] = shard_grads[4][None]
    grads["gdn_w_out"] = shard_grads[5][None]
    grads["kv_w"] = shard_grads[6]
    grads["fox_w_in"] = shard_grads[7][None]
    grads["fox_w_out"] = shard_grads[8][None]

    delta, new_m, new_v = {}, {}, {}
    for n in names:
        if n in _SMALL_NAMES:
            continue
        delta[n], new_m[n], new_v[n] = _adamw(w_in[n], grads[n], m_in[n], v_in[n], name=f"adamw_{n}")
    sm_rows = -(-sum(small_sizes) // (LANES * 8)) * 8
    pk = lambda src: _pack_flat([src[n] for n in _SMALL_NAMES], sm_rows * LANES).reshape(sm_rows, LANES)
    d_pk, m_pk, vn_pk = _adamw(pk(w_in), pk(grads), pk(m_in), pk(v_in), name="adamw_small")
    off = 0
    for n, sz in zip(_SMALL_NAMES, small_sizes):
        shp = w_in[n].shape
        delta[n] = d_pk.reshape(-1)[off:off + sz].reshape(shp)
        new_m[n] = m_pk.reshape(-1)[off:off + sz].reshape(shp)
        new_v[n] = vn_pk.reshape(-1)[off:off + sz].reshape(shp)
        off += sz

    return (loss, gx[None], *[grads[n] for n in names], *[delta[n] for n in names],
            *[new_m[n] for n in names], *[new_v[n] for n in names])
```

```python
import functools

import jax
import jax.numpy as jnp
from jax import lax
from jax.experimental import pallas as pl
from jax.experimental.pallas import tpu as pltpu

F32 = jnp.float32
BF16 = jnp.bfloat16
MESH = pl.DeviceIdType.MESH
N_DEV = 8

D_MODEL = 2048
SEQ = 8192
GDN_QK_HEADS = 16
GDN_V_HEADS = 32
GDN_HEAD_DIM = 128
GDN_QK_DIM = GDN_QK_HEADS * GDN_HEAD_DIM
GDN_V_DIM = GDN_V_HEADS * GDN_HEAD_DIM
CONV_DIM = 2 * GDN_QK_DIM + GDN_V_DIM
GDN_MAIN = CONV_DIM + GDN_V_DIM
GDN_PROJ = GDN_MAIN + 2 * GDN_V_HEADS
CONV_K = 4
GDN_CHUNK = 64
FOX_HEADS = 16
FOX_KV_HEADS = 2
FOX_GROUP = FOX_HEADS // FOX_KV_HEADS
FOX_HEAD_DIM = 256
FOX_Q_DIM = FOX_HEADS * FOX_HEAD_DIM
FOX_KV_DIM = FOX_KV_HEADS * FOX_HEAD_DIM
KV_PROJ = 2 * FOX_KV_DIM + FOX_HEADS
FFN_HIDDEN = 5632
NORM_EPS = 1e-6

ADAM_LR = 0.001
ADAM_B1 = 0.9
ADAM_B2 = 0.999
ADAM_EPS = 1e-08
ADAM_WD = 0.01
ADAM_STEP = 10

LANES = 128
VMEM_LIMIT = 56 * 1024 * 1024
PACK_COLS = 1024
ATT_BLOCK = 512
NEG = -1e30


def _pick(n, cands):
    for c in cands:
        if n % c == 0:
            return c
    return n


def _cparams(sem):
    return pltpu.CompilerParams(dimension_semantics=sem, vmem_limit_bytes=VMEM_LIMIT)


def _matmul(a, b, *, trans_a=False, trans_b=False, b_dev_blocks=False, out_dev_blocks=False, out_dtype=F32, name):
    if trans_a:
        kdim, m = a.shape
    else:
        m, kdim = a.shape
    if b_dev_blocks:
        n_dev, rows_b, per_dev = b.shape
        n, kb = (rows_b, n_dev * per_dev) if trans_b else (n_dev * per_dev, rows_b)
    elif trans_b:
        n, kb = b.shape
    else:
        kb, n = b.shape
    assert kdim == kb, (a.shape, b.shape)
    tm = _pick(m, (1024, 512, 256, 128))
    tn = _pick(n // N_DEV if (out_dev_blocks or (b_dev_blocks and not trans_b)) else n, (1408, 1024, 512, 256, 128))
    tk = _pick(kdim // N_DEV if (b_dev_blocks and trans_b) else kdim, (1024, 1408, 512, 256, 128))
    nk = kdim // tk
    dims = (((0 if trans_a else 1,), (1 if trans_b else 0,)), ((), ()))

    def body(a_ref, b_ref, o_ref, acc_ref):
        k = pl.program_id(2)

        @pl.when(k == 0)
        def _():
            acc_ref[...] = jnp.zeros_like(acc_ref)

        acc_ref[...] += lax.dot_general(a_ref[...].astype(BF16), b_ref[...].astype(BF16), dims,
                                        preferred_element_type=F32)

        @pl.when(k == nk - 1)
        def _():
            o_ref[...] = acc_ref[...].astype(o_ref.dtype)

    a_spec = (pl.BlockSpec((tk, tm), lambda i, j, k: (k, i)) if trans_a
              else pl.BlockSpec((tm, tk), lambda i, j, k: (i, k)))
    if b_dev_blocks and trans_b:
        per = kdim // N_DEV // tk
        b_spec = pl.BlockSpec((None, tn, tk), lambda i, j, k: (k // per, j, k % per))
    elif b_dev_blocks:
        per = n // N_DEV // tn
        b_spec = pl.BlockSpec((None, tk, tn), lambda i, j, k: (j // per, k, j % per))
    elif trans_b:
        b_spec = pl.BlockSpec((tn, tk), lambda i, j, k: (j, k))
    else:
        b_spec = pl.BlockSpec((tk, tn), lambda i, j, k: (k, j))
    if out_dev_blocks:
        per_o = n // N_DEV // tn
        out_spec = pl.BlockSpec((None, tm, tn), lambda i, j, k: (j // per_o, i, j % per_o))
        out_shape = jax.ShapeDtypeStruct((N_DEV, m, n // N_DEV), out_dtype)
    else:
        out_spec = pl.BlockSpec((tm, tn), lambda i, j, k: (i, j))
        out_shape = jax.ShapeDtypeStruct((m, n), out_dtype)
    return pl.pallas_call(
        body, name=name,
        grid=(m // tm, n // tn, nk),
        in_specs=[a_spec, b_spec],
        out_specs=out_spec,
        out_shape=out_shape,
        scratch_shapes=[pltpu.VMEM((tm, tn), F32)],
        compiler_params=_cparams(("parallel", "parallel", "arbitrary")),
    )(a, b)


def _make_mm(tag, dev_blocks=False):
    @jax.custom_vjp
    def mm(a, w, carrier):
        return _matmul(a, w, b_dev_blocks=dev_blocks, name=f"mm_{tag}")

    def fwd(a, w, carrier):
        return _matmul(a, w, b_dev_blocks=dev_blocks, name=f"mm_{tag}"), (a, w)

    def bwd(res, g):
        a, w = res
        da = _matmul(g, w, trans_b=True, b_dev_blocks=dev_blocks, name=f"mm_{tag}_da")
        dw = _matmul(a, g, trans_a=True, out_dev_blocks=dev_blocks, out_dtype=BF16, name=f"mm_{tag}_dw")
        return da, jnp.zeros_like(w), dw

    mm.defvjp(fwd, bwd)
    return mm


_NT = (((1,), (1,)), ((), ()))
_FOX_SCALE = FOX_HEAD_DIM ** -0.5


def _fox_fwd_call(q, kb, vb, f_col, f_row):
    length = q.shape[0]
    t = ATT_BLOCK
    tq = 2 * t
    nq = length // tq
    hd = FOX_HEAD_DIM

    def body(q_ref, k_ref, v_ref, fq_ref, fk_ref, o_ref, lse_ref, qs_sc, m_sc, l_sc, acc_sc):
        i = pl.program_id(1)
        qs_sc[...] = (q_ref[...] * _FOX_SCALE).astype(BF16)
        m_sc[...] = jnp.full_like(m_sc, NEG)
        l_sc[...] = jnp.zeros_like(l_sc)
        acc_sc[...] = jnp.zeros_like(acc_sc)

        def block(j, modes):
            off = pl.multiple_of(j * t, t)
            kj = k_ref[pl.ds(off, t), :]
            vj = v_ref[pl.ds(off, t), :]
            fk = fk_ref[0, :, pl.ds(off, t)]
            for a in (0, 1):
                if modes[a] is None:
                    continue
                rows = pl.ds(a * t, t)
                s = lax.dot_general(qs_sc[rows, :], kj, _NT, preferred_element_type=F32)
                s = s + fq_ref[0, rows, :] - fk
                if modes[a]:
                    r = lax.broadcasted_iota(jnp.int32, (t, t), 0)
                    c = lax.broadcasted_iota(jnp.int32, (t, t), 1)
                    s = jnp.where(r >= c, s, NEG)
                m_old = m_sc[a]
                m_new = jnp.maximum(m_old, jnp.max(s, axis=1, keepdims=True))
                alpha = jnp.exp(m_old - m_new)
                p = jnp.exp(s - m_new)
                l_sc[a] = alpha * l_sc[a] + jnp.sum(p, axis=1, keepdims=True)
                acc_sc[a] = alpha * acc_sc[a] + jnp.dot(p.astype(BF16), vj, preferred_element_type=F32)
                m_sc[a] = m_new

        def loop_body(j, carry):
            block(j, (False, False))
            return carry

        lax.fori_loop(0, 2 * i, loop_body, 0)
        block(2 * i, (True, False))
        block(2 * i + 1, (None, True))
        for a in (0, 1):
            rows = pl.ds(a * t, t)
            o_ref[rows, :] = acc_sc[a] / l_sc[a]
            lse_ref[0, rows, :] = m_sc[a] + jnp.log(l_sc[a])

    return pl.pallas_call(
        body, name="fox_fwd",
        grid=(FOX_HEADS, nq),
        in_specs=[
            pl.BlockSpec((tq, hd), lambda h, i: (i, h)),
            pl.BlockSpec((length, hd), lambda h, i: (0, h // FOX_GROUP)),
            pl.BlockSpec((length, hd), lambda h, i: (0, h // FOX_GROUP)),
            pl.BlockSpec((1, tq, 1), lambda h, i: (h, i, 0)),
            pl.BlockSpec((1, 1, length), lambda h, i: (h, 0, 0)),
        ],
        out_specs=[
            pl.BlockSpec((tq, hd), lambda h, i: (i, h)),
            pl.BlockSpec((1, tq, 1), lambda h, i: (h, i, 0)),
        ],
        out_shape=[jax.ShapeDtypeStruct((length, FOX_Q_DIM), F32),
                   jax.ShapeDtypeStruct((FOX_HEADS, length, 1), F32)],
        scratch_shapes=[pltpu.VMEM((tq, hd), BF16), pltpu.VMEM((2, t, 1), F32), pltpu.VMEM((2, t, 1), F32),
                        pltpu.VMEM((2, t, hd), F32)],
        compiler_params=_cparams(("parallel", "arbitrary")),
    )(q, kb, vb, f_col, f_row)


def _fox_dq_call(q, kb, vb, f_col, f_row, lse_col, do):
    length = q.shape[0]
    t = ATT_BLOCK
    tq = 2 * t
    nq = length // tq
    hd = FOX_HEAD_DIM

    def body(q_ref, k_ref, v_ref, fq_ref, fk_ref, lse_ref, do_ref, dq_ref, dl_ref, qs_sc, do_sc, acc_sc, dl_sc):
        i = pl.program_id(1)
        qs_sc[...] = (q_ref[...] * _FOX_SCALE).astype(BF16)
        do_sc[...] = do_ref[...].astype(BF16)
        acc_sc[...] = jnp.zeros_like(acc_sc)
        dl_sc[...] = jnp.zeros_like(dl_sc)

        def block(j, modes, second_pass):
            off = pl.multiple_of(j * t, t)
            kj = k_ref[pl.ds(off, t), :]
            vj = v_ref[pl.ds(off, t), :]
            fk = fk_ref[0, :, pl.ds(off, t)]
            for a in (0, 1):
                if modes[a] is None:
                    continue
                rows = pl.ds(a * t, t)
                s = lax.dot_general(qs_sc[rows, :], kj, _NT, preferred_element_type=F32)
                s = s + fq_ref[0, rows, :] - fk
                if modes[a]:
                    r = lax.broadcasted_iota(jnp.int32, (t, t), 0)
                    c = lax.broadcasted_iota(jnp.int32, (t, t), 1)
                    s = jnp.where(r >= c, s, NEG)
                p = jnp.exp(s - lse_ref[0, rows, :])
                dp = lax.dot_general(do_sc[rows, :], vj, _NT, preferred_element_type=F32)
                if second_pass:
                    ds = p * (dp - dl_sc[a])
                    acc_sc[a] += jnp.dot(ds.astype(BF16), kj, preferred_element_type=F32)
                else:
                    dl_sc[a] += jnp.sum(p * dp, axis=1, keepdims=True)

        def sweep(second_pass):
            def loop_body(j, carry):
                block(j, (False, False), second_pass)
                return carry

            lax.fori_loop(0, 2 * i, loop_body, 0)
            block(2 * i, (True, False), second_pass)
            block(2 * i + 1, (None, True), second_pass)

        sweep(False)
        sweep(True)
        for a in (0, 1):
            rows = pl.ds(a * t, t)
            dq_ref[rows, :] = acc_sc[a] * _FOX_SCALE
            dl_ref[0, rows, :] = dl_sc[a]

    col = pl.BlockSpec((1, tq, 1), lambda h, i: (h, i, 0))
    return pl.pallas_call(
        body, name="fox_dq",
        grid=(FOX_HEADS, nq),
        in_specs=[
            pl.BlockSpec((tq, hd), lambda h, i: (i, h)),
            pl.BlockSpec((length, hd), lambda h, i: (0, h // FOX_GROUP)),
            pl.BlockSpec((length, hd), lambda h, i: (0, h // FOX_GROUP)),
            col,
            pl.BlockSpec((1, 1, length), lambda h, i: (h, 0, 0)),
            col,
            pl.BlockSpec((tq, hd), lambda h, i: (i, h)),
        ],
        out_specs=[pl.BlockSpec((tq, hd), lambda h, i: (i, h)), col],
        out_shape=[jax.ShapeDtypeStruct((length, FOX_Q_DIM), F32),
                   jax.ShapeDtypeStruct((FOX_HEADS, length, 1), F32)],
        scratch_shapes=[pltpu.VMEM((tq, hd), BF16), pltpu.VMEM((tq, hd), BF16), pltpu.VMEM((2, t, hd), F32),
                        pltpu.VMEM((2, t, 1), F32)],
        compiler_params=_cparams(("parallel", "arbitrary")),
    )(q, kb, vb, f_col, f_row, lse_col, do)


def _fox_dkv_call(qs, dob, kb, vb, f_col, f_row, lse_row, dl_row):
    length = qs.shape[0]
    t = ATT_BLOCK
    tk = 2 * t
    nq = length // t
    hd = FOX_HEAD_DIM

    def body(k_ref, v_ref, q_ref, do_ref, fk_ref, fq_ref, lse_ref, dl_ref,
             dk_ref, dv_ref, df_ref, dk_sc, dv_sc, df_sc):
        j = pl.program_id(1)
        dk_sc[...] = jnp.zeros_like(dk_sc)
        dv_sc[...] = jnp.zeros_like(dv_sc)
        df_sc[...] = jnp.zeros_like(df_sc)

        def block(i, modes):
            off = pl.multiple_of(i * t, t)
            qi = q_ref[pl.ds(off, t), :]
            doi = do_ref[pl.ds(off, t), :]
            fq = fq_ref[0, :, pl.ds(off, t)]
            lse = lse_ref[0, :, pl.ds(off, t)]
            dl = dl_ref[0, :, pl.ds(off, t)]
            for b in (0, 1):
                if modes[b] is None:
                    continue
                keys = pl.ds(b * t, t)
                st = lax.dot_general(k_ref[keys, :], qi, _NT, preferred_element_type=F32)
                st = st + fq - fk_ref[0, keys, :]
                if modes[b]:
                    r = lax.broadcasted_iota(jnp.int32, (t, t), 0)
                    c = lax.broadcasted_iota(jnp.int32, (t, t), 1)
                    st = jnp.where(c >= r, st, NEG)
                pt = jnp.exp(st - lse)
                dv_sc[b] += jnp.dot(pt.astype(BF16), doi, preferred_element_type=F32)
                dpt = lax.dot_general(v_ref[keys, :], doi, _NT, preferred_element_type=F32)
                dst = pt * (dpt - dl)
                dk_sc[b] += jnp.dot(dst.astype(BF16), qi, preferred_element_type=F32)
                df_sc[b] += jnp.sum(dst, axis=1, keepdims=True)

        def loop_body(i, carry):
            block(i, (False, False))
            return carry

        block(2 * j, (True, None))
        block(2 * j + 1, (False, True))
        lax.fori_loop(2 * j + 2, nq, loop_body, 0)
        for b in (0, 1):
            keys = pl.ds(b * t, t)
            dk_ref[keys, :] = dk_sc[b]
            dv_ref[keys, :] = dv_sc[b]
            df_ref[0, keys, :] = -df_sc[b]

    row = pl.BlockSpec((1, 1, length), lambda h, j: (h, 0, 0))
    return pl.pallas_call(
        body, name="fox_dkv",
        grid=(FOX_HEADS, length // tk),
        in_specs=[
            pl.BlockSpec((tk, hd), lambda h, j: (j, h // FOX_GROUP)),
            pl.BlockSpec((tk, hd), lambda h, j: (j, h // FOX_GROUP)),
            pl.BlockSpec((length, hd), lambda h, j: (0, h)),
            pl.BlockSpec((length, hd), lambda h, j: (0, h)),
            pl.BlockSpec((1, tk, 1), lambda h, j: (h, j, 0)),
            row, row, row,
        ],
        out_specs=[
            pl.BlockSpec((tk, hd), lambda h, j: (j, h)),
            pl.BlockSpec((tk, hd), lambda h, j: (j, h)),
            pl.BlockSpec((1, tk, 1), lambda h, j: (h, j, 0)),
        ],
        out_shape=[jax.ShapeDtypeStruct((length, FOX_Q_DIM), F32),
                   jax.ShapeDtypeStruct((length, FOX_Q_DIM), F32),
                   jax.ShapeDtypeStruct((FOX_HEADS, length, 1), F32)],
        scratch_shapes=[pltpu.VMEM((2, t, hd), F32), pltpu.VMEM((2, t, hd), F32), pltpu.VMEM((2, t, 1), F32)],
        compiler_params=_cparams(("parallel", "arbitrary")),
    )(kb, vb, qs, dob, f_col, f_row, lse_row, dl_row)


@jax.custom_vjp
def _fox_attention(q, k, v, f):
    return _fox_attention_fwd(q, k, v, f)[0]


def _fox_attention_fwd(q, k, v, f):
    kb = k.astype(BF16)
    vb = v.astype(BF16)
    f_row = jnp.transpose(f)[:, None, :]
    f_col = jnp.transpose(f)[:, :, None]
    o, lse = _fox_fwd_call(q, kb, vb, f_col, f_row)
    return o, (q, kb, vb, f_col, f_row, lse)


def _fox_attention_bwd(res, do):
    q, kb, vb, f_col, f_row, lse = res
    length = q.shape[0]
    lse_row = jnp.transpose(lse, (0, 2, 1))
    dq, dl_col = _fox_dq_call(q, kb, vb, f_col, f_row, lse, do)
    dl_row = jnp.transpose(dl_col, (0, 2, 1))
    qs =(q * _FOX_SCALE).astype(BF16)
    dkh, dvh, dfk = _fox_dkv_call(qs, do.astype(BF16), kb, vb, f_col, f_row, lse_row, dl_row)
    dk = jnp.sum(dkh.reshape(length, FOX_KV_HEADS, FOX_GROUP, FOX_HEAD_DIM), axis=2).reshape(length, FOX_KV_DIM)
    dv = jnp.sum(dvh.reshape(length, FOX_KV_HEADS, FOX_GROUP, FOX_HEAD_DIM), axis=2).reshape(length, FOX_KV_DIM)
    df = jnp.transpose(dfk[:, :, 0])
    return dq, dk, dv, df


_fox_attention.defvjp(_fox_attention_fwd, _fox_attention_bwd)


def _peer(mx, my, mc, k):
    return (1 - mx if k & 4 else mx, 1 - my if k & 2 else my, 1 - mc if k & 1 else mc)


def _all_gather(x, *, name, in_vmem):
    space = pltpu.VMEM if in_vmem else pl.ANY

    def body(x_ref, out_ref, send_sems, recv_sems, local_sem):
        mx, my, mc = lax.axis_index("x"), lax.axis_index("y"), lax.axis_index("c")
        me = 4 * mx + 2 * my + mc
        mine = pltpu.make_async_copy(x_ref, out_ref.at[me], local_sem)
        mine.start()
        copies = []
        for k in range(1, N_DEV):
            cp = pltpu.make_async_remote_copy(
                src_ref=x_ref, dst_ref=out_ref.at[me],
                send_sem=send_sems.at[k - 1], recv_sem=recv_sems.at[k - 1],
                device_id=_peer(mx, my, mc, k), device_id_type=MESH)
            cp.start()
            copies.append(cp)
        for cp in copies:
            cp.wait()
        mine.wait()

    return pl.pallas_call(
        body, name=name,
        out_shape=jax.ShapeDtypeStruct((N_DEV,) + x.shape, x.dtype),
        in_specs=[pl.BlockSpec(memory_space=space)],
        out_specs=pl.BlockSpec(memory_space=space),
        scratch_shapes=[pltpu.SemaphoreType.DMA((N_DEV - 1,)), pltpu.SemaphoreType.DMA((N_DEV - 1,)),
                        pltpu.SemaphoreType.DMA],
    )(x)


N_CHIP = N_DEV // 2
_HBM = pl.BlockSpec(memory_space=pl.ANY)


def _all_gather_by_chip(xs, *, name):
    n_arr = len(xs)
    per = N_DEV - 1

    def body(*refs):
        x_refs, out_refs = refs[:n_arr], refs[n_arr:2 * n_arr]
        send_sems, recv_sems, local_sems = refs[2 * n_arr:]
        mx, my, mc = lax.axis_index("x"), lax.axis_index("y"), lax.axis_index("c")
        me, sibling = (mx, my, mc), (mx, my, 1 - mc)
        chips = [(1 - mx, my), (mx, 1 - my), (1 - mx, 1 - my)]

        def copy(a, k, block, to, from_input=False):
            px, py, pc = block
            slot = out_refs[a].at[4 * px + 2 * py + pc]
            return pltpu.make_async_remote_copy(
                src_ref=x_refs[a] if from_input else slot, dst_ref=slot,
                send_sem=send_sems.at[a * per + k], recv_sem=recv_sems.at[a * per + k],
                device_id=to, device_id_type=MESH)

        mine = [pltpu.make_async_copy(x_refs[a], out_refs[a].at[4 * mx + 2 * my + mc], local_sems.at[a])
                for a in range(n_arr)]
        for cp in mine:
            cp.start()
        sent = []
        for a in range(n_arr):
            sent.append(copy(a, 0, me, sibling, from_input=True))
            sent += [copy(a, 1 + j, me, (*chip, mc), from_input=True) for j, chip in enumerate(chips)]
        for cp in sent:
            cp.start()
        for a in range(n_arr):
            for j, chip in enumerate(chips):
                copy(a, 1 + j, (*chip, mc), me).wait_recv()
                passed = copy(a, 4 + j, (*chip, mc), sibling)
                passed.start()
                sent.append(passed)
        for a in range(n_arr):
            copy(a, 0, sibling, me).wait_recv()
            for j, chip in enumerate(chips):
                copy(a, 4 + j, (*chip, 1 - mc), me).wait_recv()
        for cp in sent:
            cp.wait_send()
        for cp in mine:
            cp.wait()

    return pl.pallas_call(
        body, name=name,
        out_shape=[jax.ShapeDtypeStruct((N_DEV,) + x.shape, x.dtype) for x in xs],
        in_specs=[_HBM] * n_arr,
        out_specs=[_HBM] * n_arr,
        scratch_shapes=[pltpu.SemaphoreType.DMA((n_arr * per,)), pltpu.SemaphoreType.DMA((n_arr * per,)),
                        pltpu.SemaphoreType.DMA((n_arr,))],
    )(*xs)


def _exchange_sibling(xs, *, name):
    n_arr = len(xs)

    def body(*refs):
        x_refs, out_refs = refs[:n_arr], refs[n_arr:2 * n_arr]
        send_sems, recv_sems = refs[2 * n_arr:]
        mc = lax.axis_index("c")
        sibling = (lax.axis_index("x"), lax.axis_index("y"), 1 - mc)
        copies = [pltpu.make_async_remote_copy(
            src_ref=x_refs[a].at[q, 1 - mc], dst_ref=out_refs[a].at[q],
            send_sem=send_sems.at[a * N_CHIP + q], recv_sem=recv_sems.at[a * N_CHIP + q],
            device_id=sibling, device_id_type=MESH) for a in range(n_arr) for q in range(N_CHIP)]
        for cp in copies:
            cp.start()
        for cp in copies:
            cp.wait()

    return pl.pallas_call(
        body, name=name,
        out_shape=[jax.ShapeDtypeStruct((N_CHIP,) + x.shape[2:], x.dtype) for x in xs],
        in_specs=[_HBM] * n_arr,
        out_specs=[_HBM] * n_arr,
        scratch_shapes=[pltpu.SemaphoreType.DMA((n_arr * N_CHIP,)), pltpu.SemaphoreType.DMA((n_arr * N_CHIP,))],
    )(*xs)


def _exchange_chips(xs, *, name):
    n_arr = len(xs)
    per = N_CHIP - 1

    def body(*refs):
        x_refs, out_refs = refs[:n_arr], refs[n_arr:2 * n_arr]
        send_sems, recv_sems, local_sems = refs[2 * n_arr:]
        mx, my, mc = lax.axis_index("x"), lax.axis_index("y"), lax.axis_index("c")
        my_chip = 2 * mx + my
        copies = []
        for a in range(n_arr):
            mine = pltpu.make_async_copy(x_refs[a].at[my_chip], out_refs[a].at[my_chip], local_sems.at[a])
            mine.start()
            copies.append(mine)
            for j, (px, py) in enumerate([(1 - mx, my), (mx, 1 - my), (1 - mx, 1 - my)]):
                cp = pltpu.make_async_remote_copy(
                    src_ref=x_refs[a].at[2 * px + py], dst_ref=out_refs[a].at[my_chip],
                    send_sem=send_sems.at[a * per + j], recv_sem=recv_sems.at[a * per + j],
                    device_id=(px, py, mc), device_id_type=MESH)
                cp.start()
                copies.append(cp)
        for cp in copies:
            cp.wait()

    return pl.pallas_call(
        body, name=name,
        out_shape=[jax.ShapeDtypeStruct(x.shape, x.dtype) for x in xs],
        in_specs=[_HBM] * n_arr,
        out_specs=[_HBM] * n_arr,
        scratch_shapes=[pltpu.SemaphoreType.DMA((n_arr * per,)), pltpu.SemaphoreType.DMA((n_arr * per,)),
                        pltpu.SemaphoreType.DMA((n_arr,))],
    )(*xs)


def _add_pair(a, b, *, name):
    n, r, c = a.shape
    tr = _pick(r, (512, 256, 128, 64, 32, 16))

    def body(a_ref, b_ref, o_ref):
        o_ref[...] = (a_ref[...].astype(F32) + b_ref[...].astype(F32)).astype(o_ref.dtype)

    spec = pl.BlockSpec((1, tr, c), lambda q, i: (q, i, 0))
    return pl.pallas_call(
        body, name=name,
        grid=(n, r // tr),
        in_specs=[spec, spec],
        out_specs=spec,
        out_shape=jax.ShapeDtypeStruct(a.shape, a.dtype),
        compiler_params=_cparams(("parallel", "parallel")),
    )(a, b)


def _sum_slots(x, *, name):
    n, r, c = x.shape
    tr = _pick(r, (256, 128, 64, 32, 16, 8))

    def body(x_ref, o_ref):
        acc = x_ref[0].astype(F32)
        for s in range(1, n):
            acc = acc + x_ref[s].astype(F32)
        o_ref[...] = acc

    return pl.pallas_call(
        body, name=name,
        grid=(r // tr,),
        in_specs=[pl.BlockSpec((n, tr, c), lambda i: (0, i, 0))],
        out_specs=pl.BlockSpec((tr, c), lambda i: (i, 0)),
        out_shape=jax.ShapeDtypeStruct((r, c), F32),
        compiler_params=_cparams(("parallel",)),
    )(x)


def _adamw(w, g, m, v, *, name):
    shape = w.shape
    c = shape[-1]
    r = w.size // c
    w2, g2, m2, v2 = (t.reshape(r, c) for t in (w, g, m, v))
    tr = _pick(r, (128, 64, 32, 16, 8))

    def body(w_ref, g_ref, m_ref, v_ref, d_ref, mo_ref, vo_ref):
        gv = g_ref[...]
        mn = ADAM_B1 * m_ref[...] + (1.0 - ADAM_B1) * gv
        vn = ADAM_B2 * v_ref[...] + (1.0 - ADAM_B2) * (gv * gv)
        m_hat = mn / (1.0 - ADAM_B1 ** ADAM_STEP)
        v_hat = vn / (1.0 - ADAM_B2 ** ADAM_STEP)
        d_ref[...] = -ADAM_LR * (m_hat / (jnp.sqrt(v_hat) + ADAM_EPS) + ADAM_WD * w_ref[...])
        mo_ref[...] = mn
        vo_ref[...] = vn

    spec = pl.BlockSpec((tr, c), lambda i: (i, 0))
    outs = pl.pallas_call(
        body, name=name,
        grid=(r // tr,),
        in_specs=[spec] * 4,
        out_specs=[spec] * 3,
        out_shape=[jax.ShapeDtypeStruct((r, c), F32)] * 3,
        compiler_params=_cparams(("parallel",)),
    )(w2, g2, m2, v2)
    return tuple(o.reshape(shape) for o in outs)


GDN_HEADS_PER_STEP = 8
_TN = (((0,), (0,)), ((), ()))


def _gdn_scan_specs(qd, u, attn, reverse):
    h, n, c, dk = qd.shape
    dv = u.shape[-1]
    g = GDN_HEADS_PER_STEP
    idx = (lambda hb, i: (hb, n - 1 - i, 0, 0)) if reverse else (lambda hb, i: (hb, i, 0, 0))
    return dict(
        qk=pl.BlockSpec((g, 1, c, dk), idx), uv=pl.BlockSpec((g, 1, c, dv), idx),
        attn=pl.BlockSpec((g, 1, c, c), idx), lane=pl.BlockSpec((g, 1, 1, dv), idx),
        state=pl.BlockSpec((g, 1, dk, dv), idx), grid=(h // g, n))


def _gdn_scan_fwd_call(qd, kd, u, w, attn, egl):
    h, n, c, dk = qd.shape
    dv = u.shape[-1]
    sp = _gdn_scan_specs(qd, u, attn, reverse=False)

    def body(q_ref, k_ref, u_ref, w_ref, a_ref, e_ref, o_ref, st_ref, s_sc):
        @pl.when(pl.program_id(1) == 0)
        def _():
            s_sc[...] = jnp.zeros_like(s_sc)

        for g in range(GDN_HEADS_PER_STEP):
            s = s_sc[g]
            st_ref[g, 0] = s
            sb = s.astype(BF16)
            v_new = u_ref[g, 0] - jnp.dot(w_ref[g, 0].astype(BF16), sb, preferred_element_type=F32)
            vb = v_new.astype(BF16)
            o_ref[g, 0] = (jnp.dot(q_ref[g, 0].astype(BF16), sb, preferred_element_type=F32)
                           + jnp.dot(a_ref[g, 0].astype(BF16), vb, preferred_element_type=F32))
            s_sc[g] = s * e_ref[g, 0] + lax.dot_general(k_ref[g, 0].astype(BF16), vb, _TN,
                                                        preferred_element_type=F32)

    return pl.pallas_call(
        body, name="gdn_scan_fwd",
        grid=sp["grid"],
        in_specs=[sp["qk"], sp["qk"], sp["uv"], sp["qk"], sp["attn"], sp["lane"]],
        out_specs=[sp["uv"], sp["state"]],
        out_shape=[jax.ShapeDtypeStruct((h, n, c, dv), F32), jax.ShapeDtypeStruct((h, n, dk, dv), F32)],
        scratch_shapes=[pltpu.VMEM((GDN_HEADS_PER_STEP, dk, dv), F32)],
        compiler_params=_cparams(("parallel", "arbitrary")),
    )(qd, kd, u, w, attn, egl)


def _gdn_scan_bwd_call(qd, kd, u, w, attn, egl, states, do):
    h, n, c, dk = qd.shape
    dv = u.shape[-1]
    sp = _gdn_scan_specs(qd, u, attn, reverse=True)

    def body(q_ref, k_ref, u_ref, w_ref, a_ref, e_ref, st_ref, do_ref,
             dq_ref, dk_ref, du_ref, dw_ref, da_ref, de_ref, ds_sc):
        @pl.when(pl.program_id(1) == 0)
        def _():
            ds_sc[...] = jnp.zeros_like(ds_sc)

        for g in range(GDN_HEADS_PER_STEP):
            s = st_ref[g, 0]
            sb = s.astype(BF16)
            ds = ds_sc[g]
            dsb = ds.astype(BF16)
            qb, kb, wb, ab = (r[g, 0].astype(BF16) for r in (q_ref, k_ref, w_ref, a_ref))
            dob = do_ref[g, 0].astype(BF16)
            vb = (u_ref[g, 0] - jnp.dot(wb, sb, preferred_element_type=F32)).astype(BF16)
            dv_new = (lax.dot_general(ab, dob, _TN, preferred_element_type=F32)
                      + jnp.dot(kb, dsb, preferred_element_type=F32))
            dvb = dv_new.astype(BF16)
            dq_ref[g, 0] = lax.dot_general(dob, sb, _NT, preferred_element_type=F32)
            da_ref[g, 0] = lax.dot_general(dob, vb, _NT, preferred_element_type=F32)
            dk_ref[g, 0] = lax.dot_general(vb, dsb, _NT, preferred_element_type=F32)
            du_ref[g, 0] = dv_new
            dw_ref[g, 0] = -lax.dot_general(dvb, sb, _NT, preferred_element_type=F32)
            de_ref[g, 0] = jnp.sum(ds * s, axis=0, keepdims=True)
            ds_sc[g] = (ds * e_ref[g, 0] + lax.dot_general(qb, dob, _TN, preferred_element_type=F32)
                        - lax.dot_general(wb, dvb, _TN, preferred_element_type=F32))

    return pl.pallas_call(
        body, name="gdn_scan_bwd",
        grid=sp["grid"],
        in_specs=[sp["qk"], sp["qk"], sp["uv"], sp["qk"], sp["attn"], sp["lane"], sp["state"], sp["uv"]],
        out_specs=[sp["qk"], sp["qk"], sp["uv"], sp["qk"], sp["attn"], sp["lane"]],
        out_shape=[jax.ShapeDtypeStruct((h, n, c, dk), F32), jax.ShapeDtypeStruct((h, n, c, dk), F32),
                   jax.ShapeDtypeStruct((h, n, c, dv), F32), jax.ShapeDtypeStruct((h, n, c, dk), F32),
                   jax.ShapeDtypeStruct((h, n, c, c), F32), jax.ShapeDtypeStruct((h, n, 1, dv), F32)],
        scratch_shapes=[pltpu.VMEM((GDN_HEADS_PER_STEP, dk, dv), F32)],
        compiler_params=_cparams(("parallel", "arbitrary")),
    )(qd, kd, u, w, attn, egl, states, do)


GDN_CHUNKS_PER_STEP = 16
GDN_DIAG = 16


def _tri_solve_fwd_call(a, rhs):
    h, n, c, _ = a.shape
    r = rhs.shape[-1]
    g_n = GDN_CHUNKS_PER_STEP

    batched = (((2,), (1,)), ((0,), (0,)))

    def dot(x, y):
        return lax.dot_general(x.astype(BF16), y.astype(BF16), batched, preferred_element_type=F32)

    def body(a_ref, rhs_ref, sol_ref, t_ref):
        ri = lax.broadcasted_iota(jnp.int32, (g_n, c, c), 1)
        ci = lax.broadcasted_iota(jnp.int32, (g_n, c, c), 2)
        eye = (ri == ci).astype(F32)
        av = a_ref[0]
        d1 = jnp.where(ri // GDN_DIAG == ci // GDN_DIAG, av, 0.0)
        d2 = dot(d1, d1)
        d4 = dot(d2, d2)
        d8 = dot(d4, d4)
        t_inv = dot(dot(eye - d1, eye + d2), dot(eye + d4, eye + d8))
        size = GDN_DIAG
        while size < c:
            below = jnp.where((ri // (2 * size) == ci // (2 * size)) & (ri // size != ci // size), av, 0.0)
            t_inv = t_inv - dot(t_inv, dot(below, t_inv))
            size *= 2
        t_ref[0] = t_inv
        sol_ref[0] = dot(t_inv, rhs_ref[0])

    a_spec = pl.BlockSpec((1, g_n, c, c), lambda i, j: (i, j, 0, 0))
    r_spec = pl.BlockSpec((1, g_n, c, r), lambda i, j: (i, j, 0, 0))
    return pl.pallas_call(
        body, name="gdn_tri_solve",
        grid=(h, n // g_n),
        in_specs=[a_spec, r_spec],
        out_specs=[r_spec, a_spec],
        out_shape=[jax.ShapeDtypeStruct(rhs.shape, F32), jax.ShapeDtypeStruct(a.shape, F32)],
        compiler_params=_cparams(("parallel", "parallel")),
    )(a, rhs)


def _tri_solve_bwd_call(t_inv, sol, dsol):
    h, n, c, _ = t_inv.shape
    r = sol.shape[-1]
    g_n = GDN_CHUNKS_PER_STEP

    def body(t_ref, sol_ref, dsol_ref, drhs_ref, da_ref):
        for g in range(g_n):
            d_rhs = lax.dot_general(t_ref[0, g].astype(BF16), dsol_ref[0, g].astype(BF16), _TN,
                                    preferred_element_type=F32)
            drhs_ref[0, g] = d_rhs
            da_ref[0, g] = -lax.dot_general(d_rhs.astype(BF16), sol_ref[0, g].astype(BF16), _NT,
                                            preferred_element_type=F32)

    a_spec = pl.BlockSpec((1, g_n, c, c), lambda i, j: (i, j, 0, 0))
    r_spec = pl.BlockSpec((1, g_n, c, r), lambda i, j: (i, j, 0, 0))
    return pl.pallas_call(
        body, name="gdn_tri_solve_bwd",
        grid=(h, n // g_n),
        in_specs=[a_spec, r_spec, r_spec],
        out_specs=[r_spec, a_spec],
        out_shape=[jax.ShapeDtypeStruct(sol.shape, F32), jax.ShapeDtypeStruct(t_inv.shape, F32)],
        compiler_params=_cparams(("parallel", "parallel")),
    )(t_inv, sol, dsol)


@jax.custom_vjp
def _tri_solve(a, rhs):
    return _tri_solve_fwd_call(a, rhs)[0]


def _tri_solve_fwd(a, rhs):
    sol, t_inv = _tri_solve_fwd_call(a, rhs)
    return sol, (t_inv, sol)


def _tri_solve_bwd(res, dsol):
    t_inv, sol = res
    d_rhs, d_a = _tri_solve_bwd_call(t_inv, sol, dsol)
    return d_a, d_rhs


_tri_solve.defvjp(_tri_solve_fwd, _tri_solve_bwd)


@jax.custom_vjp
def _gdn_scan(qd, kd, u, w, attn, g_last):
    return _gdn_scan_fwd(qd, kd, u, w, attn, g_last)[0]


def _gdn_scan_fwd(qd, kd, u, w, attn, g_last):
    egl = jnp.broadcast_to(jnp.exp(g_last)[:, :, None, None], g_last.shape + (1, u.shape[-1]))
    o, states = _gdn_scan_fwd_call(qd, kd, u, w, attn, egl)
    return o, (qd, kd, u, w, attn, egl, states)


def _gdn_scan_bwd(res, do):
    qd, kd, u, w, attn, egl, states = res
    dq, dk, du, dw, da, de = _gdn_scan_bwd_call(qd, kd, u, w, attn, egl, states, do)
    dgl = jnp.sum(de[:, :, 0, :], axis=-1) * egl[:, :, 0, 0]
    return dq, dk, du, dw, da, dgl


_gdn_scan.defvjp(_gdn_scan_fwd, _gdn_scan_bwd)


@functools.partial(jax.custom_vjp, nondiff_argnums=(1,))
def _split_cols(x, sizes):
    bounds = [sum(sizes[:i]) for i in range(len(sizes) + 1)]
    return tuple(x[:, bounds[i]:bounds[i + 1]] for i in range(len(sizes)))


def _split_cols_fwd(x, sizes):
    return _split_cols(x, sizes), None


def _split_cols_bwd(sizes, _, parts):
    return (jnp.concatenate(parts, axis=1),)


_split_cols.defvjp(_split_cols_fwd, _split_cols_bwd)


def _rms_norm(x, w):
    y = x * lax.rsqrt(jnp.mean(x * x, axis=-1, keepdims=True) + NORM_EPS)
    return y * w


def _modulate(h, shift, scale):
    return h * (1 + scale) + shift


def _l2_normalize(x):
    return x * lax.rsqrt(jnp.sum(x * x, axis=-1, keepdims=True) + NORM_EPS)


def _causal_conv(x, w):
    width = w.shape[0]
    length = x.shape[0]
    xp = jnp.pad(x, ((width - 1, 0), (0, 0)))
    return sum(xp[i:i + length] * w[i] for i in range(width))


def _gated_delta_rule_chunked(q, k, v, g, beta):
    h, length, dk = q.shape
    dv = v.shape[-1]
    n = length // GDN_CHUNK
    blk = lambda t: t.reshape(h, n, GDN_CHUNK, *t.shape[2:])
    q = blk(q) * dk ** -0.5
    k = blk(k)
    v = blk(v)
    beta = blk(beta)
    g = jnp.cumsum(blk(g), axis=-1)
    causal = jnp.tril(jnp.ones((GDN_CHUNK, GDN_CHUNK), dtype=bool))
    strict = jnp.tril(jnp.ones((GDN_CHUNK, GDN_CHUNK), dtype=bool), k=-1)
    decay = jnp.exp(jnp.where(causal, g[..., :, None] - g[..., None, :], -jnp.inf))
    k_beta = k * beta[..., None]
    a_strict = jnp.where(strict, jnp.einsum('hnid,hnjd->hnij', k_beta, k) * decay, 0.0)
    rhs = jnp.concatenate([v * beta[..., None], k_beta * jnp.exp(g)[..., None]], axis=-1)
    sol = _tri_solve(a_strict, rhs)
    u, w = sol[..., :dv], sol[..., dv:]
    attn = jnp.where(causal, jnp.einsum('hnid,hnjd->hnij', q, k) * decay, 0.0)
    g_last = g[..., -1]
    q_dec = q * jnp.exp(g)[..., None]
    k_dec = k * jnp.exp(g_last[..., None] - g)[..., None]

    return _gdn_scan(q_dec, k_dec, u, w, attn, g_last).reshape(h, length, dv)


_MM = {tag: _make_mm(tag) for tag in ("gdn_in", "gdn_ba", "gdn_out", "ffn_out0", "kv", "kv_f", "fox_out", "ffn_out1")}
_MM.update({tag: _make_mm(tag, dev_blocks=True) for tag in ("ffn_in0", "ffn_in1", "fox_in")})


def _swiglu(h, w_in, w_out, c_in, c_out, layer):
    gu = _MM[f"ffn_in{layer}"](h, w_in, c_in)
    gate, up = _split_cols(gu, (FFN_HIDDEN, FFN_HIDDEN))
    return _MM[f"ffn_out{layer}"](jax.nn.silu(gate) * up, w_out, c_out)


def _gated_deltanet(h, wts, car, conv_w, a_log, dt_bias, norm_w):
    length = h.shape[0]
    main = _MM["gdn_in"](h, wts["gdn_in"], car["gdn_in"])
    ba = _MM["gdn_ba"](h, wts["gdn_ba"], car["gdn_ba"])
    qkv, z = _split_cols(main, (CONV_DIM, GDN_V_DIM))
    beta_logit, a = ba[:, :GDN_V_HEADS], ba[:, GDN_V_HEADS:2 * GDN_V_HEADS]
    qkv = jax.nn.silu(_causal_conv(qkv, conv_w))
    q, k, v = _split_cols(qkv, (GDN_QK_DIM, GDN_QK_DIM, GDN_V_DIM))
    rep = GDN_V_HEADS // GDN_QK_HEADS
    heads = lambda t, nh: t.reshape(length, nh, GDN_HEAD_DIM)
    q = jnp.repeat(_l2_normalize(heads(q, GDN_QK_HEADS)), rep, axis=1)
    k = jnp.repeat(_l2_normalize(heads(k, GDN_QK_HEADS)), rep, axis=1)
    v = heads(v, GDN_V_HEADS)
    beta = jax.nn.sigmoid(beta_logit)
    g = -jnp.exp(a_log) * jax.nn.softplus(a + dt_bias)
    tr = lambda t: jnp.swapaxes(t, 0, 1)
    o = _gated_delta_rule_chunked(tr(q), tr(k), tr(v), tr(g), tr(beta))
    o = _rms_norm(tr(o), norm_w) * jax.nn.silu(heads(z, GDN_V_HEADS))
    return _MM["gdn_out"](o.reshape(length, GDN_V_DIM), wts["gdn_out"], car["gdn_out"])


def _local_loss(x, mods, small, car, wts, target):
    split6 = lambda m: tuple(m[i * D_MODEL:(i + 1) * D_MODEL][None, :] for i in range(6))
    sh_m, sc_m, g_m, sh_f, sc_f, g_f = split6(mods["ada0"])
    h = _modulate(_rms_norm(x, small["norm_mix"][0]), sh_m, sc_m)
    y = _gated_deltanet(h, wts, car, small["gdn_conv"], small["gdn_a_log"][0], small["gdn_dt_bias"][0],
                        small["gdn_norm"][0])
    x = x + g_m * y
    h = _modulate(_rms_norm(x, small["norm_ffn"][0]), sh_f, sc_f)
    x = x + g_f * _swiglu(h, wts["ffn_in0"], wts["ffn_out0"], car["ffn_in0"], car["ffn_out0"], 0)
    length = x.shape[0]
    kv_shift, kv_scale = mods["kvada"][None, :D_MODEL], mods["kvada"][None, D_MODEL:]
    hk = _modulate(_rms_norm(x, small["kv_norm"]), kv_shift, kv_scale)
    kv = _MM["kv"](hk, wts["kv"], car["kv"])
    f_logit = _MM["kv_f"](hk, wts["kv_f"], car["kv_f"])[:, :FOX_HEADS]
    k_raw, v_sh = _split_cols(kv, (FOX_KV_DIM, FOX_KV_DIM))
    k_sh = _rms_norm(k_raw.reshape(length, FOX_KV_HEADS, FOX_HEAD_DIM), small["k_norm"])
    k_sh = k_sh.reshape(length, FOX_KV_DIM)
    f_cum = jnp.cumsum(jax.nn.log_sigmoid(f_logit + small["forget_b"]), axis=0)
    sh_m, sc_m, g_m, sh_f, sc_f, g_f = split6(mods["ada1"])
    h = _modulate(_rms_norm(x, small["norm_mix"][1]), sh_m, sc_m)
    qg = _MM["fox_in"](h, wts["fox_in"], car["fox_in"])
    q_raw, q_gate = _split_cols(qg, (FOX_Q_DIM, FOX_Q_DIM))
    q = _rms_norm(q_raw.reshape(length, FOX_HEADS, FOX_HEAD_DIM), small["q_norm"][0])
    o = _fox_attention(q.reshape(length, FOX_Q_DIM), k_sh, v_sh, f_cum)
    y = _MM["fox_out"](o * jax.nn.sigmoid(q_gate), wts["fox_out"], car["fox_out"])
    x = x + g_m * y
    h = _modulate(_rms_norm(x, small["norm_ffn"][1]), sh_f, sc_f)
    x = x + g_f * _swiglu(h, wts["ffn_in1"], wts["ffn_out1"], car["ffn_in1"], car["ffn_out1"], 1)
    out = _modulate(_rms_norm(x, small["out_norm"]), mods["outada"][None, :D_MODEL], mods["outada"][None, D_MODEL:])
    err = jnp.square(out - target)
    return 0.5 * jnp.sum(jnp.mean(err, axis=-1))


def _pad_rows(a, rows):
    return jnp.pad(a, ((0, rows - a.shape[0]), (0, 0)))


def _pack_flat(parts, total):
    flat = jnp.concatenate([p.reshape(-1) for p in parts])
    return jnp.pad(flat, (0, total - flat.shape[0]))


_SMALL_NAMES = ("ada_b", "norm_mix", "norm_ffn", "gdn_a_log", "gdn_dt_bias", "gdn_norm", "kv_ada_b", "kv_norm",
                "k_norm", "forget_b", "q_norm", "out_ada_b", "out_norm")


def kernel(x, c, ada_w, ada_b, norm_mix, norm_ffn, ffn_w_in, ffn_w_out, gdn_w_in, gdn_conv, gdn_a_log, gdn_dt_bias, gdn_norm, gdn_w_out, kv_ada_w, kv_ada_b, kv_norm, kv_w, k_norm, forget_b, fox_w_in, q_norm, fox_w_out, out_ada_w, out_ada_b, out_norm, loss_target, m_ada_w, m_ada_b, m_norm_mix, m_norm_ffn, m_ffn_w_in, m_ffn_w_out, m_gdn_w_in, m_gdn_conv, m_gdn_a_log, m_gdn_dt_bias, m_gdn_norm, m_gdn_w_out, m_kv_ada_w, m_kv_ada_b, m_kv_norm, m_kv_w, m_k_norm, m_forget_b, m_fox_w_in, m_q_norm, m_fox_w_out, m_out_ada_w, m_out_ada_b, m_out_norm, v_ada_w, v_ada_b, v_norm_mix, v_norm_ffn, v_ffn_w_in, v_ffn_w_out, v_gdn_w_in, v_gdn_conv, v_gdn_a_log, v_gdn_dt_bias, v_gdn_norm, v_gdn_w_out, v_kv_ada_w, v_kv_ada_b, v_kv_norm, v_kv_w, v_k_norm, v_forget_b, v_fox_w_in, v_q_norm, v_fox_w_out, v_out_ada_w, v_out_ada_b, v_out_norm):
    w_in = dict(ada_w=ada_w, ada_b=ada_b, norm_mix=norm_mix, norm_ffn=norm_ffn, ffn_w_in=ffn_w_in, ffn_w_out=ffn_w_out, gdn_w_in=gdn_w_in, gdn_conv=gdn_conv, gdn_a_log=gdn_a_log, gdn_dt_bias=gdn_dt_bias, gdn_norm=gdn_norm, gdn_w_out=gdn_w_out, kv_ada_w=kv_ada_w, kv_ada_b=kv_ada_b, kv_norm=kv_norm, kv_w=kv_w, k_norm=k_norm, forget_b=forget_b, fox_w_in=fox_w_in, q_norm=q_norm, fox_w_out=fox_w_out, out_ada_w=out_ada_w, out_ada_b=out_ada_b, out_norm=out_norm)
    m_in = dict(ada_w=m_ada_w, ada_b=m_ada_b, norm_mix=m_norm_mix, norm_ffn=m_norm_ffn, ffn_w_in=m_ffn_w_in, ffn_w_out=m_ffn_w_out, gdn_w_in=m_gdn_w_in, gdn_conv=m_gdn_conv, gdn_a_log=m_gdn_a_log, gdn_dt_bias=m_gdn_dt_bias, gdn_norm=m_gdn_norm, gdn_w_out=m_gdn_w_out, kv_ada_w=m_kv_ada_w, kv_ada_b=m_kv_ada_b, kv_norm=m_kv_norm, kv_w=m_kv_w, k_norm=m_k_norm, forget_b=m_forget_b, fox_w_in=m_fox_w_in, q_norm=m_q_norm, fox_w_out=m_fox_w_out, out_ada_w=m_out_ada_w, out_ada_b=m_out_ada_b, out_norm=m_out_norm)
    v_in = dict(ada_w=v_ada_w, ada_b=v_ada_b, norm_mix=v_norm_mix, norm_ffn=v_norm_ffn, ffn_w_in=v_ffn_w_in, ffn_w_out=v_ffn_w_out, gdn_w_in=v_gdn_w_in, gdn_conv=v_gdn_conv, gdn_a_log=v_gdn_a_log, gdn_dt_bias=v_gdn_dt_bias, gdn_norm=v_gdn_norm, gdn_w_out=v_gdn_w_out, kv_ada_w=v_kv_ada_w, kv_ada_b=v_kv_ada_b, kv_norm=v_kv_norm, kv_w=v_kv_w, k_norm=v_k_norm, forget_b=v_forget_b, fox_w_in=v_fox_w_in, q_norm=v_q_norm, fox_w_out=v_fox_w_out, out_ada_w=v_out_ada_w, out_ada_b=v_out_ada_b, out_norm=v_out_norm)
    names = list(w_in)
    me = 4 * lax.axis_index("x") + 2 * lax.axis_index("y") + lax.axis_index("c")
    d = D_MODEL

    n_conv = CONV_K * (CONV_DIM // N_DEV)
    first = _pack_flat([c, gdn_conv], d + n_conv).reshape(-1, LANES)
    first_all = _all_gather(first, name="ag_cond_conv", in_vmem=True).reshape(N_DEV, d + n_conv)
    cond_all = jax.nn.silu(first_all[:, :d])
    conv_full = first_all[:, d:].reshape(N_DEV, CONV_K, CONV_DIM // N_DEV)
    conv_full = jnp.transpose(conv_full, (1, 0, 2)).reshape(CONV_K, CONV_DIM)

    ada_cat = jnp.concatenate([ada_w[0], ada_w[1], kv_ada_w, out_ada_w], axis=1)
    n_ada = ada_cat.shape[1]
    cond_pad = _pad_rows(cond_all, LANES)
    mods_part = _matmul(cond_pad, ada_cat, name="mm_ada")[:N_DEV]
    mods_all = _all_gather(mods_part.reshape(-1, LANES), name="ag_mods", in_vmem=True)
    mods_all = mods_all.reshape(N_DEV, N_DEV, n_ada)
    mine = lax.dynamic_index_in_dim(mods_all, me, axis=1, keepdims=False)
    s6 = 6 * d // N_DEV
    s2 = 2 * d // N_DEV
    mods = {
        "ada0": mine[:, :s6].reshape(-1) + ada_b[0],
        "ada1": mine[:, s6:2 * s6].reshape(-1) + ada_b[1],
        "kvada": mine[:, 2 * s6:2 * s6 + s2].reshape(-1) + kv_ada_b,
        "outada": mine[:, 2 * s6 + s2:].reshape(-1) + out_ada_b,
    }

    big = [ffn_w_in[0], ffn_w_in[1], ffn_w_out[0], ffn_w_out[1], gdn_w_in[0], gdn_w_out[0], kv_w, fox_w_in[0],
           fox_w_out[0]]
    g_ffn_in0, g_ffn_in1, g_ffn_out0, g_ffn_out1, g_gdn_in, g_gdn_out, g_kv, g_fox_in, g_fox_out = (
        _all_gather_by_chip([b.astype(BF16) for b in big], name="ag_weights"))
    rows = lambda t: t.reshape(N_DEV * t.shape[1], t.shape[2])
    per_gdn = GDN_PROJ // N_DEV
    last_main = GDN_MAIN - (N_DEV - 1) * per_gdn
    gdn_main = jnp.concatenate([g_gdn_in[dv] for dv in range(N_DEV - 1)] + [g_gdn_in[N_DEV - 1][:, :last_main]],
                               axis=1)
    gdn_ba = jnp.pad(g_gdn_in[N_DEV - 1][:, last_main:], ((0, 0), (0, LANES - 2 * GDN_V_HEADS)))
    kv_full = rows(g_kv)
    wts = {
        "ffn_in0": g_ffn_in0, "ffn_in1": g_ffn_in1, "ffn_out0": rows(g_ffn_out0), "ffn_out1": rows(g_ffn_out1),
        "gdn_in": gdn_main, "gdn_ba": gdn_ba, "gdn_out": rows(g_gdn_out),
        "kv": kv_full[:, :2 * FOX_KV_DIM],
        "kv_f": jnp.pad(kv_full[:, 2 * FOX_KV_DIM:], ((0, 0), (0, LANES - FOX_HEADS))),
        "fox_in": g_fox_in, "fox_out": rows(g_fox_out),
    }
    car = {k: jnp.zeros(w.shape, BF16) for k, w in wts.items()}
    small = {n: w_in[n] for n in _SMALL_NAMES if n not in ("ada_b", "kv_ada_b", "out_ada_b")}
    small["gdn_conv"] = conv_full

    loss_fn = functools.partial(_local_loss, wts=wts, target=loss_target[0])
    loss_local, vjp = jax.vjp(loss_fn, x[0], mods, small, car)
    gx, gmods, gsmall, gcar = vjp(jnp.ones((), F32))
    loss = lax.psum(loss_local, ("x", "y", "c"))

    gsm = dict(gsmall)
    gsm["ada_b"] = jnp.stack([gmods["ada0"], gmods["ada1"]])
    gsm["kv_ada_b"] = gmods["kvada"]
    gsm["out_ada_b"] = gmods["outada"]
    small_sizes = [w_in[n].size for n in _SMALL_NAMES]
    n_small = sum(small_sizes) + CONV_K * CONV_DIM
    small_rows = -(-n_small // (LANES * 8)) * 8
    vec = _pack_flat([gsm[n] for n in _SMALL_NAMES] + [gsm["gdn_conv"]], small_rows * LANES)
    vec_all = _all_gather(vec.reshape(small_rows, LANES), name="ag_small_grads", in_vmem=True)
    vec_sum = _sum_slots(vec_all, name="sum_small_grads").reshape(-1)
    grads = {}
    off = 0
    for n, sz in zip(_SMALL_NAMES, small_sizes):
        grads[n] = vec_sum[off:off + sz].reshape(w_in[n].shape)
        off += sz
    conv_sum = vec_sum[off:off + CONV_K * CONV_DIM].reshape(CONV_K, N_DEV, CONV_DIM // N_DEV)
    grads["gdn_conv"] = lax.dynamic_index_in_dim(conv_sum, me, axis=1, keepdims=False)[None]

    vec_flat = vec_all.reshape(N_DEV, -1)
    d_ada = vec_flat[:, :2 * 6 * d].reshape(N_DEV, 2, N_DEV, s6)
    o_kv = sum(small_sizes[:6])
    d_kvada = vec_flat[:, o_kv:o_kv + 2 * d].reshape(N_DEV, N_DEV, s2)
    o_out = sum(small_sizes[:11])
    d_outada = vec_flat[:, o_out:o_out + 2 * d].reshape(N_DEV, N_DEV, s2)
    pick = lambda t, axis: lax.dynamic_index_in_dim(t, me, axis=axis, keepdims=False)
    d_cat = jnp.concatenate([pick(d_ada[:, 0], 1), pick(d_ada[:, 1], 1), pick(d_kvada, 1), pick(d_outada, 1)],
                            axis=1)
    g_ada_cat = _matmul(cond_pad, _pad_rows(d_cat, LANES), trans_a=True, name="mm_ada_dw")
    grads["ada_w"] = jnp.stack([g_ada_cat[:, :s6], g_ada_cat[:, s6:2 * s6]])
    grads["kv_ada_w"] = g_ada_cat[:, 2 * s6:2 * s6 + s2]
    grads["out_ada_w"] = g_ada_cat[:, 2 * s6 + s2:]

    d_main, d_ba = gcar["gdn_in"], gcar["gdn_ba"]
    d_gdn_in = jnp.stack(
        [d_main[:, dv * per_gdn:(dv + 1) * per_gdn] for dv in range(N_DEV - 1)]
        + [jnp.concatenate([d_main[:, (N_DEV - 1) * per_gdn:], d_ba[:, :2 * GDN_V_HEADS]], axis=1)])
    d_kv = jnp.concatenate([gcar["kv"], gcar["kv_f"][:, :FOX_HEADS]], axis=1)
    owner_rows = lambda t: t.reshape(N_DEV, t.shape[0] // N_DEV, t.shape[1])
    by_owner = [gcar["ffn_in0"], gcar["ffn_in1"], owner_rows(gcar["ffn_out0"]), owner_rows(gcar["ffn_out1"]),
                d_gdn_in, owner_rows(gcar["gdn_out"]), owner_rows(d_kv), gcar["fox_in"],
                owner_rows(gcar["fox_out"])]
    by_chip_core = [t.reshape((N_CHIP, 2) + t.shape[1:]) for t in by_owner]
    my_c = lax.axis_index("c")
    from_sibling = _exchange_sibling(by_chip_core, name="rs_sibling")
    shard_grads = []
    pair_sums = []
    for idx, (t, got) in enumerate(zip(by_chip_core, from_sibling)):
        for_my_core = lax.dynamic_index_in_dim(t, my_c, axis=1, keepdims=False)
        pair_sums.append(_add_pair(for_my_core, got, name=f"rs_pair_sum{idx}"))
    by_chip = _exchange_chips(pair_sums, name="rs_chips")
    for idx, t in enumerate(by_chip):
        shard_grads.append(_sum_slots(t, name=f"sum_grads{idx}"))
    grads["ffn_w_in"] = jnp.stack(shard_grads[0:2])
    grads["ffn_w_out"] = jnp.stack(shard_grads[2:4])
    grads["gdn_w_in"] = shard_grads[4][None]
    grads["gdn_w_out"] = shard_grads[5][None]
    grads["kv_w"] = shard_grads[6]
    grads["fox_w_in"] = shard_grads[7][None]
    grads["fox_w_out"] = shard_grads[8][None]

    delta, new_m, new_v = {}, {}, {}
    for n in names:
        if n in _SMALL_NAMES:
            continue
        delta[n], new_m[n], new_v[n] = _adamw(w_in[n], grads[n], m_in[n], v_in[n], name=f"adamw_{n}")
    sm_rows = -(-sum(small_sizes) // (LANES * 8)) * 8
    pk = lambda src: _pack_flat([src[n] for n in _SMALL_NAMES], sm_rows * LANES).reshape(sm_rows, LANES)
    d_pk, m_pk, vn_pk = _adamw(pk(w_in), pk(grads), pk(m_in), pk(v_in), name="adamw_small")
    off = 0
    for n, sz in zip(_SMALL_NAMES, small_sizes):
        shp = w_in[n].shape
        delta[n] = d_pk.reshape(-1)[off:off + sz].reshape(shp)
        new_m[n] = m_pk.reshape(-1)[off:off + sz].reshape(shp)
        new_v[n] = vn_pk.reshape(-1)[off:off + sz].reshape(shp)
        off += sz

    return (loss, gx[None], *[grads[n] for n in names], *[delta[n] for n in names],
            *[new_m[n] for n in names], *[new_v[n] for n in names])
```

```python
import functools

import jax
import jax.numpy as jnp
from jax import lax
from jax.experimental import pallas as pl
from jax.experimental.pallas import tpu as pltpu

F32 = jnp.float32
BF16 = jnp.bfloat16
MESH = pl.DeviceIdType.MESH
N_DEV = 8

D_MODEL = 2048
SEQ = 8192
GDN_QK_HEADS = 16
GDN_V_HEADS = 32
GDN_HEAD_DIM = 128
GDN_QK_DIM = GDN_QK_HEADS * GDN_HEAD_DIM
GDN_V_DIM = GDN_V_HEADS * GDN_HEAD_DIM
CONV_DIM = 2 * GDN_QK_DIM + GDN_V_DIM
GDN_MAIN = CONV_DIM + GDN_V_DIM
GDN_PROJ = GDN_MAIN + 2 * GDN_V_HEADS
CONV_K = 4
GDN_CHUNK = 64
FOX_HEADS = 16
FOX_KV_HEADS = 2
FOX_GROUP = FOX_HEADS // FOX_KV_HEADS
FOX_HEAD_DIM = 256
FOX_Q_DIM = FOX_HEADS * FOX_HEAD_DIM
FOX_KV_DIM = FOX_KV_HEADS * FOX_HEAD_DIM
KV_PROJ = 2 * FOX_KV_DIM + FOX_HEADS
FFN_HIDDEN = 5632
NORM_EPS = 1e-6

ADAM_LR = 0.001
ADAM_B1 = 0.9
ADAM_B2 = 0.999
ADAM_EPS = 1e-08
ADAM_WD = 0.01
ADAM_STEP = 10

LANES = 128
VMEM_LIMIT = 56 * 1024 * 1024
PACK_COLS = 1024
ATT_BLOCK = 512
NEG = -1e30


def _pick(n, cands):
    for c in cands:
        if n % c == 0:
            return c
    return n


def _cparams(sem):
    return pltpu.CompilerParams(dimension_semantics=sem, vmem_limit_bytes=VMEM_LIMIT)


def _matmul(a, b, *, trans_a=False, trans_b=False, b_dev_blocks=False, out_dev_blocks=False, out_dtype=F32, name):
    if trans_a:
        kdim, m = a.shape
    else:
        m, kdim = a.shape
    if b_dev_blocks:
        n_dev, rows_b, per_dev = b.shape
        n, kb = (rows_b, n_dev * per_dev) if trans_b else (n_dev * per_dev, rows_b)
    elif trans_b:
        n, kb = b.shape
    else:
        kb, n = b.shape
    assert kdim == kb, (a.shape, b.shape)
    tm = _pick(m, (1024, 512, 256, 128))
    tn = _pick(n // N_DEV if (out_dev_blocks or (b_dev_blocks and not trans_b)) else n, (1408, 1024, 512, 256, 128))
    tk = _pick(kdim // N_DEV if (b_dev_blocks and trans_b) else kdim, (1024, 1408, 512, 256, 128))
    nk = kdim // tk
    dims = (((0 if trans_a else 1,), (1 if trans_b else 0,)), ((), ()))

    def body(a_ref, b_ref, o_ref, acc_ref):
        k = pl.program_id(2)

        @pl.when(k == 0)
        def _():
            acc_ref[...] = jnp.zeros_like(acc_ref)

        acc_ref[...] += lax.dot_general(a_ref[...].astype(BF16), b_ref[...].astype(BF16), dims,
                                        preferred_element_type=F32)

        @pl.when(k == nk - 1)
        def _():
            o_ref[...] = acc_ref[...].astype(o_ref.dtype)

    a_spec = (pl.BlockSpec((tk, tm), lambda i, j, k: (k, i)) if trans_a
              else pl.BlockSpec((tm, tk), lambda i, j, k: (i, k)))
    if b_dev_blocks and trans_b:
        per = kdim // N_DEV // tk
        b_spec = pl.BlockSpec((None, tn, tk), lambda i, j, k: (k // per, j, k % per))
    elif b_dev_blocks:
        per = n // N_DEV // tn
        b_spec = pl.BlockSpec((None, tk, tn), lambda i, j, k: (j // per, k, j % per))
    elif trans_b:
        b_spec = pl.BlockSpec((tn, tk), lambda i, j, k: (j, k))
    else:
        b_spec = pl.BlockSpec((tk, tn), lambda i, j, k: (k, j))
    if out_dev_blocks:
        per_o = n // N_DEV // tn
        out_spec = pl.BlockSpec((None, tm, tn), lambda i, j, k: (j // per_o, i, j % per_o))
        out_shape = jax.ShapeDtypeStruct((N_DEV, m, n // N_DEV), out_dtype)
    else:
        out_spec = pl.BlockSpec((tm, tn), lambda i, j, k: (i, j))
        out_shape = jax.ShapeDtypeStruct((m, n), out_dtype)
    return pl.pallas_call(
        body, name=name,
        grid=(m // tm, n // tn, nk),
        in_specs=[a_spec, b_spec],
        out_specs=out_spec,
        out_shape=out_shape,
        scratch_shapes=[pltpu.VMEM((tm, tn), F32)],
        compiler_params=_cparams(("parallel", "parallel", "arbitrary")),
    )(a, b)


def _make_mm(tag, dev_blocks=False):
    @jax.custom_vjp
    def mm(a, w, carrier):
        return _matmul(a, w, b_dev_blocks=dev_blocks, name=f"mm_{tag}")

    def fwd(a, w, carrier):
        return _matmul(a, w, b_dev_blocks=dev_blocks, name=f"mm_{tag}"), (a, w)

    def bwd(res, g):
        a, w = res
        da = _matmul(g, w, trans_b=True, b_dev_blocks=dev_blocks, name=f"mm_{tag}_da")
        dw = _matmul(a, g, trans_a=True, out_dev_blocks=dev_blocks, out_dtype=BF16, name=f"mm_{tag}_dw")
        return da, jnp.zeros_like(w), dw

    mm.defvjp(fwd, bwd)
    return mm


_NT = (((1,), (1,)), ((), ()))
_FOX_SCALE = FOX_HEAD_DIM ** -0.5


def _lane_groups_sum(x):
    out = x[:, :LANES]
    for i in range(1, x.shape[1] // LANES):
        out = out + x[:, i * LANES:(i + 1) * LANES]
    return out


def _fox_fwd_call(q, kb, vb, f_col, f_row):
    length = q.shape[0]
    t = ATT_BLOCK
    tq = 2 * t
    nq = length // tq
    hd = FOX_HEAD_DIM

    def body(q_ref, k_ref, v_ref, fq_ref, fk_ref, o_ref, lse_ref, qs_sc, m_sc, l_sc, acc_sc):
        i = pl.program_id(1)
        qs_sc[...] = (q_ref[...] * _FOX_SCALE).astype(BF16)
        m_sc[...] = jnp.full_like(m_sc, NEG)
        l_sc[...] = jnp.zeros_like(l_sc)
        acc_sc[...] = jnp.zeros_like(acc_sc)

        def block(j, modes):
            off = pl.multiple_of(j * t, t)
            kj = k_ref[pl.ds(off, t), :]
            vj = v_ref[pl.ds(off, t), :]
            fk = fk_ref[0, :, pl.ds(off, t)]
            for a in (0, 1):
                if modes[a] is None:
                    continue
                rows = pl.ds(a * t, t)
                s = lax.dot_general(qs_sc[rows, :], kj, _NT, preferred_element_type=F32)
                s = s + fq_ref[0, rows, :] - fk
                if modes[a]:
                    r = lax.broadcasted_iota(jnp.int32, (t, t), 0)
                    c = lax.broadcasted_iota(jnp.int32, (t, t), 1)
                    s = jnp.where(r >= c, s, NEG)
                m_old = m_sc[a]
                m_new = jnp.maximum(m_old, jnp.max(s, axis=1, keepdims=True))
                alpha = jnp.exp(m_old - m_new)
                p = jnp.exp(s - m_new)
                l_sc[a] = alpha * l_sc[a] + _lane_groups_sum(p)
                acc_sc[a] = alpha * acc_sc[a] + jnp.dot(p.astype(BF16), vj, preferred_element_type=F32)
                m_sc[a] = m_new

        def loop_body(j, carry):
            block(j, (False, False))
            return carry

        lax.fori_loop(0, 2 * i, loop_body, 0)
        block(2 * i, (True, False))
        block(2 * i + 1, (None, True))
        for a in (0, 1):
            rows = pl.ds(a * t, t)
            l_row = jnp.sum(l_sc[a], axis=1, keepdims=True)
            o_ref[rows, :] = acc_sc[a] / l_row
            lse_ref[0, rows, :] = m_sc[a] + jnp.log(l_row)

    return pl.pallas_call(
        body, name="fox_fwd",
        grid=(FOX_HEADS, nq),
        in_specs=[
            pl.BlockSpec((tq, hd), lambda h, i: (i, h)),
            pl.BlockSpec((length, hd), lambda h, i: (0, h // FOX_GROUP)),
            pl.BlockSpec((length, hd), lambda h, i: (0, h // FOX_GROUP)),
            pl.BlockSpec((1, tq, 1), lambda h, i: (h, i, 0)),
            pl.BlockSpec((1, 1, length), lambda h, i: (h, 0, 0)),
        ],
        out_specs=[
            pl.BlockSpec((tq, hd), lambda h, i: (i, h)),
            pl.BlockSpec((1, tq, 1), lambda h, i: (h, i, 0)),
        ],
        out_shape=[jax.ShapeDtypeStruct((length, FOX_Q_DIM), F32),
                   jax.ShapeDtypeStruct((FOX_HEADS, length, 1), F32)],
        scratch_shapes=[pltpu.VMEM((tq, hd), BF16), pltpu.VMEM((2, t, 1), F32), pltpu.VMEM((2, t, LANES), F32),
                        pltpu.VMEM((2, t, hd), F32)],
        compiler_params=_cparams(("parallel", "arbitrary")),
    )(q, kb, vb, f_col, f_row)


def _fox_dq_call(q, kb, vb, f_col, f_row, lse_col, do):
    length = q.shape[0]
    t = ATT_BLOCK
    tq = 2 * t
    nq = length // tq
    hd = FOX_HEAD_DIM

    def body(q_ref, k_ref, v_ref, fq_ref, fk_ref, lse_ref, do_ref, dq_ref, dl_ref, qs_sc, do_sc, acc_sc, dl_sc,
             dl_lane_sc):
        i = pl.program_id(1)
        qs_sc[...] = (q_ref[...] * _FOX_SCALE).astype(BF16)
        do_sc[...] = do_ref[...].astype(BF16)
        acc_sc[...] = jnp.zeros_like(acc_sc)
        dl_lane_sc[...] = jnp.zeros_like(dl_lane_sc)

        def block(j, modes, second_pass):
            off = pl.multiple_of(j * t, t)
            kj = k_ref[pl.ds(off, t), :]
            vj = v_ref[pl.ds(off, t), :]
            fk = fk_ref[0, :, pl.ds(off, t)]
            for a in (0, 1):
                if modes[a] is None:
                    continue
                rows = pl.ds(a * t, t)
                s = lax.dot_general(qs_sc[rows, :], kj, _NT, preferred_element_type=F32)
                s = s + fq_ref[0, rows, :] - fk
                if modes[a]:
                    r = lax.broadcasted_iota(jnp.int32, (t, t), 0)
                    c = lax.broadcasted_iota(jnp.int32, (t, t), 1)
                    s = jnp.where(r >= c, s, NEG)
                p = jnp.exp(s - lse_ref[0, rows, :])
                dp = lax.dot_general(do_sc[rows, :], vj, _NT, preferred_element_type=F32)
                if second_pass:
                    ds = p * (dp - dl_sc[a])
                    acc_sc[a] += jnp.dot(ds.astype(BF16), kj, preferred_element_type=F32)
                else:
                    dl_lane_sc[a] += _lane_groups_sum(p * dp)

        def sweep(second_pass):
            def loop_body(j, carry):
                block(j, (False, False), second_pass)
                return carry

            lax.fori_loop(0, 2 * i, loop_body, 0)
            block(2 * i, (True, False), second_pass)
            block(2 * i + 1, (None, True), second_pass)

        sweep(False)
        for a in (0, 1):
            dl_sc[a] = jnp.sum(dl_lane_sc[a], axis=1, keepdims=True)
        sweep(True)
        for a in (0, 1):
            rows = pl.ds(a * t, t)
            dq_ref[rows, :] = acc_sc[a] * _FOX_SCALE
            dl_ref[0, rows, :] = dl_sc[a]

    col = pl.BlockSpec((1, tq, 1), lambda h, i: (h, i, 0))
    return pl.pallas_call(
        body, name="fox_dq",
        grid=(FOX_HEADS, nq),
        in_specs=[
            pl.BlockSpec((tq, hd), lambda h, i: (i, h)),
            pl.BlockSpec((length, hd), lambda h, i: (0, h // FOX_GROUP)),
            pl.BlockSpec((length, hd), lambda h, i: (0, h // FOX_GROUP)),
            col,
            pl.BlockSpec((1, 1, length), lambda h, i: (h, 0, 0)),
            col,
            pl.BlockSpec((tq, hd), lambda h, i: (i, h)),
        ],
        out_specs=[pl.BlockSpec((tq, hd), lambda h, i: (i, h)), col],
        out_shape=[jax.ShapeDtypeStruct((length, FOX_Q_DIM), F32),
                   jax.ShapeDtypeStruct((FOX_HEADS, length, 1), F32)],
        scratch_shapes=[pltpu.VMEM((tq, hd), BF16), pltpu.VMEM((tq, hd), BF16), pltpu.VMEM((2, t, hd), F32),
                        pltpu.VMEM((2, t, 1), F32), pltpu.VMEM((2, t, LANES), F32)],
        compiler_params=_cparams(("parallel", "arbitrary")),
    )(q, kb, vb, f_col, f_row, lse_col, do)


def _fox_dkv_call(qs, dob, kb, vb, f_col, f_row, lse_row, dl_row):
    length = qs.shape[0]
    t = ATT_BLOCK
    tk = 2 * t
    nq = length // t
    hd = FOX_HEAD_DIM

    def body(k_ref, v_ref, q_ref, do_ref, fk_ref, fq_ref, lse_ref, dl_ref,
             dk_ref, dv_ref, df_ref, dk_sc, dv_sc, df_sc):
        j = pl.program_id(1)
        dk_sc[...] = jnp.zeros_like(dk_sc)
        dv_sc[...] = jnp.zeros_like(dv_sc)
        df_sc[...] = jnp.zeros_like(df_sc)

        def block(i, modes):
            off = pl.multiple_of(i * t, t)
            qi = q_ref[pl.ds(off, t), :]
            doi = do_ref[pl.ds(off, t), :]
            fq = fq_ref[0, :, pl.ds(off, t)]
            lse = lse_ref[0, :, pl.ds(off, t)]
            dl = dl_ref[0, :, pl.ds(off, t)]
            for b in (0, 1):
                if modes[b] is None:
                    continue
                keys = pl.ds(b * t, t)
                st = lax.dot_general(k_ref[keys, :], qi, _NT, preferred_element_type=F32)
                st = st + fq - fk_ref[0, keys, :]
                if modes[b]:
                    r = lax.broadcasted_iota(jnp.int32, (t, t), 0)
                    c = lax.broadcasted_iota(jnp.int32, (t, t), 1)
                    st = jnp.where(c >= r, st, NEG)
                pt = jnp.exp(st - lse)
                dv_sc[b] += jnp.dot(pt.astype(BF16), doi, preferred_element_type=F32)
                dpt = lax.dot_general(v_ref[keys, :], doi, _NT, preferred_element_type=F32)
                dst = pt * (dpt - dl)
                dk_sc[b] += jnp.dot(dst.astype(BF16), qi, preferred_element_type=F32)
                df_sc[b] += _lane_groups_sum(dst)

        def loop_body(i, carry):
            block(i, (False, False))
            return carry

        block(2 * j, (True, None))
        block(2 * j + 1, (False, True))
        lax.fori_loop(2 * j + 2, nq, loop_body, 0)
        for b in (0, 1):
            keys = pl.ds(b * t, t)
            dk_ref[keys, :] = dk_sc[b]
            dv_ref[keys, :] = dv_sc[b]
            df_ref[0, keys, :] = -jnp.sum(df_sc[b], axis=1, keepdims=True)

    row = pl.BlockSpec((1, 1, length), lambda h, j: (h, 0, 0))
    return pl.pallas_call(
        body, name="fox_dkv",
        grid=(FOX_HEADS, length // tk),
        in_specs=[
            pl.BlockSpec((tk, hd), lambda h, j: (j, h // FOX_GROUP)),
            pl.BlockSpec((tk, hd), lambda h, j: (j, h // FOX_GROUP)),
            pl.BlockSpec((length, hd), lambda h, j: (0, h)),
            pl.BlockSpec((length, hd), lambda h, j: (0, h)),
            pl.BlockSpec((1, tk, 1), lambda h, j: (h, j, 0)),
            row, row, row,
        ],
        out_specs=[
            pl.BlockSpec((tk, hd), lambda h, j: (j, h)),
            pl.BlockSpec((tk, hd), lambda h, j: (j, h)),
            pl.BlockSpec((1, tk, 1), lambda h, j: (h, j, 0)),
        ],
        out_shape=[jax.ShapeDtypeStruct((length, FOX_Q_DIM), F32),
                   jax.ShapeDtypeStruct((length, FOX_Q_DIM), F32),
                   jax.ShapeDtypeStruct((FOX_HEADS, length, 1), F32)],
        scratch_shapes=[pltpu.VMEM((2, t, hd), F32), pltpu.VMEM((2, t, hd), F32), pltpu.VMEM((2, t, LANES), F32)],
        compiler_params=_cparams(("parallel", "arbitrary")),
    )(kb, vb, qs, dob, f_col, f_row, lse_row, dl_row)


@jax.custom_vjp
def _fox_attention(q, k, v, f):
    return _fox_attention_fwd(q, k, v, f)[0]


def _fox_attention_fwd(q, k, v, f):
    kb = k.astype(BF16)
    vb = v.astype(BF16)
    f_row = jnp.transpose(f)[:, None, :]
    f_col = jnp.transpose(f)[:, :, None]
    o, lse = _fox_fwd_call(q, kb, vb, f_col, f_row)
    return o, (q, kb, vb, f_col, f_row, lse)


def _fox_attention_bwd(res, do):
    q, kb, vb, f_col, f_row, lse = res
    length = q.shape[0]
    lse_row = jnp.transpose(lse, (0, 2, 1))
    dq, dl_col = _fox_dq_call(q, kb, vb, f_col, f_row, lse, do)
    dl_row = jnp.transpose(dl_col, (0, 2, 1))
    qs =(q * _FOX_SCALE).astype(BF16)
    dkh, dvh, dfk = _fox_dkv_call(qs, do.astype(BF16), kb, vb, f_col, f_row, lse_row, dl_row)
    dk = jnp.sum(dkh.reshape(length, FOX_KV_HEADS, FOX_GROUP, FOX_HEAD_DIM), axis=2).reshape(length, FOX_KV_DIM)
    dv = jnp.sum(dvh.reshape(length, FOX_KV_HEADS, FOX_GROUP, FOX_HEAD_DIM), axis=2).reshape(length, FOX_KV_DIM)
    df = jnp.transpose(dfk[:, :, 0])
    return dq, dk, dv, df


_fox_attention.defvjp(_fox_attention_fwd, _fox_attention_bwd)


def _peer(mx, my, mc, k):
    return (1 - mx if k & 4 else mx, 1 - my if k & 2 else my, 1 - mc if k & 1 else mc)


def _all_gather(x, *, name, in_vmem):
    space = pltpu.VMEM if in_vmem else pl.ANY

    def body(x_ref, out_ref, send_sems, recv_sems, local_sem):
        mx, my, mc = lax.axis_index("x"), lax.axis_index("y"), lax.axis_index("c")
        me = 4 * mx + 2 * my + mc
        mine = pltpu.make_async_copy(x_ref, out_ref.at[me], local_sem)
        mine.start()
        copies = []
        for k in range(1, N_DEV):
            cp = pltpu.make_async_remote_copy(
                src_ref=x_ref, dst_ref=out_ref.at[me],
                send_sem=send_sems.at[k - 1], recv_sem=recv_sems.at[k - 1],
                device_id=_peer(mx, my, mc, k), device_id_type=MESH)
            cp.start()
            copies.append(cp)
        for cp in copies:
            cp.wait()
        mine.wait()

    return pl.pallas_call(
        body, name=name,
        out_shape=jax.ShapeDtypeStruct((N_DEV,) + x.shape, x.dtype),
        in_specs=[pl.BlockSpec(memory_space=space)],
        out_specs=pl.BlockSpec(memory_space=space),
        scratch_shapes=[pltpu.SemaphoreType.DMA((N_DEV - 1,)), pltpu.SemaphoreType.DMA((N_DEV - 1,)),
                        pltpu.SemaphoreType.DMA],
    )(x)


N_CHIP = N_DEV // 2
_HBM = pl.BlockSpec(memory_space=pl.ANY)


def _all_gather_by_chip(xs, *, name):
    n_arr = len(xs)
    per = N_DEV - 1

    def body(*refs):
        x_refs, out_refs = refs[:n_arr], refs[n_arr:2 * n_arr]
        send_sems, recv_sems, local_sems = refs[2 * n_arr:]
        mx, my, mc = lax.axis_index("x"), lax.axis_index("y"), lax.axis_index("c")
        me, sibling = (mx, my, mc), (mx, my, 1 - mc)
        chips = [(1 - mx, my), (mx, 1 - my), (1 - mx, 1 - my)]

        def copy(a, k, block, to, from_input=False):
            px, py, pc = block
            slot = out_refs[a].at[4 * px + 2 * py + pc]
            return pltpu.make_async_remote_copy(
                src_ref=x_refs[a] if from_input else slot, dst_ref=slot,
                send_sem=send_sems.at[a * per + k], recv_sem=recv_sems.at[a * per + k],
                device_id=to, device_id_type=MESH)

        mine = [pltpu.make_async_copy(x_refs[a], out_refs[a].at[4 * mx + 2 * my + mc], local_sems.at[a])
                for a in range(n_arr)]
        for cp in mine:
            cp.start()
        sent = []
        for a in range(n_arr):
            sent.append(copy(a, 0, me, sibling, from_input=True))
            sent += [copy(a, 1 + j, me, (*chip, mc), from_input=True) for j, chip in enumerate(chips)]
        for cp in sent:
            cp.start()
        for a in range(n_arr):
            for j, chip in enumerate(chips):
                copy(a, 1 + j, (*chip, mc), me).wait_recv()
                passed = copy(a, 4 + j, (*chip, mc), sibling)
                passed.start()
                sent.append(passed)
        for a in range(n_arr):
            copy(a, 0, sibling, me).wait_recv()
            for j, chip in enumerate(chips):
                copy(a, 4 + j, (*chip, 1 - mc), me).wait_recv()
        for cp in sent:
            cp.wait_send()
        for cp in mine:
            cp.wait()

    return pl.pallas_call(
        body, name=name,
        out_shape=[jax.ShapeDtypeStruct((N_DEV,) + x.shape, x.dtype) for x in xs],
        in_specs=[_HBM] * n_arr,
        out_specs=[_HBM] * n_arr,
        scratch_shapes=[pltpu.SemaphoreType.DMA((n_arr * per,)), pltpu.SemaphoreType.DMA((n_arr * per,)),
                        pltpu.SemaphoreType.DMA((n_arr,))],
    )(*xs)


def _exchange_sibling(xs, *, name):
    n_arr = len(xs)

    def body(*refs):
        x_refs, out_refs = refs[:n_arr], refs[n_arr:2 * n_arr]
        send_sems, recv_sems = refs[2 * n_arr:]
        mc = lax.axis_index("c")
        sibling = (lax.axis_index("x"), lax.axis_index("y"), 1 - mc)
        copies = [pltpu.make_async_remote_copy(
            src_ref=x_refs[a].at[q, 1 - mc], dst_ref=out_refs[a].at[q],
            send_sem=send_sems.at[a * N_CHIP + q], recv_sem=recv_sems.at[a * N_CHIP + q],
            device_id=sibling, device_id_type=MESH) for a in range(n_arr) for q in range(N_CHIP)]
        for cp in copies:
            cp.start()
        for cp in copies:
            cp.wait()

    return pl.pallas_call(
        body, name=name,
        out_shape=[jax.ShapeDtypeStruct((N_CHIP,) + x.shape[2:], x.dtype) for x in xs],
        in_specs=[_HBM] * n_arr,
        out_specs=[_HBM] * n_arr,
        scratch_shapes=[pltpu.SemaphoreType.DMA((n_arr * N_CHIP,)), pltpu.SemaphoreType.DMA((n_arr * N_CHIP,))],
    )(*xs)


def _exchange_chips(xs, *, name):
    n_arr = len(xs)
    per = N_CHIP - 1

    def body(*refs):
        x_refs, out_refs = refs[:n_arr], refs[n_arr:2 * n_arr]
        send_sems, recv_sems, local_sems = refs[2 * n_arr:]
        mx, my, mc = lax.axis_index("x"), lax.axis_index("y"), lax.axis_index("c")
        my_chip = 2 * mx + my
        copies = []
        for a in range(n_arr):
            mine = pltpu.make_async_copy(x_refs[a].at[my_chip], out_refs[a].at[my_chip], local_sems.at[a])
            mine.start()
            copies.append(mine)
            for j, (px, py) in enumerate([(1 - mx, my), (mx, 1 - my), (1 - mx, 1 - my)]):
                cp = pltpu.make_async_remote_copy(
                    src_ref=x_refs[a].at[2 * px + py], dst_ref=out_refs[a].at[my_chip],
                    send_sem=send_sems.at[a * per + j], recv_sem=recv_sems.at[a * per + j],
                    device_id=(px, py, mc), device_id_type=MESH)
                cp.start()
                copies.append(cp)
        for cp in copies:
            cp.wait()

    return pl.pallas_call(
        body, name=name,
        out_shape=[jax.ShapeDtypeStruct(x.shape, x.dtype) for x in xs],
        in_specs=[_HBM] * n_arr,
        out_specs=[_HBM] * n_arr,
        scratch_shapes=[pltpu.SemaphoreType.DMA((n_arr * per,)), pltpu.SemaphoreType.DMA((n_arr * per,)),
                        pltpu.SemaphoreType.DMA((n_arr,))],
    )(*xs)


def _add_pair(a, b, *, name):
    n, r, c = a.shape
    tr = _pick(r, (512, 256, 128, 64, 32, 16))

    def body(a_ref, b_ref, o_ref):
        o_ref[...] = (a_ref[...].astype(F32) + b_ref[...].astype(F32)).astype(o_ref.dtype)

    spec = pl.BlockSpec((1, tr, c), lambda q, i: (q, i, 0))
    return pl.pallas_call(
        body, name=name,
        grid=(n, r // tr),
        in_specs=[spec, spec],
        out_specs=spec,
        out_shape=jax.ShapeDtypeStruct(a.shape, a.dtype),
        compiler_params=_cparams(("parallel", "parallel")),
    )(a, b)


def _sum_slots(x, *, name):
    n, r, c = x.shape
    tr = _pick(r, (256, 128, 64, 32, 16, 8))

    def body(x_ref, o_ref):
        acc = x_ref[0].astype(F32)
        for s in range(1, n):
            acc = acc + x_ref[s].astype(F32)
        o_ref[...] = acc

    return pl.pallas_call(
        body, name=name,
        grid=(r // tr,),
        in_specs=[pl.BlockSpec((n, tr, c), lambda i: (0, i, 0))],
        out_specs=pl.BlockSpec((tr, c), lambda i: (i, 0)),
        out_shape=jax.ShapeDtypeStruct((r, c), F32),
        compiler_params=_cparams(("parallel",)),
    )(x)


def _adamw(w, g, m, v, *, name):
    shape = w.shape
    c = shape[-1]
    r = w.size // c
    w2, g2, m2, v2 = (t.reshape(r, c) for t in (w, g, m, v))
    tr = _pick(r, (128, 64, 32, 16, 8))

    def body(w_ref, g_ref, m_ref, v_ref, d_ref, mo_ref, vo_ref):
        gv = g_ref[...]
        mn = ADAM_B1 * m_ref[...] + (1.0 - ADAM_B1) * gv
        vn = ADAM_B2 * v_ref[...] + (1.0 - ADAM_B2) * (gv * gv)
        m_hat = mn / (1.0 - ADAM_B1 ** ADAM_STEP)
        v_hat = vn / (1.0 - ADAM_B2 ** ADAM_STEP)
        d_ref[...] = -ADAM_LR * (m_hat / (jnp.sqrt(v_hat) + ADAM_EPS) + ADAM_WD * w_ref[...])
        mo_ref[...] = mn
        vo_ref[...] = vn

    spec = pl.BlockSpec((tr, c), lambda i: (i, 0))
    outs = pl.pallas_call(
        body, name=name,
        grid=(r // tr,),
        in_specs=[spec] * 4,
        out_specs=[spec] * 3,
        out_shape=[jax.ShapeDtypeStruct((r, c), F32)] * 3,
        compiler_params=_cparams(("parallel",)),
    )(w2, g2, m2, v2)
    return tuple(o.reshape(shape) for o in outs)


GDN_HEADS_PER_STEP = 8
_TN = (((0,), (0,)), ((), ()))


def _gdn_scan_specs(qd, u, attn, reverse):
    h, n, c, dk = qd.shape
    dv = u.shape[-1]
    g = GDN_HEADS_PER_STEP
    idx = (lambda hb, i: (hb, n - 1 - i, 0, 0)) if reverse else (lambda hb, i: (hb, i, 0, 0))
    return dict(
        qk=pl.BlockSpec((g, 1, c, dk), idx), uv=pl.BlockSpec((g, 1, c, dv), idx),
        attn=pl.BlockSpec((g, 1, c, c), idx), lane=pl.BlockSpec((g, 1, 1, dv), idx),
        state=pl.BlockSpec((g, 1, dk, dv), idx), grid=(h // g, n))


def _gdn_scan_fwd_call(qd, kd, u, w, attn, egl):
    h, n, c, dk = qd.shape
    dv = u.shape[-1]
    sp = _gdn_scan_specs(qd, u, attn, reverse=False)

    def body(q_ref, k_ref, u_ref, w_ref, a_ref, e_ref, o_ref, st_ref, s_sc):
        @pl.when(pl.program_id(1) == 0)
        def _():
            s_sc[...] = jnp.zeros_like(s_sc)

        for g in range(GDN_HEADS_PER_STEP):
            s = s_sc[g]
            st_ref[g, 0] = s
            sb = s.astype(BF16)
            v_new = u_ref[g, 0] - jnp.dot(w_ref[g, 0].astype(BF16), sb, preferred_element_type=F32)
            vb = v_new.astype(BF16)
            o_ref[g, 0] = (jnp.dot(q_ref[g, 0].astype(BF16), sb, preferred_element_type=F32)
                           + jnp.dot(a_ref[g, 0].astype(BF16), vb, preferred_element_type=F32))
            s_sc[g] = s * e_ref[g, 0] + lax.dot_general(k_ref[g, 0].astype(BF16), vb, _TN,
                                                        preferred_element_type=F32)

    return pl.pallas_call(
        body, name="gdn_scan_fwd",
        grid=sp["grid"],
        in_specs=[sp["qk"], sp["qk"], sp["uv"], sp["qk"], sp["attn"], sp["lane"]],
        out_specs=[sp["uv"], sp["state"]],
        out_shape=[jax.ShapeDtypeStruct((h, n, c, dv), F32), jax.ShapeDtypeStruct((h, n, dk, dv), F32)],
        scratch_shapes=[pltpu.VMEM((GDN_HEADS_PER_STEP, dk, dv), F32)],
        compiler_params=_cparams(("parallel", "arbitrary")),
    )(qd, kd, u, w, attn, egl)


def _gdn_scan_bwd_call(qd, kd, u, w, attn, egl, states, do):
    h, n, c, dk = qd.shape
    dv = u.shape[-1]
    sp = _gdn_scan_specs(qd, u, attn, reverse=True)

    def body(q_ref, k_ref, u_ref, w_ref, a_ref, e_ref, st_ref, do_ref,
             dq_ref, dk_ref, du_ref, dw_ref, da_ref, de_ref, ds_sc):
        @pl.when(pl.program_id(1) == 0)
        def _():
            ds_sc[...] = jnp.zeros_like(ds_sc)

        for g in range(GDN_HEADS_PER_STEP):
            s = st_ref[g, 0]
            sb = s.astype(BF16)
            ds = ds_sc[g]
            dsb = ds.astype(BF16)
            qb, kb, wb, ab = (r[g, 0].astype(BF16) for r in (q_ref, k_ref, w_ref, a_ref))
            dob = do_ref[g, 0].astype(BF16)
            vb = (u_ref[g, 0] - jnp.dot(wb, sb, preferred_element_type=F32)).astype(BF16)
            dv_new = (lax.dot_general(ab, dob, _TN, preferred_element_type=F32)
                      + jnp.dot(kb, dsb, preferred_element_type=F32))
            dvb = dv_new.astype(BF16)
            dq_ref[g, 0] = lax.dot_general(dob, sb, _NT, preferred_element_type=F32)
            da_ref[g, 0] = lax.dot_general(dob, vb, _NT, preferred_element_type=F32)
            dk_ref[g, 0] = lax.dot_general(vb, dsb, _NT, preferred_element_type=F32)
            du_ref[g, 0] = dv_new
            dw_ref[g, 0] = -lax.dot_general(dvb, sb, _NT, preferred_element_type=F32)
            de_ref[g, 0] = jnp.sum(ds * s, axis=0, keepdims=True)
            ds_sc[g] = (ds * e_ref[g, 0] + lax.dot_general(qb, dob, _TN, preferred_element_type=F32)
                        - lax.dot_general(wb, dvb, _TN, preferred_element_type=F32))

    return pl.pallas_call(
        body, name="gdn_scan_bwd",
        grid=sp["grid"],
        in_specs=[sp["qk"], sp["qk"], sp["uv"], sp["qk"], sp["attn"], sp["lane"], sp["state"], sp["uv"]],
        out_specs=[sp["qk"], sp["qk"], sp["uv"], sp["qk"], sp["attn"], sp["lane"]],
        out_shape=[jax.ShapeDtypeStruct((h, n, c, dk), F32), jax.ShapeDtypeStruct((h, n, c, dk), F32),
                   jax.ShapeDtypeStruct((h, n, c, dv), F32), jax.ShapeDtypeStruct((h, n, c, dk), F32),
                   jax.ShapeDtypeStruct((h, n, c, c), F32), jax.ShapeDtypeStruct((h, n, 1, dv), F32)],
        scratch_shapes=[pltpu.VMEM((GDN_HEADS_PER_STEP, dk, dv), F32)],
        compiler_params=_cparams(("parallel", "arbitrary")),
    )(qd, kd, u, w, attn, egl, states, do)


GDN_CHUNKS_PER_STEP = 16
GDN_DIAG = 16


def _tri_solve_fwd_call(a, rhs):
    h, n, c, _ = a.shape
    r = rhs.shape[-1]
    g_n = GDN_CHUNKS_PER_STEP

    batched = (((2,), (1,)), ((0,), (0,)))

    def dot(x, y):
        return lax.dot_general(x.astype(BF16), y.astype(BF16), batched, preferred_element_type=F32)

    def body(a_ref, rhs_ref, sol_ref, t_ref):
        ri = lax.broadcasted_iota(jnp.int32, (g_n, c, c), 1)
        ci = lax.broadcasted_iota(jnp.int32, (g_n, c, c), 2)
        eye = (ri == ci).astype(F32)
        av = a_ref[0]
        d1 = jnp.where(ri // GDN_DIAG == ci // GDN_DIAG, av, 0.0)
        d2 = dot(d1, d1)
        d4 = dot(d2, d2)
        d8 = dot(d4, d4)
        t_inv = dot(dot(eye - d1, eye + d2), dot(eye + d4, eye + d8))
        size = GDN_DIAG
        while size < c:
            below = jnp.where((ri // (2 * size) == ci // (2 * size)) & (ri // size != ci // size), av, 0.0)
            t_inv = t_inv - dot(t_inv, dot(below, t_inv))
            size *= 2
        t_ref[0] = t_inv
        sol_ref[0] = dot(t_inv, rhs_ref[0])

    a_spec = pl.BlockSpec((1, g_n, c, c), lambda i, j: (i, j, 0, 0))
    r_spec = pl.BlockSpec((1, g_n, c, r), lambda i, j: (i, j, 0, 0))
    return pl.pallas_call(
        body, name="gdn_tri_solve",
        grid=(h, n // g_n),
        in_specs=[a_spec, r_spec],
        out_specs=[r_spec, a_spec],
        out_shape=[jax.ShapeDtypeStruct(rhs.shape, F32), jax.ShapeDtypeStruct(a.shape, F32)],
        compiler_params=_cparams(("parallel", "parallel")),
    )(a, rhs)


def _tri_solve_bwd_call(t_inv, sol, dsol):
    h, n, c, _ = t_inv.shape
    r = sol.shape[-1]
    g_n = GDN_CHUNKS_PER_STEP

    def body(t_ref, sol_ref, dsol_ref, drhs_ref, da_ref):
        for g in range(g_n):
            d_rhs = lax.dot_general(t_ref[0, g].astype(BF16), dsol_ref[0, g].astype(BF16), _TN,
                                    preferred_element_type=F32)
            drhs_ref[0, g] = d_rhs
            da_ref[0, g] = -lax.dot_general(d_rhs.astype(BF16), sol_ref[0, g].astype(BF16), _NT,
                                            preferred_element_type=F32)

    a_spec = pl.BlockSpec((1, g_n, c, c), lambda i, j: (i, j, 0, 0))
    r_spec = pl.BlockSpec((1, g_n, c, r), lambda i, j: (i, j, 0, 0))
    return pl.pallas_call(
        body, name="gdn_tri_solve_bwd",
        grid=(h, n // g_n),
        in_specs=[a_spec, r_spec, r_spec],
        out_specs=[r_spec, a_spec],
        out_shape=[jax.ShapeDtypeStruct(sol.shape, F32), jax.ShapeDtypeStruct(t_inv.shape, F32)],
        compiler_params=_cparams(("parallel", "parallel")),
    )(t_inv, sol, dsol)


@jax.custom_vjp
def _tri_solve(a, rhs):
    return _tri_solve_fwd_call(a, rhs)[0]


def _tri_solve_fwd(a, rhs):
    sol, t_inv = _tri_solve_fwd_call(a, rhs)
    return sol, (t_inv, sol)


def _tri_solve_bwd(res, dsol):
    t_inv, sol = res
    d_rhs, d_a = _tri_solve_bwd_call(t_inv, sol, dsol)
    return d_a, d_rhs


_tri_solve.defvjp(_tri_solve_fwd, _tri_solve_bwd)


@jax.custom_vjp
def _gdn_scan(qd, kd, u, w, attn, g_last):
    return _gdn_scan_fwd(qd, kd, u, w, attn, g_last)[0]


def _gdn_scan_fwd(qd, kd, u, w, attn, g_last):
    egl = jnp.broadcast_to(jnp.exp(g_last)[:, :, None, None], g_last.shape + (1, u.shape[-1]))
    o, states = _gdn_scan_fwd_call(qd, kd, u, w, attn, egl)
    return o, (qd, kd, u, w, attn, egl, states)


def _gdn_scan_bwd(res, do):
    qd, kd, u, w, attn, egl, states = res
    dq, dk, du, dw, da, de = _gdn_scan_bwd_call(qd, kd, u, w, attn, egl, states, do)
    dgl = jnp.sum(de[:, :, 0, :], axis=-1) * egl[:, :, 0, 0]
    return dq, dk, du, dw, da, dgl


_gdn_scan.defvjp(_gdn_scan_fwd, _gdn_scan_bwd)


@functools.partial(jax.custom_vjp, nondiff_argnums=(1,))
def _split_cols(x, sizes):
    bounds = [sum(sizes[:i]) for i in range(len(sizes) + 1)]
    return tuple(x[:, bounds[i]:bounds[i + 1]] for i in range(len(sizes)))


def _split_cols_fwd(x, sizes):
    return _split_cols(x, sizes), None


def _split_cols_bwd(sizes, _, parts):
    return (jnp.concatenate(parts, axis=1),)


_split_cols.defvjp(_split_cols_fwd, _split_cols_bwd)


def _rms_norm(x, w):
    y = x * lax.rsqrt(jnp.mean(x * x, axis=-1, keepdims=True) + NORM_EPS)
    return y * w


def _modulate(h, shift, scale):
    return h * (1 + scale) + shift


def _l2_normalize(x):
    return x * lax.rsqrt(jnp.sum(x * x, axis=-1, keepdims=True) + NORM_EPS)


def _causal_conv(x, w):
    width = w.shape[0]
    length = x.shape[0]
    xp = jnp.pad(x, ((width - 1, 0), (0, 0)))
    return sum(xp[i:i + length] * w[i] for i in range(width))


def _gated_delta_rule_chunked(q, k, v, g, beta):
    h, length, dk = q.shape
    dv = v.shape[-1]
    n = length // GDN_CHUNK
    blk = lambda t: t.reshape(h, n, GDN_CHUNK, *t.shape[2:])
    q = blk(q) * dk ** -0.5
    k = blk(k)
    v = blk(v)
    beta = blk(beta)
    g = jnp.cumsum(blk(g), axis=-1)
    causal = jnp.tril(jnp.ones((GDN_CHUNK, GDN_CHUNK), dtype=bool))
    strict = jnp.tril(jnp.ones((GDN_CHUNK, GDN_CHUNK), dtype=bool), k=-1)
    decay = jnp.exp(jnp.where(causal, g[..., :, None] - g[..., None, :], -jnp.inf))
    k_beta = k * beta[..., None]
    a_strict = jnp.where(strict, jnp.einsum('hnid,hnjd->hnij', k_beta, k) * decay, 0.0)
    rhs = jnp.concatenate([v * beta[..., None], k_beta * jnp.exp(g)[..., None]], axis=-1)
    sol = _tri_solve(a_strict, rhs)
    u, w = sol[..., :dv], sol[..., dv:]
    attn = jnp.where(causal, jnp.einsum('hnid,hnjd->hnij', q, k) * decay, 0.0)
    g_last = g[..., -1]
    q_dec = q * jnp.exp(g)[..., None]
    k_dec = k * jnp.exp(g_last[..., None] - g)[..., None]

    return _gdn_scan(q_dec, k_dec, u, w, attn, g_last).reshape(h, length, dv)


_MM = {tag: _make_mm(tag) for tag in ("gdn_in", "gdn_ba", "gdn_out", "ffn_out0", "kv", "kv_f", "fox_out", "ffn_out1")}
_MM.update({tag: _make_mm(tag, dev_blocks=True) for tag in ("ffn_in0", "ffn_in1", "fox_in")})


def _swiglu(h, w_in, w_out, c_in, c_out, layer):
    gu = _MM[f"ffn_in{layer}"](h, w_in, c_in)
    gate, up = _split_cols(gu, (FFN_HIDDEN, FFN_HIDDEN))
    return _MM[f"ffn_out{layer}"](jax.nn.silu(gate) * up, w_out, c_out)


def _gated_deltanet(h, wts, car, conv_w, a_log, dt_bias, norm_w):
    length = h.shape[0]
    main = _MM["gdn_in"](h, wts["gdn_in"], car["gdn_in"])
    ba = _MM["gdn_ba"](h, wts["gdn_ba"], car["gdn_ba"])
    qkv, z = _split_cols(main, (CONV_DIM, GDN_V_DIM))
    beta_logit, a = ba[:, :GDN_V_HEADS], ba[:, GDN_V_HEADS:2 * GDN_V_HEADS]
    qkv = jax.nn.silu(_causal_conv(qkv, conv_w))
    q, k, v = _split_cols(qkv, (GDN_QK_DIM, GDN_QK_DIM, GDN_V_DIM))
    rep = GDN_V_HEADS // GDN_QK_HEADS
    heads = lambda t, nh: t.reshape(length, nh, GDN_HEAD_DIM)
    q = jnp.repeat(_l2_normalize(heads(q, GDN_QK_HEADS)), rep, axis=1)
    k = jnp.repeat(_l2_normalize(heads(k, GDN_QK_HEADS)), rep, axis=1)
    v = heads(v, GDN_V_HEADS)
    beta = jax.nn.sigmoid(beta_logit)
    g = -jnp.exp(a_log) * jax.nn.softplus(a + dt_bias)
    tr = lambda t: jnp.swapaxes(t, 0, 1)
    o = _gated_delta_rule_chunked(tr(q), tr(k), tr(v), tr(g), tr(beta))
    o = _rms_norm(tr(o), norm_w) * jax.nn.silu(heads(z, GDN_V_HEADS))
    return _MM["gdn_out"](o.reshape(length, GDN_V_DIM), wts["gdn_out"], car["gdn_out"])


def _local_loss(x, mods, small, car, wts, target):
    split6 = lambda m: tuple(m[i * D_MODEL:(i + 1) * D_MODEL][None, :] for i in range(6))
    sh_m, sc_m, g_m, sh_f, sc_f, g_f = split6(mods["ada0"])
    h = _modulate(_rms_norm(x, small["norm_mix"][0]), sh_m, sc_m)
    y = _gated_deltanet(h, wts, car, small["gdn_conv"], small["gdn_a_log"][0], small["gdn_dt_bias"][0],
                        small["gdn_norm"][0])
    x = x + g_m * y
    h = _modulate(_rms_norm(x, small["norm_ffn"][0]), sh_f, sc_f)
    x = x + g_f * _swiglu(h, wts["ffn_in0"], wts["ffn_out0"], car["ffn_in0"], car["ffn_out0"], 0)
    length = x.shape[0]
    kv_shift, kv_scale = mods["kvada"][None, :D_MODEL], mods["kvada"][None, D_MODEL:]
    hk = _modulate(_rms_norm(x, small["kv_norm"]), kv_shift, kv_scale)
    kv = _MM["kv"](hk, wts["kv"], car["kv"])
    f_logit = _MM["kv_f"](hk, wts["kv_f"], car["kv_f"])[:, :FOX_HEADS]
    k_raw, v_sh = _split_cols(kv, (FOX_KV_DIM, FOX_KV_DIM))
    k_sh = _rms_norm(k_raw.reshape(length, FOX_KV_HEADS, FOX_HEAD_DIM), small["k_norm"])
    k_sh = k_sh.reshape(length, FOX_KV_DIM)
    f_cum = jnp.cumsum(jax.nn.log_sigmoid(f_logit + small["forget_b"]), axis=0)
    sh_m, sc_m, g_m, sh_f, sc_f, g_f = split6(mods["ada1"])
    h = _modulate(_rms_norm(x, small["norm_mix"][1]), sh_m, sc_m)
    qg = _MM["fox_in"](h, wts["fox_in"], car["fox_in"])
    q_raw, q_gate = _split_cols(qg, (FOX_Q_DIM, FOX_Q_DIM))
    q = _rms_norm(q_raw.reshape(length, FOX_HEADS, FOX_HEAD_DIM), small["q_norm"][0])
    o = _fox_attention(q.reshape(length, FOX_Q_DIM), k_sh, v_sh, f_cum)
    y = _MM["fox_out"](o * jax.nn.sigmoid(q_gate), wts["fox_out"], car["fox_out"])
    x = x + g_m * y
    h = _modulate(_rms_norm(x, small["norm_ffn"][1]), sh_f, sc_f)
    x = x + g_f * _swiglu(h, wts["ffn_in1"], wts["ffn_out1"], car["ffn_in1"], car["ffn_out1"], 1)
    out = _modulate(_rms_norm(x, small["out_norm"]), mods["outada"][None, :D_MODEL], mods["outada"][None, D_MODEL:])
    err = jnp.square(out - target)
    return 0.5 * jnp.sum(jnp.mean(err, axis=-1))


def _pad_rows(a, rows):
    return jnp.pad(a, ((0, rows - a.shape[0]), (0, 0)))


def _pack_flat(parts, total):
    flat = jnp.concatenate([p.reshape(-1) for p in parts])
    return jnp.pad(flat, (0, total - flat.shape[0]))


_SMALL_NAMES = ("ada_b", "norm_mix", "norm_ffn", "gdn_a_log", "gdn_dt_bias", "gdn_norm", "kv_ada_b", "kv_norm",
                "k_norm", "forget_b", "q_norm", "out_ada_b", "out_norm")


def kernel(x, c, ada_w, ada_b, norm_mix, norm_ffn, ffn_w_in, ffn_w_out, gdn_w_in, gdn_conv, gdn_a_log, gdn_dt_bias, gdn_norm, gdn_w_out, kv_ada_w, kv_ada_b, kv_norm, kv_w, k_norm, forget_b, fox_w_in, q_norm, fox_w_out, out_ada_w, out_ada_b, out_norm, loss_target, m_ada_w, m_ada_b, m_norm_mix, m_norm_ffn, m_ffn_w_in, m_ffn_w_out, m_gdn_w_in, m_gdn_conv, m_gdn_a_log, m_gdn_dt_bias, m_gdn_norm, m_gdn_w_out, m_kv_ada_w, m_kv_ada_b, m_kv_norm, m_kv_w, m_k_norm, m_forget_b, m_fox_w_in, m_q_norm, m_fox_w_out, m_out_ada_w, m_out_ada_b, m_out_norm, v_ada_w, v_ada_b, v_norm_mix, v_norm_ffn, v_ffn_w_in, v_ffn_w_out, v_gdn_w_in, v_gdn_conv, v_gdn_a_log, v_gdn_dt_bias, v_gdn_norm, v_gdn_w_out, v_kv_ada_w, v_kv_ada_b, v_kv_norm, v_kv_w, v_k_norm, v_forget_b, v_fox_w_in, v_q_norm, v_fox_w_out, v_out_ada_w, v_out_ada_b, v_out_norm):
    w_in = dict(ada_w=ada_w, ada_b=ada_b, norm_mix=norm_mix, norm_ffn=norm_ffn, ffn_w_in=ffn_w_in, ffn_w_out=ffn_w_out, gdn_w_in=gdn_w_in, gdn_conv=gdn_conv, gdn_a_log=gdn_a_log, gdn_dt_bias=gdn_dt_bias, gdn_norm=gdn_norm, gdn_w_out=gdn_w_out, kv_ada_w=kv_ada_w, kv_ada_b=kv_ada_b, kv_norm=kv_norm, kv_w=kv_w, k_norm=k_norm, forget_b=forget_b, fox_w_in=fox_w_in, q_norm=q_norm, fox_w_out=fox_w_out, out_ada_w=out_ada_w, out_ada_b=out_ada_b, out_norm=out_norm)
    m_in = dict(ada_w=m_ada_w, ada_b=m_ada_b, norm_mix=m_norm_mix, norm_ffn=m_norm_ffn, ffn_w_in=m_ffn_w_in, ffn_w_out=m_ffn_w_out, gdn_w_in=m_gdn_w_in, gdn_conv=m_gdn_conv, gdn_a_log=m_gdn_a_log, gdn_dt_bias=m_gdn_dt_bias, gdn_norm=m_gdn_norm, gdn_w_out=m_gdn_w_out, kv_ada_w=m_kv_ada_w, kv_ada_b=m_kv_ada_b, kv_norm=m_kv_norm, kv_w=m_kv_w, k_norm=m_k_norm, forget_b=m_forget_b, fox_w_in=m_fox_w_in, q_norm=m_q_norm, fox_w_out=m_fox_w_out, out_ada_w=m_out_ada_w, out_ada_b=m_out_ada_b, out_norm=m_out_norm)
    v_in = dict(ada_w=v_ada_w, ada_b=v_ada_b, norm_mix=v_norm_mix, norm_ffn=v_norm_ffn, ffn_w_in=v_ffn_w_in, ffn_w_out=v_ffn_w_out, gdn_w_in=v_gdn_w_in, gdn_conv=v_gdn_conv, gdn_a_log=v_gdn_a_log, gdn_dt_bias=v_gdn_dt_bias, gdn_norm=v_gdn_norm, gdn_w_out=v_gdn_w_out, kv_ada_w=v_kv_ada_w, kv_ada_b=v_kv_ada_b, kv_norm=v_kv_norm, kv_w=v_kv_w, k_norm=v_k_norm, forget_b=v_forget_b, fox_w_in=v_fox_w_in, q_norm=v_q_norm, fox_w_out=v_fox_w_out, out_ada_w=v_out_ada_w, out_ada_b=v_out_ada_b, out_norm=v_out_norm)
    names = list(w_in)
    me = 4 * lax.axis_index("x") + 2 * lax.axis_index("y") + lax.axis_index("c")
    d = D_MODEL

    n_conv = CONV_K * (CONV_DIM // N_DEV)
    first = _pack_flat([c, gdn_conv], d + n_conv).reshape(-1, LANES)
    first_all = _all_gather(first, name="ag_cond_conv", in_vmem=True).reshape(N_DEV, d + n_conv)
    cond_all = jax.nn.silu(first_all[:, :d])
    conv_full = first_all[:, d:].reshape(N_DEV, CONV_K, CONV_DIM // N_DEV)
    conv_full = jnp.transpose(conv_full, (1, 0, 2)).reshape(CONV_K, CONV_DIM)

    ada_cat = jnp.concatenate([ada_w[0], ada_w[1], kv_ada_w, out_ada_w], axis=1)
    n_ada = ada_cat.shape[1]
    cond_pad = _pad_rows(cond_all, LANES)
    mods_part = _matmul(cond_pad, ada_cat, name="mm_ada")[:N_DEV]
    mods_all = _all_gather(mods_part.reshape(-1, LANES), name="ag_mods", in_vmem=True)
    mods_all = mods_all.reshape(N_DEV, N_DEV, n_ada)
    mine = lax.dynamic_index_in_dim(mods_all, me, axis=1, keepdims=False)
    s6 = 6 * d // N_DEV
    s2 = 2 * d // N_DEV
    mods = {
        "ada0": mine[:, :s6].reshape(-1) + ada_b[0],
        "ada1": mine[:, s6:2 * s6].reshape(-1) + ada_b[1],
        "kvada": mine[:, 2 * s6:2 * s6 + s2].reshape(-1) + kv_ada_b,
        "outada": mine[:, 2 * s6 + s2:].reshape(-1) + out_ada_b,
    }

    big = [ffn_w_in[0], ffn_w_in[1], ffn_w_out[0], ffn_w_out[1], gdn_w_in[0], gdn_w_out[0], kv_w, fox_w_in[0],
           fox_w_out[0]]
    g_ffn_in0, g_ffn_in1, g_ffn_out0, g_ffn_out1, g_gdn_in, g_gdn_out, g_kv, g_fox_in, g_fox_out = (
        _all_gather_by_chip([b.astype(BF16) for b in big], name="ag_weights"))
    rows = lambda t: t.reshape(N_DEV * t.shape[1], t.shape[2])
    per_gdn = GDN_PROJ // N_DEV
    last_main = GDN_MAIN - (N_DEV - 1) * per_gdn
    gdn_main = jnp.concatenate([g_gdn_in[dv] for dv in range(N_DEV - 1)] + [g_gdn_in[N_DEV - 1][:, :last_main]],
                               axis=1)
    gdn_ba = jnp.pad(g_gdn_in[N_DEV - 1][:, last_main:], ((0, 0), (0, LANES - 2 * GDN_V_HEADS)))
    kv_full = rows(g_kv)
    wts = {
        "ffn_in0": g_ffn_in0, "ffn_in1": g_ffn_in1, "ffn_out0": rows(g_ffn_out0), "ffn_out1": rows(g_ffn_out1),
        "gdn_in": gdn_main, "gdn_ba": gdn_ba, "gdn_out": rows(g_gdn_out),
        "kv": kv_full[:, :2 * FOX_KV_DIM],
        "kv_f": jnp.pad(kv_full[:, 2 * FOX_KV_DIM:], ((0, 0), (0, LANES - FOX_HEADS))),
        "fox_in": g_fox_in, "fox_out": rows(g_fox_out),
    }
    car = {k: jnp.zeros(w.shape, BF16) for k, w in wts.items()}
    small = {n: w_in[n] for n in _SMALL_NAMES if n not in ("ada_b", "kv_ada_b", "out_ada_b")}
    small["gdn_conv"] = conv_full

    loss_fn = functools.partial(_local_loss, wts=wts, target=loss_target[0])
    loss_local, vjp = jax.vjp(loss_fn, x[0], mods, small, car)
    gx, gmods, gsmall, gcar = vjp(jnp.ones((), F32))
    loss = lax.psum(loss_local, ("x", "y", "c"))

    gsm = dict(gsmall)
    gsm["ada_b"] = jnp.stack([gmods["ada0"], gmods["ada1"]])
    gsm["kv_ada_b"] = gmods["kvada"]
    gsm["out_ada_b"] = gmods["outada"]
    small_sizes = [w_in[n].size for n in _SMALL_NAMES]
    n_small = sum(small_sizes) + CONV_K * CONV_DIM
    small_rows = -(-n_small // (LANES * 8)) * 8
    vec = _pack_flat([gsm[n] for n in _SMALL_NAMES] + [gsm["gdn_conv"]], small_rows * LANES)
    vec_all = _all_gather(vec.reshape(small_rows, LANES), name="ag_small_grads", in_vmem=True)
    vec_sum = _sum_slots(vec_all, name="sum_small_grads").reshape(-1)
    grads = {}
    off = 0
    for n, sz in zip(_SMALL_NAMES, small_sizes):
        grads[n] = vec_sum[off:off + sz].reshape(w_in[n].shape)
        off += sz
    conv_sum = vec_sum[off:off + CONV_K * CONV_DIM].reshape(CONV_K, N_DEV, CONV_DIM // N_DEV)
    grads["gdn_conv"] = lax.dynamic_index_in_dim(conv_sum, me, axis=1, keepdims=False)[None]

    vec_flat = vec_all.reshape(N_DEV, -1)
    d_ada = vec_flat[:, :2 * 6 * d].reshape(N_DEV, 2, N_DEV, s6)
    o_kv = sum(small_sizes[:6])
    d_kvada = vec_flat[:, o_kv:o_kv + 2 * d].reshape(N_DEV, N_DEV, s2)
    o_out = sum(small_sizes[:11])
    d_outada = vec_flat[:, o_out:o_out + 2 * d].reshape(N_DEV, N_DEV, s2)
    pick = lambda t, axis: lax.dynamic_index_in_dim(t, me, axis=axis, keepdims=False)
    d_cat = jnp.concatenate([pick(d_ada[:, 0], 1), pick(d_ada[:, 1], 1), pick(d_kvada, 1), pick(d_outada, 1)],
                            axis=1)
    g_ada_cat = _matmul(cond_pad, _pad_rows(d_cat, LANES), trans_a=True, name="mm_ada_dw")
    grads["ada_w"] = jnp.stack([g_ada_cat[:, :s6], g_ada_cat[:, s6:2 * s6]])
    grads["kv_ada_w"] = g_ada_cat[:, 2 * s6:2 * s6 + s2]
    grads["out_ada_w"] = g_ada_cat[:, 2 * s6 + s2:]

    d_main, d_ba = gcar["gdn_in"], gcar["gdn_ba"]
    d_gdn_in = jnp.stack(
        [d_main[:, dv * per_gdn:(dv + 1) * per_gdn] for dv in range(N_DEV - 1)]
        + [jnp.concatenate([d_main[:, (N_DEV - 1) * per_gdn:], d_ba[:, :2 * GDN_V_HEADS]], axis=1)])
    d_kv = jnp.concatenate([gcar["kv"], gcar["kv_f"][:, :FOX_HEADS]], axis=1)
    owner_rows = lambda t: t.reshape(N_DEV, t.shape[0] // N_DEV, t.shape[1])
    by_owner = [gcar["ffn_in0"], gcar["ffn_in1"], owner_rows(gcar["ffn_out0"]), owner_rows(gcar["ffn_out1"]),
                d_gdn_in, owner_rows(gcar["gdn_out"]), owner_rows(d_kv), gcar["fox_in"],
                owner_rows(gcar["fox_out"])]
    by_chip_core = [t.reshape((N_CHIP, 2) + t.shape[1:]) for t in by_owner]
    my_c = lax.axis_index("c")
    from_sibling = _exchange_sibling(by_chip_core, name="rs_sibling")
    shard_grads = []
    pair_sums = []
    for idx, (t, got) in enumerate(zip(by_chip_core, from_sibling)):
        for_my_core = lax.dynamic_index_in_dim(t, my_c, axis=1, keepdims=False)
        pair_sums.append(_add_pair(for_my_core, got, name=f"rs_pair_sum{idx}"))
    by_chip = _exchange_chips(pair_sums, name="rs_chips")
    for idx, t in enumerate(by_chip):
        shard_grads.append(_sum_slots(t, name=f"sum_grads{idx}"))
    grads["ffn_w_in"] = jnp.stack(shard_grads[0:2])
    grads["ffn_w_out"] = jnp.stack(shard_grads[2:4])
    grads["gdn_w_in"] = shard_grads[4][None]
    grads["gdn_w_out"] = shard_grads[5][None]
    grads["kv_w"] = shard_grads[6]
    grads["fox_w_in"] = shard_grads[7][None]
    grads["fox_w_out"] = shard_grads[8][None]

    delta, new_m, new_v = {}, {}, {}
    for n in names:
        if n in _SMALL_NAMES:
            continue
        delta[n], new_m[n], new_v[n] = _adamw(w_in[n], grads[n], m_in[n], v_in[n], name=f"adamw_{n}")
    sm_rows = -(-sum(small_sizes) // (LANES * 8)) * 8
    pk = lambda src: _pack_flat([src[n] for n in _SMALL_NAMES], sm_rows * LANES).reshape(sm_rows, LANES)
    d_pk, m_pk, vn_pk = _adamw(pk(w_in), pk(grads), pk(m_in), pk(v_in), name="adamw_small")
    off = 0
    for n, sz in zip(_SMALL_NAMES, small_sizes):
        shp = w_in[n].shape
        delta[n] = d_pk.reshape(-1)[off:off + sz].reshape(shp)
        new_m[n] = m_pk.reshape(-1)[off:off + sz].reshape(shp)
        new_v[n] = vn_pk.reshape(-1)[off:off + sz].reshape(shp)
        off += sz

    return (loss, gx[None], *[grads[n] for n in names], *[delta[n] for n in names],
            *[new_m[n] for n in names], *[new_v[n] for n in names])
```

```python
import functools

import jax
import jax.numpy as jnp
from jax import lax
from jax.experimental import pallas as pl
from jax.experimental.pallas import tpu as pltpu

F32 = jnp.float32
BF16 = jnp.bfloat16
MESH = pl.DeviceIdType.MESH
N_DEV = 8

D_MODEL = 2048
SEQ = 8192
GDN_QK_HEADS = 16
GDN_V_HEADS = 32
GDN_HEAD_DIM = 128
GDN_QK_DIM = GDN_QK_HEADS * GDN_HEAD_DIM
GDN_V_DIM = GDN_V_HEADS * GDN_HEAD_DIM
CONV_DIM = 2 * GDN_QK_DIM + GDN_V_DIM
GDN_MAIN = CONV_DIM + GDN_V_DIM
GDN_PROJ = GDN_MAIN + 2 * GDN_V_HEADS
CONV_K = 4
GDN_CHUNK = 64
FOX_HEADS = 16
FOX_KV_HEADS = 2
FOX_GROUP = FOX_HEADS // FOX_KV_HEADS
FOX_HEAD_DIM = 256
FOX_Q_DIM = FOX_HEADS * FOX_HEAD_DIM
FOX_KV_DIM = FOX_KV_HEADS * FOX_HEAD_DIM
KV_PROJ = 2 * FOX_KV_DIM + FOX_HEADS
FFN_HIDDEN = 5632
NORM_EPS = 1e-6

ADAM_LR = 0.001
ADAM_B1 = 0.9
ADAM_B2 = 0.999
ADAM_EPS = 1e-08
ADAM_WD = 0.01
ADAM_STEP = 10

LANES = 128
VMEM_LIMIT = 56 * 1024 * 1024
PACK_COLS = 1024
ATT_BLOCK = 512
NEG = -1e30


def _pick(n, cands):
    for c in cands:
        if n % c == 0:
            return c
    return n


def _cparams(sem):
    return pltpu.CompilerParams(dimension_semantics=sem, vmem_limit_bytes=VMEM_LIMIT)


def _matmul(a, b, *, trans_a=False, trans_b=False, b_dev_blocks=False, out_dev_blocks=False, out_dtype=F32, name):
    if trans_a:
        kdim, m = a.shape
    else:
        m, kdim = a.shape
    if b_dev_blocks:
        n_dev, rows_b, per_dev = b.shape
        n, kb = (rows_b, n_dev * per_dev) if trans_b else (n_dev * per_dev, rows_b)
    elif trans_b:
        n, kb = b.shape
    else:
        kb, n = b.shape
    assert kdim == kb, (a.shape, b.shape)
    tm = _pick(m, (1024, 512, 256, 128))
    tn = _pick(n // N_DEV if (out_dev_blocks or (b_dev_blocks and not trans_b)) else n, (1408, 1024, 512, 256, 128))
    tk = _pick(kdim // N_DEV if (b_dev_blocks and trans_b) else kdim, (1024, 1408, 512, 256, 128))
    nk = kdim // tk
    dims = (((0 if trans_a else 1,), (1 if trans_b else 0,)), ((), ()))

    def body(a_ref, b_ref, o_ref, acc_ref):
        k = pl.program_id(2)

        @pl.when(k == 0)
        def _():
            acc_ref[...] = jnp.zeros_like(acc_ref)

        acc_ref[...] += lax.dot_general(a_ref[...].astype(BF16), b_ref[...].astype(BF16), dims,
                                        preferred_element_type=F32)

        @pl.when(k == nk - 1)
        def _():
            o_ref[...] = acc_ref[...].astype(o_ref.dtype)

    a_spec = (pl.BlockSpec((tk, tm), lambda i, j, k: (k, i)) if trans_a
              else pl.BlockSpec((tm, tk), lambda i, j, k: (i, k)))
    if b_dev_blocks and trans_b:
        per = kdim // N_DEV // tk
        b_spec = pl.BlockSpec((None, tn, tk), lambda i, j, k: (k // per, j, k % per))
    elif b_dev_blocks:
        per = n // N_DEV // tn
        b_spec = pl.BlockSpec((None, tk, tn), lambda i, j, k: (j // per, k, j % per))
    elif trans_b:
        b_spec = pl.BlockSpec((tn, tk), lambda i, j, k: (j, k))
    else:
        b_spec = pl.BlockSpec((tk, tn), lambda i, j, k: (k, j))
    if out_dev_blocks:
        per_o = n // N_DEV // tn
        out_spec = pl.BlockSpec((None, tm, tn), lambda i, j, k: (j // per_o, i, j % per_o))
        out_shape = jax.ShapeDtypeStruct((N_DEV, m, n // N_DEV), out_dtype)
    else:
        out_spec = pl.BlockSpec((tm, tn), lambda i, j, k: (i, j))
        out_shape = jax.ShapeDtypeStruct((m, n), out_dtype)
    return pl.pallas_call(
        body, name=name,
        grid=(m // tm, n // tn, nk),
        in_specs=[a_spec, b_spec],
        out_specs=out_spec,
        out_shape=out_shape,
        scratch_shapes=[pltpu.VMEM((tm, tn), F32)],
        compiler_params=_cparams(("parallel", "parallel", "arbitrary")),
    )(a, b)


def _make_mm(tag, dev_blocks=False):
    @jax.custom_vjp
    def mm(a, w, carrier):
        return _matmul(a, w, b_dev_blocks=dev_blocks, name=f"mm_{tag}")

    def fwd(a, w, carrier):
        return _matmul(a, w, b_dev_blocks=dev_blocks, name=f"mm_{tag}"), (a, w)

    def bwd(res, g):
        a, w = res
        da = _matmul(g, w, trans_b=True, b_dev_blocks=dev_blocks, name=f"mm_{tag}_da")
        dw = _matmul(a, g, trans_a=True, out_dev_blocks=dev_blocks, out_dtype=BF16, name=f"mm_{tag}_dw")
        return da, jnp.zeros_like(w), dw

    mm.defvjp(fwd, bwd)
    return mm


_NT = (((1,), (1,)), ((), ()))
_FOX_SCALE = FOX_HEAD_DIM ** -0.5
_LOG2E = 1.4426950408889634
_LN2 = 0.6931471805599453
_FOX_SCALE2 = _FOX_SCALE * _LOG2E


def _lane_groups_sum(x):
    out = x[:, :LANES]
    for i in range(1, x.shape[1] // LANES):
        out = out + x[:, i * LANES:(i + 1) * LANES]
    return out


def _fox_fwd_call(q, kb, vb, f_col, f_row):
    length = q.shape[0]
    t = ATT_BLOCK
    tq = 2 * t
    nq = length // tq
    hd = FOX_HEAD_DIM

    def body(q_ref, k_ref, v_ref, fq_ref, fk_ref, o_ref, lse_ref, qs_sc, m_sc, l_sc, acc_sc):
        i = pl.program_id(1)
        qs_sc[...] = (q_ref[...] * _FOX_SCALE2).astype(BF16)
        m_sc[...] = jnp.full_like(m_sc, NEG)
        l_sc[...] = jnp.zeros_like(l_sc)
        acc_sc[...] = jnp.zeros_like(acc_sc)

        def block(j, modes):
            off = pl.multiple_of(j * t, t)
            kj = k_ref[pl.ds(off, t), :]
            vj = v_ref[pl.ds(off, t), :]
            fk = fk_ref[0, :, pl.ds(off, t)]
            for a in (0, 1):
                if modes[a] is None:
                    continue
                rows = pl.ds(a * t, t)
                s = lax.dot_general(qs_sc[rows, :], kj, _NT, preferred_element_type=F32)
                s = s + fq_ref[0, rows, :] - fk
                if modes[a]:
                    r = lax.broadcasted_iota(jnp.int32, (t, t), 0)
                    c = lax.broadcasted_iota(jnp.int32, (t, t), 1)
                    s = jnp.where(r >= c, s, NEG)
                m_old = m_sc[a]
                m_new = jnp.maximum(m_old, jnp.max(s, axis=1, keepdims=True))
                alpha = jnp.exp2(m_old - m_new)
                p = jnp.exp2(s - m_new)
                l_sc[a] = alpha * l_sc[a] + _lane_groups_sum(p)
                acc_sc[a] = alpha * acc_sc[a] + jnp.dot(p.astype(BF16), vj, preferred_element_type=F32)
                m_sc[a] = m_new

        def loop_body(j, carry):
            block(j, (False, False))
            return carry

        lax.fori_loop(0, 2 * i, loop_body, 0)
        block(2 * i, (True, False))
        block(2 * i + 1, (None, True))
        for a in (0, 1):
            rows = pl.ds(a * t, t)
            l_row = jnp.sum(l_sc[a], axis=1, keepdims=True)
            o_ref[rows, :] = acc_sc[a] / l_row
            lse_ref[0, rows, :] = m_sc[a] + jnp.log2(l_row)

    return pl.pallas_call(
        body, name="fox_fwd",
        grid=(FOX_HEADS, nq),
        in_specs=[
            pl.BlockSpec((tq, hd), lambda h, i: (i, h)),
            pl.BlockSpec((length, hd), lambda h, i: (0, h // FOX_GROUP)),
            pl.BlockSpec((length, hd), lambda h, i: (0, h // FOX_GROUP)),
            pl.BlockSpec((1, tq, 1), lambda h, i: (h, i, 0)),
            pl.BlockSpec((1, 1, length), lambda h, i: (h, 0, 0)),
        ],
        out_specs=[
            pl.BlockSpec((tq, hd), lambda h, i: (i, h)),
            pl.BlockSpec((1, tq, 1), lambda h, i: (h, i, 0)),
        ],
        out_shape=[jax.ShapeDtypeStruct((length, FOX_Q_DIM), F32),
                   jax.ShapeDtypeStruct((FOX_HEADS, length, 1), F32)],
        scratch_shapes=[pltpu.VMEM((tq, hd), BF16), pltpu.VMEM((2, t, 1), F32), pltpu.VMEM((2, t, LANES), F32),
                        pltpu.VMEM((2, t, hd), F32)],
        compiler_params=_cparams(("parallel", "arbitrary")),
    )(q, kb, vb, f_col, f_row)


def _fox_dq_call(q, kb, vb, f_col, f_row, lse_col, do):
    length = q.shape[0]
    t = ATT_BLOCK
    tq = 2 * t
    nq = length // tq
    hd = FOX_HEAD_DIM

    def body(q_ref, k_ref, v_ref, fq_ref, fk_ref, lse_ref, do_ref, dq_ref, dl_ref, qs_sc, do_sc, acc_sc, dl_sc,
             dl_lane_sc):
        i = pl.program_id(1)
        qs_sc[...] = (q_ref[...] * _FOX_SCALE2).astype(BF16)
        do_sc[...] = do_ref[...].astype(BF16)
        acc_sc[...] = jnp.zeros_like(acc_sc)
        dl_lane_sc[...] = jnp.zeros_like(dl_lane_sc)

        def block(j, modes, second_pass):
            off = pl.multiple_of(j * t, t)
            kj = k_ref[pl.ds(off, t), :]
            vj = v_ref[pl.ds(off, t), :]
            fk = fk_ref[0, :, pl.ds(off, t)]
            for a in (0, 1):
                if modes[a] is None:
                    continue
                rows = pl.ds(a * t, t)
                s = lax.dot_general(qs_sc[rows, :], kj, _NT, preferred_element_type=F32)
                s = s + fq_ref[0, rows, :] - fk
                if modes[a]:
                    r = lax.broadcasted_iota(jnp.int32, (t, t), 0)
                    c = lax.broadcasted_iota(jnp.int32, (t, t), 1)
                    s = jnp.where(r >= c, s, NEG)
                p = jnp.exp2(s - lse_ref[0, rows, :])
                dp = lax.dot_general(do_sc[rows, :], vj, _NT, preferred_element_type=F32)
                if second_pass:
                    ds = p * (dp - dl_sc[a])
                    acc_sc[a] += jnp.dot(ds.astype(BF16), kj, preferred_element_type=F32)
                else:
                    dl_lane_sc[a] += _lane_groups_sum(p * dp)

        def sweep(second_pass):
            def loop_body(j, carry):
                block(j, (False, False), second_pass)
                return carry

            lax.fori_loop(0, 2 * i, loop_body, 0)
            block(2 * i, (True, False), second_pass)
            block(2 * i + 1, (None, True), second_pass)

        sweep(False)
        for a in (0, 1):
            dl_sc[a] = jnp.sum(dl_lane_sc[a], axis=1, keepdims=True)
        sweep(True)
        for a in (0, 1):
            rows = pl.ds(a * t, t)
            dq_ref[rows, :] = acc_sc[a] * _FOX_SCALE
            dl_ref[0, rows, :] = dl_sc[a]

    col = pl.BlockSpec((1, tq, 1), lambda h, i: (h, i, 0))
    return pl.pallas_call(
        body, name="fox_dq",
        grid=(FOX_HEADS, nq),
        in_specs=[
            pl.BlockSpec((tq, hd), lambda h, i: (i, h)),
            pl.BlockSpec((length, hd), lambda h, i: (0, h // FOX_GROUP)),
            pl.BlockSpec((length, hd), lambda h, i: (0, h // FOX_GROUP)),
            col,
            pl.BlockSpec((1, 1, length), lambda h, i: (h, 0, 0)),
            col,
            pl.BlockSpec((tq, hd), lambda h, i: (i, h)),
        ],
        out_specs=[pl.BlockSpec((tq, hd), lambda h, i: (i, h)), col],
        out_shape=[jax.ShapeDtypeStruct((length, FOX_Q_DIM), F32),
                   jax.ShapeDtypeStruct((FOX_HEADS, length, 1), F32)],
        scratch_shapes=[pltpu.VMEM((tq, hd), BF16), pltpu.VMEM((tq, hd), BF16), pltpu.VMEM((2, t, hd), F32),
                        pltpu.VMEM((2, t, 1), F32), pltpu.VMEM((2, t, LANES), F32)],
        compiler_params=_cparams(("parallel", "arbitrary")),
    )(q, kb, vb, f_col, f_row, lse_col, do)


def _fox_dkv_call(qs, dob, kb, vb, f_col, f_row, lse_row, dl_row):
    length = qs.shape[0]
    t = ATT_BLOCK
    tk = 2 * t
    nq = length // t
    hd = FOX_HEAD_DIM

    def body(k_ref, v_ref, q_ref, do_ref, fk_ref, fq_ref, lse_ref, dl_ref,
             dk_ref, dv_ref, df_ref, dk_sc, dv_sc, df_sc):
        j = pl.program_id(1)
        dk_sc[...] = jnp.zeros_like(dk_sc)
        dv_sc[...] = jnp.zeros_like(dv_sc)
        df_sc[...] = jnp.zeros_like(df_sc)

        def block(i, modes):
            off = pl.multiple_of(i * t, t)
            qi = q_ref[pl.ds(off, t), :]
            doi = do_ref[pl.ds(off, t), :]
            fq = fq_ref[0, :, pl.ds(off, t)]
            lse = lse_ref[0, :, pl.ds(off, t)]
            dl = dl_ref[0, :, pl.ds(off, t)]
            for b in (0, 1):
                if modes[b] is None:
                    continue
                keys = pl.ds(b * t, t)
                st = lax.dot_general(k_ref[keys, :], qi, _NT, preferred_element_type=F32)
                st = st + fq - fk_ref[0, keys, :]
                if modes[b]:
                    r = lax.broadcasted_iota(jnp.int32, (t, t), 0)
                    c = lax.broadcasted_iota(jnp.int32, (t, t), 1)
                    st = jnp.where(c >= r, st, NEG)
                pt = jnp.exp2(st - lse)
                dv_sc[b] += jnp.dot(pt.astype(BF16), doi, preferred_element_type=F32)
                dpt = lax.dot_general(v_ref[keys, :], doi, _NT, preferred_element_type=F32)
                dst = pt * (dpt - dl)
                dk_sc[b] += jnp.dot(dst.astype(BF16), qi, preferred_element_type=F32)
                df_sc[b] += _lane_groups_sum(dst)

        def loop_body(i, carry):
            block(i, (False, False))
            return carry

        block(2 * j, (True, None))
        block(2 * j + 1, (False, True))
        lax.fori_loop(2 * j + 2, nq, loop_body, 0)
        for b in (0, 1):
            keys = pl.ds(b * t, t)
            dk_ref[keys, :] = dk_sc[b] * _LN2
            dv_ref[keys, :] = dv_sc[b]
            df_ref[0, keys, :] = -jnp.sum(df_sc[b], axis=1, keepdims=True)

    row = pl.BlockSpec((1, 1, length), lambda h, j: (h, 0, 0))
    return pl.pallas_call(
        body, name="fox_dkv",
        grid=(FOX_HEADS, length // tk),
        in_specs=[
            pl.BlockSpec((tk, hd), lambda h, j: (j, h // FOX_GROUP)),
            pl.BlockSpec((tk, hd), lambda h, j: (j, h // FOX_GROUP)),
            pl.BlockSpec((length, hd), lambda h, j: (0, h)),
            pl.BlockSpec((length, hd), lambda h, j: (0, h)),
            pl.BlockSpec((1, tk, 1), lambda h, j: (h, j, 0)),
            row, row, row,
        ],
        out_specs=[
            pl.BlockSpec((tk, hd), lambda h, j: (j, h)),
            pl.BlockSpec((tk, hd), lambda h, j: (j, h)),
            pl.BlockSpec((1, tk, 1), lambda h, j: (h, j, 0)),
        ],
        out_shape=[jax.ShapeDtypeStruct((length, FOX_Q_DIM), F32),
                   jax.ShapeDtypeStruct((length, FOX_Q_DIM), F32),
                   jax.ShapeDtypeStruct((FOX_HEADS, length, 1), F32)],
        scratch_shapes=[pltpu.VMEM((2, t, hd), F32), pltpu.VMEM((2, t, hd), F32), pltpu.VMEM((2, t, LANES), F32)],
        compiler_params=_cparams(("parallel", "arbitrary")),
    )(kb, vb, qs, dob, f_col, f_row, lse_row, dl_row)


@jax.custom_vjp
def _fox_attention(q, k, v, f):
    return _fox_attention_fwd(q, k, v, f)[0]


def _fox_attention_fwd(q, k, v, f):
    kb = k.astype(BF16)
    vb = v.astype(BF16)
    f2 = jnp.transpose(f) * _LOG2E
    f_row = f2[:, None, :]
    f_col = f2[:, :, None]
    o, lse = _fox_fwd_call(q, kb, vb, f_col, f_row)
    return o, (q, kb, vb, f_col, f_row, lse)


def _fox_attention_bwd(res, do):
    q, kb, vb, f_col, f_row, lse = res
    length = q.shape[0]
    lse_row = jnp.transpose(lse, (0, 2, 1))
    dq, dl_col = _fox_dq_call(q, kb, vb, f_col, f_row, lse, do)
    dl_row = jnp.transpose(dl_col, (0, 2, 1))
    qs = (q * _FOX_SCALE2).astype(BF16)
    dkh, dvh, dfk = _fox_dkv_call(qs, do.astype(BF16), kb, vb, f_col, f_row, lse_row, dl_row)
    dk = jnp.sum(dkh.reshape(length, FOX_KV_HEADS, FOX_GROUP, FOX_HEAD_DIM), axis=2).reshape(length, FOX_KV_DIM)
    dv = jnp.sum(dvh.reshape(length, FOX_KV_HEADS, FOX_GROUP, FOX_HEAD_DIM), axis=2).reshape(length, FOX_KV_DIM)
    df = jnp.transpose(dfk[:, :, 0])
    return dq, dk, dv, df


_fox_attention.defvjp(_fox_attention_fwd, _fox_attention_bwd)


def _peer(mx, my, mc, k):
    return (1 - mx if k & 4 else mx, 1 - my if k & 2 else my, 1 - mc if k & 1 else mc)


def _all_gather(x, *, name, in_vmem):
    space = pltpu.VMEM if in_vmem else pl.ANY

    def body(x_ref, out_ref, send_sems, recv_sems, local_sem):
        mx, my, mc = lax.axis_index("x"), lax.axis_index("y"), lax.axis_index("c")
        me = 4 * mx + 2 * my + mc
        mine = pltpu.make_async_copy(x_ref, out_ref.at[me], local_sem)
        mine.start()
        copies = []
        for k in range(1, N_DEV):
            cp = pltpu.make_async_remote_copy(
                src_ref=x_ref, dst_ref=out_ref.at[me],
                send_sem=send_sems.at[k - 1], recv_sem=recv_sems.at[k - 1],
                device_id=_peer(mx, my, mc, k), device_id_type=MESH)
            cp.start()
            copies.append(cp)
        for cp in copies:
            cp.wait()
        mine.wait()

    return pl.pallas_call(
        body, name=name,
        out_shape=jax.ShapeDtypeStruct((N_DEV,) + x.shape, x.dtype),
        in_specs=[pl.BlockSpec(memory_space=space)],
        out_specs=pl.BlockSpec(memory_space=space),
        scratch_shapes=[pltpu.SemaphoreType.DMA((N_DEV - 1,)), pltpu.SemaphoreType.DMA((N_DEV - 1,)),
                        pltpu.SemaphoreType.DMA],
    )(x)


N_CHIP = N_DEV // 2
_HBM = pl.BlockSpec(memory_space=pl.ANY)


def _all_gather_by_chip(xs, *, name):
    n_arr = len(xs)
    per = N_DEV - 1

    def body(*refs):
        x_refs, out_refs = refs[:n_arr], refs[n_arr:2 * n_arr]
        send_sems, recv_sems, local_sems = refs[2 * n_arr:]
        mx, my, mc = lax.axis_index("x"), lax.axis_index("y"), lax.axis_index("c")
        me, sibling = (mx, my, mc), (mx, my, 1 - mc)
        chips = [(1 - mx, my), (mx, 1 - my), (1 - mx, 1 - my)]

        def copy(a, k, block, to, from_input=False):
            px, py, pc = block
            slot = out_refs[a].at[4 * px + 2 * py + pc]
            return pltpu.make_async_remote_copy(
                src_ref=x_refs[a] if from_input else slot, dst_ref=slot,
                send_sem=send_sems.at[a * per + k], recv_sem=recv_sems.at[a * per + k],
                device_id=to, device_id_type=MESH)

        mine = [pltpu.make_async_copy(x_refs[a], out_refs[a].at[4 * mx + 2 * my + mc], local_sems.at[a])
                for a in range(n_arr)]
        for cp in mine:
            cp.start()
        sent = []
        for a in range(n_arr):
            sent.append(copy(a, 0, me, sibling, from_input=True))
            sent += [copy(a, 1 + j, me, (*chip, mc), from_input=True) for j, chip in enumerate(chips)]
        for cp in sent:
            cp.start()
        for a in range(n_arr):
            for j, chip in enumerate(chips):
                copy(a, 1 + j, (*chip, mc), me).wait_recv()
                passed = copy(a, 4 + j, (*chip, mc), sibling)
                passed.start()
                sent.append(passed)
        for a in range(n_arr):
            copy(a, 0, sibling, me).wait_recv()
            for j, chip in enumerate(chips):
                copy(a, 4 + j, (*chip, 1 - mc), me).wait_recv()
        for cp in sent:
            cp.wait_send()
        for cp in mine:
            cp.wait()

    return pl.pallas_call(
        body, name=name,
        out_shape=[jax.ShapeDtypeStruct((N_DEV,) + x.shape, x.dtype) for x in xs],
        in_specs=[_HBM] * n_arr,
        out_specs=[_HBM] * n_arr,
        scratch_shapes=[pltpu.SemaphoreType.DMA((n_arr * per,)), pltpu.SemaphoreType.DMA((n_arr * per,)),
                        pltpu.SemaphoreType.DMA((n_arr,))],
    )(*xs)


def _exchange_sibling(xs, *, name):
    n_arr = len(xs)

    def body(*refs):
        x_refs, out_refs = refs[:n_arr], refs[n_arr:2 * n_arr]
        send_sems, recv_sems = refs[2 * n_arr:]
        mc = lax.axis_index("c")
        sibling = (lax.axis_index("x"), lax.axis_index("y"), 1 - mc)
        copies = [pltpu.make_async_remote_copy(
            src_ref=x_refs[a].at[q, 1 - mc], dst_ref=out_refs[a].at[q],
            send_sem=send_sems.at[a * N_CHIP + q], recv_sem=recv_sems.at[a * N_CHIP + q],
            device_id=sibling, device_id_type=MESH) for a in range(n_arr) for q in range(N_CHIP)]
        for cp in copies:
            cp.start()
        for cp in copies:
            cp.wait()

    return pl.pallas_call(
        body, name=name,
        out_shape=[jax.ShapeDtypeStruct((N_CHIP,) + x.shape[2:], x.dtype) for x in xs],
        in_specs=[_HBM] * n_arr,
        out_specs=[_HBM] * n_arr,
        scratch_shapes=[pltpu.SemaphoreType.DMA((n_arr * N_CHIP,)), pltpu.SemaphoreType.DMA((n_arr * N_CHIP,))],
    )(*xs)


def _exchange_chips(xs, *, name):
    n_arr = len(xs)
    per = N_CHIP - 1

    def body(*refs):
        x_refs, out_refs = refs[:n_arr], refs[n_arr:2 * n_arr]
        send_sems, recv_sems, local_sems = refs[2 * n_arr:]
        mx, my, mc = lax.axis_index("x"), lax.axis_index("y"), lax.axis_index("c")
        my_chip = 2 * mx + my
        copies = []
        for a in range(n_arr):
            mine = pltpu.make_async_copy(x_refs[a].at[my_chip], out_refs[a].at[my_chip], local_sems.at[a])
            mine.start()
            copies.append(mine)
            for j, (px, py) in enumerate([(1 - mx, my), (mx, 1 - my), (1 - mx, 1 - my)]):
                cp = pltpu.make_async_remote_copy(
                    src_ref=x_refs[a].at[2 * px + py], dst_ref=out_refs[a].at[my_chip],
                    send_sem=send_sems.at[a * per + j], recv_sem=recv_sems.at[a * per + j],
                    device_id=(px, py, mc), device_id_type=MESH)
                cp.start()
                copies.append(cp)
        for cp in copies:
            cp.wait()

    return pl.pallas_call(
        body, name=name,
        out_shape=[jax.ShapeDtypeStruct(x.shape, x.dtype) for x in xs],
        in_specs=[_HBM] * n_arr,
        out_specs=[_HBM] * n_arr,
        scratch_shapes=[pltpu.SemaphoreType.DMA((n_arr * per,)), pltpu.SemaphoreType.DMA((n_arr * per,)),
                        pltpu.SemaphoreType.DMA((n_arr,))],
    )(*xs)


def _add_pair(a, b, *, name):
    n, r, c = a.shape
    tr = _pick(r, (512, 256, 128, 64, 32, 16))

    def body(a_ref, b_ref, o_ref):
        o_ref[...] = (a_ref[...].astype(F32) + b_ref[...].astype(F32)).astype(o_ref.dtype)

    spec = pl.BlockSpec((1, tr, c), lambda q, i: (q, i, 0))
    return pl.pallas_call(
        body, name=name,
        grid=(n, r // tr),
        in_specs=[spec, spec],
        out_specs=spec,
        out_shape=jax.ShapeDtypeStruct(a.shape, a.dtype),
        compiler_params=_cparams(("parallel", "parallel")),
    )(a, b)


def _sum_slots(x, *, name):
    n, r, c = x.shape
    tr = _pick(r, (256, 128, 64, 32, 16, 8))

    def body(x_ref, o_ref):
        acc = x_ref[0].astype(F32)
        for s in range(1, n):
            acc = acc + x_ref[s].astype(F32)
        o_ref[...] = acc

    return pl.pallas_call(
        body, name=name,
        grid=(r // tr,),
        in_specs=[pl.BlockSpec((n, tr, c), lambda i: (0, i, 0))],
        out_specs=pl.BlockSpec((tr, c), lambda i: (i, 0)),
        out_shape=jax.ShapeDtypeStruct((r, c), F32),
        compiler_params=_cparams(("parallel",)),
    )(x)


def _adamw(w, g, m, v, *, name):
    shape = w.shape
    c = shape[-1]
    r = w.size // c
    w2, g2, m2, v2 = (t.reshape(r, c) for t in (w, g, m, v))
    tr = _pick(r, (128, 64, 32, 16, 8))

    def body(w_ref, g_ref, m_ref, v_ref, d_ref, mo_ref, vo_ref):
        gv = g_ref[...]
        mn = ADAM_B1 * m_ref[...] + (1.0 - ADAM_B1) * gv
        vn = ADAM_B2 * v_ref[...] + (1.0 - ADAM_B2) * (gv * gv)
        m_hat = mn / (1.0 - ADAM_B1 ** ADAM_STEP)
        v_hat = vn / (1.0 - ADAM_B2 ** ADAM_STEP)
        d_ref[...] = -ADAM_LR * (m_hat / (jnp.sqrt(v_hat) + ADAM_EPS) + ADAM_WD * w_ref[...])
        mo_ref[...] = mn
        vo_ref[...] = vn

    spec = pl.BlockSpec((tr, c), lambda i: (i, 0))
    outs = pl.pallas_call(
        body, name=name,
        grid=(r // tr,),
        in_specs=[spec] * 4,
        out_specs=[spec] * 3,
        out_shape=[jax.ShapeDtypeStruct((r, c), F32)] * 3,
        compiler_params=_cparams(("parallel",)),
    )(w2, g2, m2, v2)
    return tuple(o.reshape(shape) for o in outs)


GDN_HEADS_PER_STEP = 16
_TN = (((0,), (0,)), ((), ()))


def _gdn_scan_specs(qd, u, attn, reverse):
    h, n, c, dk = qd.shape
    dv = u.shape[-1]
    g = GDN_HEADS_PER_STEP
    idx = (lambda hb, i: (hb, n - 1 - i, 0, 0)) if reverse else (lambda hb, i: (hb, i, 0, 0))
    return dict(
        qk=pl.BlockSpec((g, 1, c, dk), idx), uv=pl.BlockSpec((g, 1, c, dv), idx),
        attn=pl.BlockSpec((g, 1, c, c), idx), lane=pl.BlockSpec((g, 1, 1, dv), idx),
        state=pl.BlockSpec((g, 1, dk, dv), idx), grid=(h // g, n))


def _gdn_scan_fwd_call(qd, kd, u, w, attn, egl):
    h, n, c, dk = qd.shape
    dv = u.shape[-1]
    sp = _gdn_scan_specs(qd, u, attn, reverse=False)

    def body(q_ref, k_ref, u_ref, w_ref, a_ref, e_ref, o_ref, st_ref, s_sc):
        @pl.when(pl.program_id(1) == 0)
        def _():
            s_sc[...] = jnp.zeros_like(s_sc)

        for g in range(GDN_HEADS_PER_STEP):
            s = s_sc[g]
            st_ref[g, 0] = s
            sb = s.astype(BF16)
            v_new = u_ref[g, 0] - jnp.dot(w_ref[g, 0].astype(BF16), sb, preferred_element_type=F32)
            vb = v_new.astype(BF16)
            o_ref[g, 0] = (jnp.dot(q_ref[g, 0].astype(BF16), sb, preferred_element_type=F32)
                           + jnp.dot(a_ref[g, 0].astype(BF16), vb, preferred_element_type=F32))
            s_sc[g] = s * e_ref[g, 0] + lax.dot_general(k_ref[g, 0].astype(BF16), vb, _TN,
                                                        preferred_element_type=F32)

    return pl.pallas_call(
        body, name="gdn_scan_fwd",
        grid=sp["grid"],
        in_specs=[sp["qk"], sp["qk"], sp["uv"], sp["qk"], sp["attn"], sp["lane"]],
        out_specs=[sp["uv"], sp["state"]],
        out_shape=[jax.ShapeDtypeStruct((h, n, c, dv), F32), jax.ShapeDtypeStruct((h, n, dk, dv), F32)],
        scratch_shapes=[pltpu.VMEM((GDN_HEADS_PER_STEP, dk, dv), F32)],
        compiler_params=_cparams(("parallel", "arbitrary")),
    )(qd, kd, u, w, attn, egl)


def _gdn_scan_bwd_call(qd, kd, u, w, attn, egl, states, do):
    h, n, c, dk = qd.shape
    dv = u.shape[-1]
    sp = _gdn_scan_specs(qd, u, attn, reverse=True)

    def body(q_ref, k_ref, u_ref, w_ref, a_ref, e_ref, st_ref, do_ref,
             dq_ref, dk_ref, du_ref, dw_ref, da_ref, de_ref, ds_sc):
        @pl.when(pl.program_id(1) == 0)
        def _():
            ds_sc[...] = jnp.zeros_like(ds_sc)

        for g in range(GDN_HEADS_PER_STEP):
            s = st_ref[g, 0]
            sb = s.astype(BF16)
            ds = ds_sc[g]
            dsb = ds.astype(BF16)
            qb, kb, wb, ab = (r[g, 0].astype(BF16) for r in (q_ref, k_ref, w_ref, a_ref))
            dob = do_ref[g, 0].astype(BF16)
            vb = (u_ref[g, 0] - jnp.dot(wb, sb, preferred_element_type=F32)).astype(BF16)
            dv_new = (lax.dot_general(ab, dob, _TN, preferred_element_type=F32)
                      + jnp.dot(kb, dsb, preferred_element_type=F32))
            dvb = dv_new.astype(BF16)
            dq_ref[g, 0] = lax.dot_general(dob, sb, _NT, preferred_element_type=F32)
            da_ref[g, 0] = lax.dot_general(dob, vb, _NT, preferred_element_type=F32)
            dk_ref[g, 0] = lax.dot_general(vb, dsb, _NT, preferred_element_type=F32)
            du_ref[g, 0] = dv_new
            dw_ref[g, 0] = -lax.dot_general(dvb, sb, _NT, preferred_element_type=F32)
            de_ref[g, 0] = jnp.sum(ds * s, axis=0, keepdims=True)
            ds_sc[g] = (ds * e_ref[g, 0] + lax.dot_general(qb, dob, _TN, preferred_element_type=F32)
                        - lax.dot_general(wb, dvb, _TN, preferred_element_type=F32))

    return pl.pallas_call(
        body, name="gdn_scan_bwd",
        grid=sp["grid"],
        in_specs=[sp["qk"], sp["qk"], sp["uv"], sp["qk"], sp["attn"], sp["lane"], sp["state"], sp["uv"]],
        out_specs=[sp["qk"], sp["qk"], sp["uv"], sp["qk"], sp["attn"], sp["lane"]],
        out_shape=[jax.ShapeDtypeStruct((h, n, c, dk), F32), jax.ShapeDtypeStruct((h, n, c, dk), F32),
                   jax.ShapeDtypeStruct((h, n, c, dv), F32), jax.ShapeDtypeStruct((h, n, c, dk), F32),
                   jax.ShapeDtypeStruct((h, n, c, c), F32), jax.ShapeDtypeStruct((h, n, 1, dv), F32)],
        scratch_shapes=[pltpu.VMEM((GDN_HEADS_PER_STEP, dk, dv), F32)],
        compiler_params=_cparams(("parallel", "arbitrary")),
    )(qd, kd, u, w, attn, egl, states, do)


GDN_CHUNKS_PER_STEP = 16
GDN_DIAG = 16


def _tri_solve_fwd_call(a, rhs):
    h, n, c, _ = a.shape
    r = rhs.shape[-1]
    g_n = GDN_CHUNKS_PER_STEP

    batched = (((2,), (1,)), ((0,), (0,)))

    def dot(x, y):
        return lax.dot_general(x.astype(BF16), y.astype(BF16), batched, preferred_element_type=F32)

    def body(a_ref, rhs_ref, sol_ref, t_ref):
        ri = lax.broadcasted_iota(jnp.int32, (g_n, c, c), 1)
        ci = lax.broadcasted_iota(jnp.int32, (g_n, c, c), 2)
        eye = (ri == ci).astype(F32)
        av = a_ref[0]
        d1 = jnp.where(ri // GDN_DIAG == ci // GDN_DIAG, av, 0.0)
        d2 = dot(d1, d1)
        d4 = dot(d2, d2)
        d8 = dot(d4, d4)
        t_inv = dot(dot(eye - d1, eye + d2), dot(eye + d4, eye + d8))
        size = GDN_DIAG
        while size < c:
            below = jnp.where((ri // (2 * size) == ci // (2 * size)) & (ri // size != ci // size), av, 0.0)
            t_inv = t_inv - dot(t_inv, dot(below, t_inv))
            size *= 2
        t_ref[0] = t_inv
        sol_ref[0] = dot(t_inv, rhs_ref[0])

    a_spec = pl.BlockSpec((1, g_n, c, c), lambda i, j: (i, j, 0, 0))
    r_spec = pl.BlockSpec((1, g_n, c, r), lambda i, j: (i, j, 0, 0))
    return pl.pallas_call(
        body, name="gdn_tri_solve",
        grid=(h, n // g_n),
        in_specs=[a_spec, r_spec],
        out_specs=[r_spec, a_spec],
        out_shape=[jax.ShapeDtypeStruct(rhs.shape, F32), jax.ShapeDtypeStruct(a.shape, F32)],
        compiler_params=_cparams(("parallel", "parallel")),
    )(a, rhs)


def _tri_solve_bwd_call(t_inv, sol, dsol):
    h, n, c, _ = t_inv.shape
    r = sol.shape[-1]
    g_n = GDN_CHUNKS_PER_STEP

    def body(t_ref, sol_ref, dsol_ref, drhs_ref, da_ref):
        for g in range(g_n):
            d_rhs = lax.dot_general(t_ref[0, g].astype(BF16), dsol_ref[0, g].astype(BF16), _TN,
                                    preferred_element_type=F32)
            drhs_ref[0, g] = d_rhs
            da_ref[0, g] = -lax.dot_general(d_rhs.astype(BF16), sol_ref[0, g].astype(BF16), _NT,
                                            preferred_element_type=F32)

    a_spec = pl.BlockSpec((1, g_n, c, c), lambda i, j: (i, j, 0, 0))
    r_spec = pl.BlockSpec((1, g_n, c, r), lambda i, j: (i, j, 0, 0))
    return pl.pallas_call(
        body, name="gdn_tri_solve_bwd",
        grid=(h, n // g_n),
        in_specs=[a_spec, r_spec, r_spec],
        out_specs=[r_spec, a_spec],
        out_shape=[jax.ShapeDtypeStruct(sol.shape, F32), jax.ShapeDtypeStruct(t_inv.shape, F32)],
        compiler_params=_cparams(("parallel", "parallel")),
    )(t_inv, sol, dsol)


@jax.custom_vjp
def _tri_solve(a, rhs):
    return _tri_solve_fwd_call(a, rhs)[0]


def _tri_solve_fwd(a, rhs):
    sol, t_inv = _tri_solve_fwd_call(a, rhs)
    return sol, (t_inv, sol)


def _tri_solve_bwd(res, dsol):
    t_inv, sol = res
    d_rhs, d_a = _tri_solve_bwd_call(t_inv, sol, dsol)
    return d_a, d_rhs


_tri_solve.defvjp(_tri_solve_fwd, _tri_solve_bwd)


@jax.custom_vjp
def _gdn_scan(qd, kd, u, w, attn, g_last):
    return _gdn_scan_fwd(qd, kd, u, w, attn, g_last)[0]


def _gdn_scan_fwd(qd, kd, u, w, attn, g_last):
    egl = jnp.broadcast_to(jnp.exp(g_last)[:, :, None, None], g_last.shape + (1, u.shape[-1]))
    o, states = _gdn_scan_fwd_call(qd, kd, u, w, attn, egl)
    return o, (qd, kd, u, w, attn, egl, states)


def _gdn_scan_bwd(res, do):
    qd, kd, u, w, attn, egl, states = res
    dq, dk, du, dw, da, de = _gdn_scan_bwd_call(qd, kd, u, w, attn, egl, states, do)
    dgl = jnp.sum(de[:, :, 0, :], axis=-1) * egl[:, :, 0, 0]
    return dq, dk, du, dw, da, dgl


_gdn_scan.defvjp(_gdn_scan_fwd, _gdn_scan_bwd)


@functools.partial(jax.custom_vjp, nondiff_argnums=(1,))
def _split_cols(x, sizes):
    bounds = [sum(sizes[:i]) for i in range(len(sizes) + 1)]
    return tuple(x[:, bounds[i]:bounds[i + 1]] for i in range(len(sizes)))


def _split_cols_fwd(x, sizes):
    return _split_cols(x, sizes), None


def _split_cols_bwd(sizes, _, parts):
    return (jnp.concatenate(parts, axis=1),)


_split_cols.defvjp(_split_cols_fwd, _split_cols_bwd)


def _rms_norm(x, w):
    y = x * lax.rsqrt(jnp.mean(x * x, axis=-1, keepdims=True) + NORM_EPS)
    return y * w


def _modulate(h, shift, scale):
    return h * (1 + scale) + shift


def _l2_normalize(x):
    return x * lax.rsqrt(jnp.sum(x * x, axis=-1, keepdims=True) + NORM_EPS)


def _causal_conv(x, w):
    width = w.shape[0]
    length = x.shape[0]
    xp = jnp.pad(x, ((width - 1, 0), (0, 0)))
    return sum(xp[i:i + length] * w[i] for i in range(width))


def _gated_delta_rule_chunked(q, k, v, g, beta):
    h, length, dk = q.shape
    dv = v.shape[-1]
    n = length // GDN_CHUNK
    blk = lambda t: t.reshape(h, n, GDN_CHUNK, *t.shape[2:])
    q = blk(q) * dk ** -0.5
    k = blk(k)
    v = blk(v)
    beta = blk(beta)
    g = jnp.cumsum(blk(g), axis=-1)
    causal = jnp.tril(jnp.ones((GDN_CHUNK, GDN_CHUNK), dtype=bool))
    strict = jnp.tril(jnp.ones((GDN_CHUNK, GDN_CHUNK), dtype=bool), k=-1)
    decay = jnp.exp(jnp.where(causal, g[..., :, None] - g[..., None, :], -jnp.inf))
    k_beta = k * beta[..., None]
    a_strict = jnp.where(strict, jnp.einsum('hnid,hnjd->hnij', k_beta, k) * decay, 0.0)
    rhs = jnp.concatenate([v * beta[..., None], k_beta * jnp.exp(g)[..., None]], axis=-1)
    sol = _tri_solve(a_strict, rhs)
    u, w = sol[..., :dv], sol[..., dv:]
    attn = jnp.where(causal, jnp.einsum('hnid,hnjd->hnij', q, k) * decay, 0.0)
    g_last = g[..., -1]
    q_dec = q * jnp.exp(g)[..., None]
    k_dec = k * jnp.exp(g_last[..., None] - g)[..., None]

    return _gdn_scan(q_dec, k_dec, u, w, attn, g_last).reshape(h, length, dv)


_MM = {tag: _make_mm(tag) for tag in ("gdn_in", "gdn_ba", "gdn_out", "ffn_out0", "kv", "kv_f", "fox_out", "ffn_out1")}
_MM.update({tag: _make_mm(tag, dev_blocks=True) for tag in ("ffn_in0", "ffn_in1", "fox_in")})


def _swiglu(h, w_in, w_out, c_in, c_out, layer):
    gu = _MM[f"ffn_in{layer}"](h, w_in, c_in)
    gate, up = _split_cols(gu, (FFN_HIDDEN, FFN_HIDDEN))
    return _MM[f"ffn_out{layer}"](jax.nn.silu(gate) * up, w_out, c_out)


def _gated_deltanet(h, wts, car, conv_w, a_log, dt_bias, norm_w):
    length = h.shape[0]
    main = _MM["gdn_in"](h, wts["gdn_in"], car["gdn_in"])
    ba = _MM["gdn_ba"](h, wts["gdn_ba"], car["gdn_ba"])
    qkv, z = _split_cols(main, (CONV_DIM, GDN_V_DIM))
    beta_logit, a = ba[:, :GDN_V_HEADS], ba[:, GDN_V_HEADS:2 * GDN_V_HEADS]
    qkv = jax.nn.silu(_causal_conv(qkv, conv_w))
    q, k, v = _split_cols(qkv, (GDN_QK_DIM, GDN_QK_DIM, GDN_V_DIM))
    rep = GDN_V_HEADS // GDN_QK_HEADS
    heads = lambda t, nh: t.reshape(length, nh, GDN_HEAD_DIM)
    q = jnp.repeat(_l2_normalize(heads(q, GDN_QK_HEADS)), rep, axis=1)
    k = jnp.repeat(_l2_normalize(heads(k, GDN_QK_HEADS)), rep, axis=1)
    v = heads(v, GDN_V_HEADS)
    beta = jax.nn.sigmoid(beta_logit)
    g = -jnp.exp(a_log) * jax.nn.softplus(a + dt_bias)
    tr = lambda t: jnp.swapaxes(t, 0, 1)
    o = _gated_delta_rule_chunked(tr(q), tr(k), tr(v), tr(g), tr(beta))
    o = _rms_norm(tr(o), norm_w) * jax.nn.silu(heads(z, GDN_V_HEADS))
    return _MM["gdn_out"](o.reshape(length, GDN_V_DIM), wts["gdn_out"], car["gdn_out"])


def _local_loss(x, mods, small, car, wts, target):
    split6 = lambda m: tuple(m[i * D_MODEL:(i + 1) * D_MODEL][None, :] for i in range(6))
    sh_m, sc_m, g_m, sh_f, sc_f, g_f = split6(mods["ada0"])
    h = _modulate(_rms_norm(x, small["norm_mix"][0]), sh_m, sc_m)
    y = _gated_deltanet(h, wts, car, small["gdn_conv"], small["gdn_a_log"][0], small["gdn_dt_bias"][0],
                        small["gdn_norm"][0])
    x = x + g_m * y
    h = _modulate(_rms_norm(x, small["norm_ffn"][0]), sh_f, sc_f)
    x = x + g_f * _swiglu(h, wts["ffn_in0"], wts["ffn_out0"], car["ffn_in0"], car["ffn_out0"], 0)
    length = x.shape[0]
    kv_shift, kv_scale = mods["kvada"][None, :D_MODEL], mods["kvada"][None, D_MODEL:]
    hk = _modulate(_rms_norm(x, small["kv_norm"]), kv_shift, kv_scale)
    kv = _MM["kv"](hk, wts["kv"], car["kv"])
    f_logit = _MM["kv_f"](hk, wts["kv_f"], car["kv_f"])[:, :FOX_HEADS]
    k_raw, v_sh = _split_cols(kv, (FOX_KV_DIM, FOX_KV_DIM))
    k_sh = _rms_norm(k_raw.reshape(length, FOX_KV_HEADS, FOX_HEAD_DIM), small["k_norm"])
    k_sh = k_sh.reshape(length, FOX_KV_DIM)
    f_cum = jnp.cumsum(jax.nn.log_sigmoid(f_logit + small["forget_b"]), axis=0)
    sh_m, sc_m, g_m, sh_f, sc_f, g_f = split6(mods["ada1"])
    h = _modulate(_rms_norm(x, small["norm_mix"][1]), sh_m, sc_m)
    qg = _MM["fox_in"](h, wts["fox_in"], car["fox_in"])
    q_raw, q_gate = _split_cols(qg, (FOX_Q_DIM, FOX_Q_DIM))
    q = _rms_norm(q_raw.reshape(length, FOX_HEADS, FOX_HEAD_DIM), small["q_norm"][0])
    o = _fox_attention(q.reshape(length, FOX_Q_DIM), k_sh, v_sh, f_cum)
    y = _MM["fox_out"](o * jax.nn.sigmoid(q_gate), wts["fox_out"], car["fox_out"])
    x = x + g_m * y
    h = _modulate(_rms_norm(x, small["norm_ffn"][1]), sh_f, sc_f)
    x = x + g_f * _swiglu(h, wts["ffn_in1"], wts["ffn_out1"], car["ffn_in1"], car["ffn_out1"], 1)
    out = _modulate(_rms_norm(x, small["out_norm"]), mods["outada"][None, :D_MODEL], mods["outada"][None, D_MODEL:])
    err = jnp.square(out - target)
    return 0.5 * jnp.sum(jnp.mean(err, axis=-1))


def _pad_rows(a, rows):
    return jnp.pad(a, ((0, rows - a.shape[0]), (0, 0)))


def _pack_flat(parts, total):
    flat = jnp.concatenate([p.reshape(-1) for p in parts])
    return jnp.pad(flat, (0, total - flat.shape[0]))


_SMALL_NAMES = ("ada_b", "norm_mix", "norm_ffn", "gdn_a_log", "gdn_dt_bias", "gdn_norm", "kv_ada_b", "kv_norm",
                "k_norm", "forget_b", "q_norm", "out_ada_b", "out_norm")


def kernel(x, c, ada_w, ada_b, norm_mix, norm_ffn, ffn_w_in, ffn_w_out, gdn_w_in, gdn_conv, gdn_a_log, gdn_dt_bias, gdn_norm, gdn_w_out, kv_ada_w, kv_ada_b, kv_norm, kv_w, k_norm, forget_b, fox_w_in, q_norm, fox_w_out, out_ada_w, out_ada_b, out_norm, loss_target, m_ada_w, m_ada_b, m_norm_mix, m_norm_ffn, m_ffn_w_in, m_ffn_w_out, m_gdn_w_in, m_gdn_conv, m_gdn_a_log, m_gdn_dt_bias, m_gdn_norm, m_gdn_w_out, m_kv_ada_w, m_kv_ada_b, m_kv_norm, m_kv_w, m_k_norm, m_forget_b, m_fox_w_in, m_q_norm, m_fox_w_out, m_out_ada_w, m_out_ada_b, m_out_norm, v_ada_w, v_ada_b, v_norm_mix, v_norm_ffn, v_ffn_w_in, v_ffn_w_out, v_gdn_w_in, v_gdn_conv, v_gdn_a_log, v_gdn_dt_bias, v_gdn_norm, v_gdn_w_out, v_kv_ada_w, v_kv_ada_b, v_kv_norm, v_kv_w, v_k_norm, v_forget_b, v_fox_w_in, v_q_norm, v_fox_w_out, v_out_ada_w, v_out_ada_b, v_out_norm):
    w_in = dict(ada_w=ada_w, ada_b=ada_b, norm_mix=norm_mix, norm_ffn=norm_ffn, ffn_w_in=ffn_w_in, ffn_w_out=ffn_w_out, gdn_w_in=gdn_w_in, gdn_conv=gdn_conv, gdn_a_log=gdn_a_log, gdn_dt_bias=gdn_dt_bias, gdn_norm=gdn_norm, gdn_w_out=gdn_w_out, kv_ada_w=kv_ada_w, kv_ada_b=kv_ada_b, kv_norm=kv_norm, kv_w=kv_w, k_norm=k_norm, forget_b=forget_b, fox_w_in=fox_w_in, q_norm=q_norm, fox_w_out=fox_w_out, out_ada_w=out_ada_w, out_ada_b=out_ada_b, out_norm=out_norm)
    m_in = dict(ada_w=m_ada_w, ada_b=m_ada_b, norm_mix=m_norm_mix, norm_ffn=m_norm_ffn, ffn_w_in=m_ffn_w_in, ffn_w_out=m_ffn_w_out, gdn_w_in=m_gdn_w_in, gdn_conv=m_gdn_conv, gdn_a_log=m_gdn_a_log, gdn_dt_bias=m_gdn_dt_bias, gdn_norm=m_gdn_norm, gdn_w_out=m_gdn_w_out, kv_ada_w=m_kv_ada_w, kv_ada_b=m_kv_ada_b, kv_norm=m_kv_norm, kv_w=m_kv_w, k_norm=m_k_norm, forget_b=m_forget_b, fox_w_in=m_fox_w_in, q_norm=m_q_norm, fox_w_out=m_fox_w_out, out_ada_w=m_out_ada_w, out_ada_b=m_out_ada_b, out_norm=m_out_norm)
    v_in = dict(ada_w=v_ada_w, ada_b=v_ada_b, norm_mix=v_norm_mix, norm_ffn=v_norm_ffn, ffn_w_in=v_ffn_w_in, ffn_w_out=v_ffn_w_out, gdn_w_in=v_gdn_w_in, gdn_conv=v_gdn_conv, gdn_a_log=v_gdn_a_log, gdn_dt_bias=v_gdn_dt_bias, gdn_norm=v_gdn_norm, gdn_w_out=v_gdn_w_out, kv_ada_w=v_kv_ada_w, kv_ada_b=v_kv_ada_b, kv_norm=v_kv_norm, kv_w=v_kv_w, k_norm=v_k_norm, forget_b=v_forget_b, fox_w_in=v_fox_w_in, q_norm=v_q_norm, fox_w_out=v_fox_w_out, out_ada_w=v_out_ada_w, out_ada_b=v_out_ada_b, out_norm=v_out_norm)
    names = list(w_in)
    me = 4 * lax.axis_index("x") + 2 * lax.axis_index("y") + lax.axis_index("c")
    d = D_MODEL

    n_conv = CONV_K * (CONV_DIM // N_DEV)
    first = _pack_flat([c, gdn_conv], d + n_conv).reshape(-1, LANES)
    first_all = _all_gather(first, name="ag_cond_conv", in_vmem=True).reshape(N_DEV, d + n_conv)
    cond_all = jax.nn.silu(first_all[:, :d])
    conv_full = first_all[:, d:].reshape(N_DEV, CONV_K, CONV_DIM // N_DEV)
    conv_full = jnp.transpose(conv_full, (1, 0, 2)).reshape(CONV_K, CONV_DIM)

    ada_cat = jnp.concatenate([ada_w[0], ada_w[1], kv_ada_w, out_ada_w], axis=1)
    n_ada = ada_cat.shape[1]
    cond_pad = _pad_rows(cond_all, LANES)
    mods_part = _matmul(cond_pad, ada_cat, name="mm_ada")[:N_DEV]
    mods_all = _all_gather(mods_part.reshape(-1, LANES), name="ag_mods", in_vmem=True)
    mods_all = mods_all.reshape(N_DEV, N_DEV, n_ada)
    mine = lax.dynamic_index_in_dim(mods_all, me, axis=1, keepdims=False)
    s6 = 6 * d // N_DEV
    s2 = 2 * d // N_DEV
    mods = {
        "ada0": mine[:, :s6].reshape(-1) + ada_b[0],
        "ada1": mine[:, s6:2 * s6].reshape(-1) + ada_b[1],
        "kvada": mine[:, 2 * s6:2 * s6 + s2].reshape(-1) + kv_ada_b,
        "outada": mine[:, 2 * s6 + s2:].reshape(-1) + out_ada_b,
    }

    big = [ffn_w_in[0], ffn_w_in[1], ffn_w_out[0], ffn_w_out[1], gdn_w_in[0], gdn_w_out[0], kv_w, fox_w_in[0],
           fox_w_out[0]]
    g_ffn_in0, g_ffn_in1, g_ffn_out0, g_ffn_out1, g_gdn_in, g_gdn_out, g_kv, g_fox_in, g_fox_out = (
        _all_gather_by_chip([b.astype(BF16) for b in big], name="ag_weights"))
    rows = lambda t: t.reshape(N_DEV * t.shape[1], t.shape[2])
    per_gdn = GDN_PROJ // N_DEV
    last_main = GDN_MAIN - (N_DEV - 1) * per_gdn
    gdn_main = jnp.concatenate([g_gdn_in[dv] for dv in range(N_DEV - 1)] + [g_gdn_in[N_DEV - 1][:, :last_main]],
                               axis=1)
    gdn_ba = jnp.pad(g_gdn_in[N_DEV - 1][:, last_main:], ((0, 0), (0, LANES - 2 * GDN_V_HEADS)))
    kv_full = rows(g_kv)
    wts = {
        "ffn_in0": g_ffn_in0, "ffn_in1": g_ffn_in1, "ffn_out0": rows(g_ffn_out0), "ffn_out1": rows(g_ffn_out1),
        "gdn_in": gdn_main, "gdn_ba": gdn_ba, "gdn_out": rows(g_gdn_out),
        "kv": kv_full[:, :2 * FOX_KV_DIM],
        "kv_f": jnp.pad(kv_full[:, 2 * FOX_KV_DIM:], ((0, 0), (0, LANES - FOX_HEADS))),
        "fox_in": g_fox_in, "fox_out": rows(g_fox_out),
    }
    car = {k: jnp.zeros(w.shape, BF16) for k, w in wts.items()}
    small = {n: w_in[n] for n in _SMALL_NAMES if n not in ("ada_b", "kv_ada_b", "out_ada_b")}
    small["gdn_conv"] = conv_full

    loss_fn = functools.partial(_local_loss, wts=wts, target=loss_target[0])
    loss_local, vjp = jax.vjp(loss_fn, x[0], mods, small, car)
    gx, gmods, gsmall, gcar = vjp(jnp.ones((), F32))
    loss = lax.psum(loss_local, ("x", "y", "c"))

    gsm = dict(gsmall)
    gsm["ada_b"] = jnp.stack([gmods["ada0"], gmods["ada1"]])
    gsm["kv_ada_b"] = gmods["kvada"]
    gsm["out_ada_b"] = gmods["outada"]
    small_sizes = [w_in[n].size for n in _SMALL_NAMES]
    n_small = sum(small_sizes) + CONV_K * CONV_DIM
    small_rows = -(-n_small // (LANES * 8)) * 8
    vec = _pack_flat([gsm[n] for n in _SMALL_NAMES] + [gsm["gdn_conv"]], small_rows * LANES)
    vec_all = _all_gather(vec.reshape(small_rows, LANES), name="ag_small_grads", in_vmem=True)
    vec_sum = _sum_slots(vec_all, name="sum_small_grads").reshape(-1)
    grads = {}
    off = 0
    for n, sz in zip(_SMALL_NAMES, small_sizes):
        grads[n] = vec_sum[off:off + sz].reshape(w_in[n].shape)
        off += sz
    conv_sum = vec_sum[off:off + CONV_K * CONV_DIM].reshape(CONV_K, N_DEV, CONV_DIM // N_DEV)
    grads["gdn_conv"] = lax.dynamic_index_in_dim(conv_sum, me, axis=1, keepdims=False)[None]

    vec_flat = vec_all.reshape(N_DEV, -1)
    d_ada = vec_flat[:, :2 * 6 * d].reshape(N_DEV, 2, N_DEV, s6)
    o_kv = sum(small_sizes[:6])
    d_kvada = vec_flat[:, o_kv:o_kv + 2 * d].reshape(N_DEV, N_DEV, s2)
    o_out = sum(small_sizes[:11])
    d_outada = vec_flat[:, o_out:o_out + 2 * d].reshape(N_DEV, N_DEV, s2)
    pick = lambda t, axis: lax.dynamic_index_in_dim(t, me, axis=axis, keepdims=False)
    d_cat = jnp.concatenate([pick(d_ada[:, 0], 1), pick(d_ada[:, 1], 1), pick(d_kvada, 1), pick(d_outada, 1)],
                            axis=1)
    g_ada_cat = _matmul(cond_pad, _pad_rows(d_cat, LANES), trans_a=True, name="mm_ada_dw")
    grads["ada_w"] = jnp.stack([g_ada_cat[:, :s6], g_ada_cat[:, s6:2 * s6]])
    grads["kv_ada_w"] = g_ada_cat[:, 2 * s6:2 * s6 + s2]
    grads["out_ada_w"] = g_ada_cat[:, 2 * s6 + s2:]

    d_main, d_ba = gcar["gdn_in"], gcar["gdn_ba"]
    d_gdn_in = jnp.stack(
        [d_main[:, dv * per_gdn:(dv + 1) * per_gdn] for dv in range(N_DEV - 1)]
        + [jnp.concatenate([d_main[:, (N_DEV - 1) * per_gdn:], d_ba[:, :2 * GDN_V_HEADS]], axis=1)])
    d_kv = jnp.concatenate([gcar["kv"], gcar["kv_f"][:, :FOX_HEADS]], axis=1)
    owner_rows = lambda t: t.reshape(N_DEV, t.shape[0] // N_DEV, t.shape[1])
    by_owner = [gcar["ffn_in0"], gcar["ffn_in1"], owner_rows(gcar["ffn_out0"]), owner_rows(gcar["ffn_out1"]),
                d_gdn_in, owner_rows(gcar["gdn_out"]), owner_rows(d_kv), gcar["fox_in"],
                owner_rows(gcar["fox_out"])]
    by_chip_core = [t.reshape((N_CHIP, 2) + t.shape[1:]) for t in by_owner]
    my_c = lax.axis_index("c")
    from_sibling = _exchange_sibling(by_chip_core, name="rs_sibling")
    shard_grads = []
    pair_sums = []
    for idx, (t, got) in enumerate(zip(by_chip_core, from_sibling)):
        for_my_core = lax.dynamic_index_in_dim(t, my_c, axis=1, keepdims=False)
        pair_sums.append(_add_pair(for_my_core, got, name=f"rs_pair_sum{idx}"))
    by_chip = _exchange_chips(pair_sums, name="rs_chips")
    for idx, t in enumerate(by_chip):
        shard_grads.append(_sum_slots(t, name=f"sum_grads{idx}"))
    grads["ffn_w_in"] = jnp.stack(shard_grads[0:2])
    grads["ffn_w_out"] = jnp.stack(shard_grads[2:4])
    grads["gdn_w_in"] = shard_grads[4][None]
    grads["gdn_w_out"] = shard_grads[5][None]
    grads["kv_w"] = shard_grads[6]
    grads["fox_w_in"] = shard_grads[7][None]
    grads["fox_w_out"] = shard_grads[8][None]

    delta, new_m, new_v = {}, {}, {}
    for n in names:
        if n in _SMALL_NAMES:
            continue
        delta[n], new_m[n], new_v[n] = _adamw(w_in[n], grads[n], m_in[n], v_in[n], name=f"adamw_{n}")
    sm_rows = -(-sum(small_sizes) // (LANES * 8)) * 8
    pk = lambda src: _pack_flat([src[n] for n in _SMALL_NAMES], sm_rows * LANES).reshape(sm_rows, LANES)
    d_pk, m_pk, vn_pk = _adamw(pk(w_in), pk(grads), pk(m_in), pk(v_in), name="adamw_small")
    off = 0
    for n, sz in zip(_SMALL_NAMES, small_sizes):
        shp = w_in[n].shape
        delta[n] = d_pk.reshape(-1)[off:off + sz].reshape(shp)
        new_m[n] = m_pk.reshape(-1)[off:off + sz].reshape(shp)
        new_v[n] = vn_pk.reshape(-1)[off:off + sz].reshape(shp)
        off += sz

    return (loss, gx[None], *[grads[n] for n in names], *[delta[n] for n in names],
            *[new_m[n] for n in names], *[new_v[n] for n in names])
```

```python
import functools

import jax
import jax.numpy as jnp
from jax import lax
from jax.experimental import pallas as pl
from jax.experimental.pallas import tpu as pltpu

F32 = jnp.float32
BF16 = jnp.bfloat16
MESH = pl.DeviceIdType.MESH
N_DEV = 8

D_MODEL = 2048
SEQ = 8192
GDN_QK_HEADS = 16
GDN_V_HEADS = 32
GDN_HEAD_DIM = 128
GDN_QK_DIM = GDN_QK_HEADS * GDN_HEAD_DIM
GDN_V_DIM = GDN_V_HEADS * GDN_HEAD_DIM
CONV_DIM = 2 * GDN_QK_DIM + GDN_V_DIM
GDN_MAIN = CONV_DIM + GDN_V_DIM
GDN_PROJ = GDN_MAIN + 2 * GDN_V_HEADS
CONV_K = 4
GDN_CHUNK = 64
FOX_HEADS = 16
FOX_KV_HEADS = 2
FOX_GROUP = FOX_HEADS // FOX_KV_HEADS
FOX_HEAD_DIM = 256
FOX_Q_DIM = FOX_HEADS * FOX_HEAD_DIM
FOX_KV_DIM = FOX_KV_HEADS * FOX_HEAD_DIM
KV_PROJ = 2 * FOX_KV_DIM + FOX_HEADS
FFN_HIDDEN = 5632
NORM_EPS = 1e-6

ADAM_LR = 0.001
ADAM_B1 = 0.9
ADAM_B2 = 0.999
ADAM_EPS = 1e-08
ADAM_WD = 0.01
ADAM_STEP = 10

LANES = 128
VMEM_LIMIT = 56 * 1024 * 1024
PACK_COLS = 1024
ATT_BLOCK = 512
NEG = -1e30


def _pick(n, cands):
    for c in cands:
        if n % c == 0:
            return c
    return n


def _cparams(sem):
    return pltpu.CompilerParams(dimension_semantics=sem, vmem_limit_bytes=VMEM_LIMIT)


def _matmul(a, b, *, trans_a=False, trans_b=False, b_dev_blocks=False, out_dev_blocks=False, out_dtype=F32, name):
    if trans_a:
        kdim, m = a.shape
    else:
        m, kdim = a.shape
    if b_dev_blocks:
        n_dev, rows_b, per_dev = b.shape
        n, kb = (rows_b, n_dev * per_dev) if trans_b else (n_dev * per_dev, rows_b)
    elif trans_b:
        n, kb = b.shape
    else:
        kb, n = b.shape
    assert kdim == kb, (a.shape, b.shape)
    tm = _pick(m, (1024, 512, 256, 128))
    tn = _pick(n // N_DEV if (out_dev_blocks or (b_dev_blocks and not trans_b)) else n, (1408, 1024, 512, 256, 128))
    tk = _pick(kdim // N_DEV if (b_dev_blocks and trans_b) else kdim, (1024, 1408, 512, 256, 128))
    nk = kdim // tk
    dims = (((0 if trans_a else 1,), (1 if trans_b else 0,)), ((), ()))

    def body(a_ref, b_ref, o_ref, acc_ref):
        k = pl.program_id(2)

        @pl.when(k == 0)
        def _():
            acc_ref[...] = jnp.zeros_like(acc_ref)

        acc_ref[...] += lax.dot_general(a_ref[...].astype(BF16), b_ref[...].astype(BF16), dims,
                                        preferred_element_type=F32)

        @pl.when(k == nk - 1)
        def _():
            o_ref[...] = acc_ref[...].astype(o_ref.dtype)

    a_spec = (pl.BlockSpec((tk, tm), lambda i, j, k: (k, i)) if trans_a
              else pl.BlockSpec((tm, tk), lambda i, j, k: (i, k)))
    if b_dev_blocks and trans_b:
        per = kdim // N_DEV // tk
        b_spec = pl.BlockSpec((None, tn, tk), lambda i, j, k: (k // per, j, k % per))
    elif b_dev_blocks:
        per = n // N_DEV // tn
        b_spec = pl.BlockSpec((None, tk, tn), lambda i, j, k: (j // per, k, j % per))
    elif trans_b:
        b_spec = pl.BlockSpec((tn, tk), lambda i, j, k: (j, k))
    else:
        b_spec = pl.BlockSpec((tk, tn), lambda i, j, k: (k, j))
    if out_dev_blocks:
        per_o = n // N_DEV // tn
        out_spec = pl.BlockSpec((None, tm, tn), lambda i, j, k: (j // per_o, i, j % per_o))
        out_shape = jax.ShapeDtypeStruct((N_DEV, m, n // N_DEV), out_dtype)
    else:
        out_spec = pl.BlockSpec((tm, tn), lambda i, j, k: (i, j))
        out_shape = jax.ShapeDtypeStruct((m, n), out_dtype)
    return pl.pallas_call(
        body, name=name,
        grid=(m // tm, n // tn, nk),
        in_specs=[a_spec, b_spec],
        out_specs=out_spec,
        out_shape=out_shape,
        scratch_shapes=[pltpu.VMEM((tm, tn), F32)],
        compiler_params=_cparams(("parallel", "parallel", "arbitrary")),
    )(a, b)


def _make_mm(tag, dev_blocks=False):
    @jax.custom_vjp
    def mm(a, w, carrier):
        return _matmul(a, w, b_dev_blocks=dev_blocks, name=f"mm_{tag}")

    def fwd(a, w, carrier):
        a = a.astype(BF16)
        return _matmul(a, w, b_dev_blocks=dev_blocks, name=f"mm_{tag}"), (a, w)

    def bwd(res, g):
        a, w = res
        g = g.astype(BF16)
        da = _matmul(g, w, trans_b=True, b_dev_blocks=dev_blocks, name=f"mm_{tag}_da")
        dw = _matmul(a, g, trans_a=True, out_dev_blocks=dev_blocks, out_dtype=BF16, name=f"mm_{tag}_dw")
        return da, jnp.zeros_like(w), dw

    mm.defvjp(fwd, bwd)
    return mm


_NT = (((1,), (1,)), ((), ()))
_FOX_SCALE = FOX_HEAD_DIM ** -0.5
_LOG2E = 1.4426950408889634
_LN2 = 0.6931471805599453
_FOX_SCALE2 = _FOX_SCALE * _LOG2E


def _lane_groups_sum(x):
    out = x[:, :LANES]
    for i in range(1, x.shape[1] // LANES):
        out = out + x[:, i * LANES:(i + 1) * LANES]
    return out


def _fox_fwd_call(q, kb, vb, f_col, f_row):
    length = q.shape[0]
    t = ATT_BLOCK
    tq = 2 * t
    nq = length // tq
    hd = FOX_HEAD_DIM

    def body(q_ref, k_ref, v_ref, fq_ref, fk_ref, o_ref, lse_ref, qs_sc, m_sc, l_sc, acc_sc):
        i = pl.program_id(1)
        qs_sc[...] = (q_ref[...] * _FOX_SCALE2).astype(BF16)
        m_sc[...] = jnp.full_like(m_sc, NEG)
        l_sc[...] = jnp.zeros_like(l_sc)
        acc_sc[...] = jnp.zeros_like(acc_sc)

        def block(j, modes):
            off = pl.multiple_of(j * t, t)
            kj = k_ref[pl.ds(off, t), :]
            vj = v_ref[pl.ds(off, t), :]
            fk = fk_ref[0, :, pl.ds(off, t)]
            for a in (0, 1):
                if modes[a] is None:
                    continue
                rows = pl.ds(a * t, t)
                s = lax.dot_general(qs_sc[rows, :], kj, _NT, preferred_element_type=F32)
                s = s + fq_ref[0, rows, :] - fk
                if modes[a]:
                    r = lax.broadcasted_iota(jnp.int32, (t, t), 0)
                    c = lax.broadcasted_iota(jnp.int32, (t, t), 1)
                    s = jnp.where(r >= c, s, NEG)
                m_old = m_sc[a]
                m_new = jnp.maximum(m_old, jnp.max(s, axis=1, keepdims=True))
                alpha = jnp.exp2(m_old - m_new)
                p = jnp.exp2(s - m_new)
                l_sc[a] = alpha * l_sc[a] + _lane_groups_sum(p)
                acc_sc[a] = alpha * acc_sc[a] + jnp.dot(p.astype(BF16), vj, preferred_element_type=F32)
                m_sc[a] = m_new

        def loop_body(j, carry):
            block(j, (False, False))
            return carry

        lax.fori_loop(0, 2 * i, loop_body, 0)
        block(2 * i, (True, False))
        block(2 * i + 1, (None, True))
        for a in (0, 1):
            rows = pl.ds(a * t, t)
            l_row = jnp.sum(l_sc[a], axis=1, keepdims=True)
            o_ref[rows, :] = acc_sc[a] / l_row
            lse_ref[0, rows, :] = m_sc[a] + jnp.log2(l_row)

    return pl.pallas_call(
        body, name="fox_fwd",
        grid=(FOX_HEADS, nq),
        in_specs=[
            pl.BlockSpec((tq, hd), lambda h, i: (i, h)),
            pl.BlockSpec((length, hd), lambda h, i: (0, h // FOX_GROUP)),
            pl.BlockSpec((length, hd), lambda h, i: (0, h // FOX_GROUP)),
            pl.BlockSpec((1, tq, 1), lambda h, i: (h, i, 0)),
            pl.BlockSpec((1, 1, length), lambda h, i: (h, 0, 0)),
        ],
        out_specs=[
            pl.BlockSpec((tq, hd), lambda h, i: (i, h)),
            pl.BlockSpec((1, tq, 1), lambda h, i: (h, i, 0)),
        ],
        out_shape=[jax.ShapeDtypeStruct((length, FOX_Q_DIM), F32),
                   jax.ShapeDtypeStruct((FOX_HEADS, length, 1), F32)],
        scratch_shapes=[pltpu.VMEM((tq, hd), BF16), pltpu.VMEM((2, t, 1), F32), pltpu.VMEM((2, t, LANES), F32),
                        pltpu.VMEM((2, t, hd), F32)],
        compiler_params=_cparams(("parallel", "arbitrary")),
    )(q, kb, vb, f_col, f_row)


def _fox_dq_call(q, kb, vb, f_col, f_row, lse_col, do):
    length = q.shape[0]
    t = ATT_BLOCK
    tq = 2 * t
    nq = length // tq
    hd = FOX_HEAD_DIM

    def body(q_ref, k_ref, v_ref, fq_ref, fk_ref, lse_ref, do_ref, dq_ref, dl_ref, qs_sc, do_sc, acc_sc, dl_sc,
             dl_lane_sc):
        i = pl.program_id(1)
        qs_sc[...] = (q_ref[...] * _FOX_SCALE2).astype(BF16)
        do_sc[...] = do_ref[...].astype(BF16)
        acc_sc[...] = jnp.zeros_like(acc_sc)
        dl_lane_sc[...] = jnp.zeros_like(dl_lane_sc)

        def block(j, modes, second_pass):
            off = pl.multiple_of(j * t, t)
            kj = k_ref[pl.ds(off, t), :]
            vj = v_ref[pl.ds(off, t), :]
            fk = fk_ref[0, :, pl.ds(off, t)]
            for a in (0, 1):
                if modes[a] is None:
                    continue
                rows = pl.ds(a * t, t)
                s = lax.dot_general(qs_sc[rows, :], kj, _NT, preferred_element_type=F32)
                s = s + fq_ref[0, rows, :] - fk
                if modes[a]:
                    r = lax.broadcasted_iota(jnp.int32, (t, t), 0)
                    c = lax.broadcasted_iota(jnp.int32, (t, t), 1)
                    s = jnp.where(r >= c, s, NEG)
                p = jnp.exp2(s - lse_ref[0, rows, :])
                dp = lax.dot_general(do_sc[rows, :], vj, _NT, preferred_element_type=F32)
                if second_pass:
                    ds = p * (dp - dl_sc[a])
                    acc_sc[a] += jnp.dot(ds.astype(BF16), kj, preferred_element_type=F32)
                else:
                    dl_lane_sc[a] += _lane_groups_sum(p * dp)

        def sweep(second_pass):
            def loop_body(j, carry):
                block(j, (False, False), second_pass)
                return carry

            lax.fori_loop(0, 2 * i, loop_body, 0)
            block(2 * i, (True, False), second_pass)
            block(2 * i + 1, (None, True), second_pass)

        sweep(False)
        for a in (0, 1):
            dl_sc[a] = jnp.sum(dl_lane_sc[a], axis=1, keepdims=True)
        sweep(True)
        for a in (0, 1):
            rows = pl.ds(a * t, t)
            dq_ref[rows, :] = acc_sc[a] * _FOX_SCALE
            dl_ref[0, rows, :] = dl_sc[a]

    col = pl.BlockSpec((1, tq, 1), lambda h, i: (h, i, 0))
    return pl.pallas_call(
        body, name="fox_dq",
        grid=(FOX_HEADS, nq),
        in_specs=[
            pl.BlockSpec((tq, hd), lambda h, i: (i, h)),
            pl.BlockSpec((length, hd), lambda h, i: (0, h // FOX_GROUP)),
            pl.BlockSpec((length, hd), lambda h, i: (0, h // FOX_GROUP)),
            col,
            pl.BlockSpec((1, 1, length), lambda h, i: (h, 0, 0)),
            col,
            pl.BlockSpec((tq, hd), lambda h, i: (i, h)),
        ],
        out_specs=[pl.BlockSpec((tq, hd), lambda h, i: (i, h)), col],
        out_shape=[jax.ShapeDtypeStruct((length, FOX_Q_DIM), F32),
                   jax.ShapeDtypeStruct((FOX_HEADS, length, 1), F32)],
        scratch_shapes=[pltpu.VMEM((tq, hd), BF16), pltpu.VMEM((tq, hd), BF16), pltpu.VMEM((2, t, hd), F32),
                        pltpu.VMEM((2, t, 1), F32), pltpu.VMEM((2, t, LANES), F32)],
        compiler_params=_cparams(("parallel", "arbitrary")),
    )(q, kb, vb, f_col, f_row, lse_col, do)


def _fox_dkv_call(qs, dob, kb, vb, f_col, f_row, lse_row, dl_row):
    length = qs.shape[0]
    t = ATT_BLOCK
    tk = 2 * t
    nq = length // t
    hd = FOX_HEAD_DIM

    def body(k_ref, v_ref, q_ref, do_ref, fk_ref, fq_ref, lse_ref, dl_ref,
             dk_ref, dv_ref, df_ref, dk_sc, dv_sc, df_sc):
        j = pl.program_id(1)
        dk_sc[...] = jnp.zeros_like(dk_sc)
        dv_sc[...] = jnp.zeros_like(dv_sc)
        df_sc[...] = jnp.zeros_like(df_sc)

        def block(i, modes):
            off = pl.multiple_of(i * t, t)
            qi = q_ref[pl.ds(off, t), :]
            doi = do_ref[pl.ds(off, t), :]
            fq = fq_ref[0, :, pl.ds(off, t)]
            lse = lse_ref[0, :, pl.ds(off, t)]
            dl = dl_ref[0, :, pl.ds(off, t)]
            for b in (0, 1):
                if modes[b] is None:
                    continue
                keys = pl.ds(b * t, t)
                st = lax.dot_general(k_ref[keys, :], qi, _NT, preferred_element_type=F32)
                st = st + fq - fk_ref[0, keys, :]
                if modes[b]:
                    r = lax.broadcasted_iota(jnp.int32, (t, t), 0)
                    c = lax.broadcasted_iota(jnp.int32, (t, t), 1)
                    st = jnp.where(c >= r, st, NEG)
                pt = jnp.exp2(st - lse)
                dv_sc[b] += jnp.dot(pt.astype(BF16), doi, preferred_element_type=F32)
                dpt = lax.dot_general(v_ref[keys, :], doi, _NT, preferred_element_type=F32)
                dst = pt * (dpt - dl)
                dk_sc[b] += jnp.dot(dst.astype(BF16), qi, preferred_element_type=F32)
                df_sc[b] += _lane_groups_sum(dst)

        def loop_body(i, carry):
            block(i, (False, False))
            return carry

        block(2 * j, (True, None))
        block(2 * j + 1, (False, True))
        lax.fori_loop(2 * j + 2, nq, loop_body, 0)
        for b in (0, 1):
            keys = pl.ds(b * t, t)
            dk_ref[keys, :] = dk_sc[b] * _LN2
            dv_ref[keys, :] = dv_sc[b]
            df_ref[0, keys, :] = -jnp.sum(df_sc[b], axis=1, keepdims=True)

    row = pl.BlockSpec((1, 1, length), lambda h, j: (h, 0, 0))
    return pl.pallas_call(
        body, name="fox_dkv",
        grid=(FOX_HEADS, length // tk),
        in_specs=[
            pl.BlockSpec((tk, hd), lambda h, j: (j, h // FOX_GROUP)),
            pl.BlockSpec((tk, hd), lambda h, j: (j, h // FOX_GROUP)),
            pl.BlockSpec((length, hd), lambda h, j: (0, h)),
            pl.BlockSpec((length, hd), lambda h, j: (0, h)),
            pl.BlockSpec((1, tk, 1), lambda h, j: (h, j, 0)),
            row, row, row,
        ],
        out_specs=[
            pl.BlockSpec((tk, hd), lambda h, j: (j, h)),
            pl.BlockSpec((tk, hd), lambda h, j: (j, h)),
            pl.BlockSpec((1, tk, 1), lambda h, j: (h, j, 0)),
        ],
        out_shape=[jax.ShapeDtypeStruct((length, FOX_Q_DIM), F32),
                   jax.ShapeDtypeStruct((length, FOX_Q_DIM), F32),
                   jax.ShapeDtypeStruct((FOX_HEADS, length, 1), F32)],
        scratch_shapes=[pltpu.VMEM((2, t, hd), F32), pltpu.VMEM((2, t, hd), F32), pltpu.VMEM((2, t, LANES), F32)],
        compiler_params=_cparams(("parallel", "arbitrary")),
    )(kb, vb, qs, dob, f_col, f_row, lse_row, dl_row)


@jax.custom_vjp
def _fox_attention(q, k, v, f):
    return _fox_attention_fwd(q, k, v, f)[0]


def _fox_attention_fwd(q, k, v, f):
    kb = k.astype(BF16)
    vb = v.astype(BF16)
    f2 = jnp.transpose(f) * _LOG2E
    f_row = f2[:, None, :]
    f_col = f2[:, :, None]
    o, lse = _fox_fwd_call(q, kb, vb, f_col, f_row)
    return o, (q, kb, vb, f_col, f_row, lse)


def _fox_attention_bwd(res, do):
    q, kb, vb, f_col, f_row, lse = res
    length = q.shape[0]
    lse_row = jnp.transpose(lse, (0, 2, 1))
    dq, dl_col = _fox_dq_call(q, kb, vb, f_col, f_row, lse, do)
    dl_row = jnp.transpose(dl_col, (0, 2, 1))
    qs = (q * _FOX_SCALE2).astype(BF16)
    dkh, dvh, dfk = _fox_dkv_call(qs, do.astype(BF16), kb, vb, f_col, f_row, lse_row, dl_row)
    dk = jnp.sum(dkh.reshape(length, FOX_KV_HEADS, FOX_GROUP, FOX_HEAD_DIM), axis=2).reshape(length, FOX_KV_DIM)
    dv = jnp.sum(dvh.reshape(length, FOX_KV_HEADS, FOX_GROUP, FOX_HEAD_DIM), axis=2).reshape(length, FOX_KV_DIM)
    df = jnp.transpose(dfk[:, :, 0])
    return dq, dk, dv, df


_fox_attention.defvjp(_fox_attention_fwd, _fox_attention_bwd)


def _peer(mx, my, mc, k):
    return (1 - mx if k & 4 else mx, 1 - my if k & 2 else my, 1 - mc if k & 1 else mc)


def _all_gather(x, *, name, in_vmem):
    space = pltpu.VMEM if in_vmem else pl.ANY

    def body(x_ref, out_ref, send_sems, recv_sems, local_sem):
        mx, my, mc = lax.axis_index("x"), lax.axis_index("y"), lax.axis_index("c")
        me = 4 * mx + 2 * my + mc
        mine = pltpu.make_async_copy(x_ref, out_ref.at[me], local_sem)
        mine.start()
        copies = []
        for k in range(1, N_DEV):
            cp = pltpu.make_async_remote_copy(
                src_ref=x_ref, dst_ref=out_ref.at[me],
                send_sem=send_sems.at[k - 1], recv_sem=recv_sems.at[k - 1],
                device_id=_peer(mx, my, mc, k), device_id_type=MESH)
            cp.start()
            copies.append(cp)
        for cp in copies:
            cp.wait()
        mine.wait()

    return pl.pallas_call(
        body, name=name,
        out_shape=jax.ShapeDtypeStruct((N_DEV,) + x.shape, x.dtype),
        in_specs=[pl.BlockSpec(memory_space=space)],
        out_specs=pl.BlockSpec(memory_space=space),
        scratch_shapes=[pltpu.SemaphoreType.DMA((N_DEV - 1,)), pltpu.SemaphoreType.DMA((N_DEV - 1,)),
                        pltpu.SemaphoreType.DMA],
    )(x)


N_CHIP = N_DEV // 2
_HBM = pl.BlockSpec(memory_space=pl.ANY)


def _all_gather_by_chip(xs, *, name):
    n_arr = len(xs)
    per = N_DEV - 1

    def body(*refs):
        x_refs, out_refs = refs[:n_arr], refs[n_arr:2 * n_arr]
        send_sems, recv_sems, local_sems = refs[2 * n_arr:]
        mx, my, mc = lax.axis_index("x"), lax.axis_index("y"), lax.axis_index("c")
        me, sibling = (mx, my, mc), (mx, my, 1 - mc)
        chips = [(1 - mx, my), (mx, 1 - my), (1 - mx, 1 - my)]

        def copy(a, k, block, to, from_input=False):
            px, py, pc = block
            slot = out_refs[a].at[4 * px + 2 * py + pc]
            return pltpu.make_async_remote_copy(
                src_ref=x_refs[a] if from_input else slot, dst_ref=slot,
                send_sem=send_sems.at[a * per + k], recv_sem=recv_sems.at[a * per + k],
                device_id=to, device_id_type=MESH)

        mine = [pltpu.make_async_copy(x_refs[a], out_refs[a].at[4 * mx + 2 * my + mc], local_sems.at[a])
                for a in range(n_arr)]
        for cp in mine:
            cp.start()
        sent = []
        for a in range(n_arr):
            sent.append(copy(a, 0, me, sibling, from_input=True))
            sent += [copy(a, 1 + j, me, (*chip, mc), from_input=True) for j, chip in enumerate(chips)]
        for cp in sent:
            cp.start()
        for a in range(n_arr):
            for j, chip in enumerate(chips):
                copy(a, 1 + j, (*chip, mc), me).wait_recv()
                passed = copy(a, 4 + j, (*chip, mc), sibling)
                passed.start()
                sent.append(passed)
        for a in range(n_arr):
            copy(a, 0, sibling, me).wait_recv()
            for j, chip in enumerate(chips):
                copy(a, 4 + j, (*chip, 1 - mc), me).wait_recv()
        for cp in sent:
            cp.wait_send()
        for cp in mine:
            cp.wait()

    return pl.pallas_call(
        body, name=name,
        out_shape=[jax.ShapeDtypeStruct((N_DEV,) + x.shape, x.dtype) for x in xs],
        in_specs=[_HBM] * n_arr,
        out_specs=[_HBM] * n_arr,
        scratch_shapes=[pltpu.SemaphoreType.DMA((n_arr * per,)), pltpu.SemaphoreType.DMA((n_arr * per,)),
                        pltpu.SemaphoreType.DMA((n_arr,))],
    )(*xs)


def _exchange_sibling(xs, *, name):
    n_arr = len(xs)

    def body(*refs):
        x_refs, out_refs = refs[:n_arr], refs[n_arr:2 * n_arr]
        send_sems, recv_sems = refs[2 * n_arr:]
        mc = lax.axis_index("c")
        sibling = (lax.axis_index("x"), lax.axis_index("y"), 1 - mc)
        copies = [pltpu.make_async_remote_copy(
            src_ref=x_refs[a].at[q, 1 - mc], dst_ref=out_refs[a].at[q],
            send_sem=send_sems.at[a * N_CHIP + q], recv_sem=recv_sems.at[a * N_CHIP + q],
            device_id=sibling, device_id_type=MESH) for a in range(n_arr) for q in range(N_CHIP)]
        for cp in copies:
            cp.start()
        for cp in copies:
            cp.wait()

    return pl.pallas_call(
        body, name=name,
        out_shape=[jax.ShapeDtypeStruct((N_CHIP,) + x.shape[2:], x.dtype) for x in xs],
        in_specs=[_HBM] * n_arr,
        out_specs=[_HBM] * n_arr,
        scratch_shapes=[pltpu.SemaphoreType.DMA((n_arr * N_CHIP,)), pltpu.SemaphoreType.DMA((n_arr * N_CHIP,))],
    )(*xs)


def _exchange_chips(xs, *, name):
    n_arr = len(xs)
    per = N_CHIP - 1

    def body(*refs):
        x_refs, out_refs = refs[:n_arr], refs[n_arr:2 * n_arr]
        send_sems, recv_sems, local_sems = refs[2 * n_arr:]
        mx, my, mc = lax.axis_index("x"), lax.axis_index("y"), lax.axis_index("c")
        my_chip = 2 * mx + my
        copies = []
        for a in range(n_arr):
            mine = pltpu.make_async_copy(x_refs[a].at[my_chip], out_refs[a].at[my_chip], local_sems.at[a])
            mine.start()
            copies.append(mine)
            for j, (px, py) in enumerate([(1 - mx, my), (mx, 1 - my), (1 - mx, 1 - my)]):
                cp = pltpu.make_async_remote_copy(
                    src_ref=x_refs[a].at[2 * px + py], dst_ref=out_refs[a].at[my_chip],
                    send_sem=send_sems.at[a * per + j], recv_sem=recv_sems.at[a * per + j],
                    device_id=(px, py, mc), device_id_type=MESH)
                cp.start()
                copies.append(cp)
        for cp in copies:
            cp.wait()

    return pl.pallas_call(
        body, name=name,
        out_shape=[jax.ShapeDtypeStruct(x.shape, x.dtype) for x in xs],
        in_specs=[_HBM] * n_arr,
        out_specs=[_HBM] * n_arr,
        scratch_shapes=[pltpu.SemaphoreType.DMA((n_arr * per,)), pltpu.SemaphoreType.DMA((n_arr * per,)),
                        pltpu.SemaphoreType.DMA((n_arr,))],
    )(*xs)


def _add_pair(a, b, *, name):
    n, r, c = a.shape
    tr = _pick(r, (512, 256, 128, 64, 32, 16))

    def body(a_ref, b_ref, o_ref):
        o_ref[...] = (a_ref[...].astype(F32) + b_ref[...].astype(F32)).astype(o_ref.dtype)

    spec = pl.BlockSpec((1, tr, c), lambda q, i: (q, i, 0))
    return pl.pallas_call(
        body, name=name,
        grid=(n, r // tr),
        in_specs=[spec, spec],
        out_specs=spec,
        out_shape=jax.ShapeDtypeStruct(a.shape, a.dtype),
        compiler_params=_cparams(("parallel", "parallel")),
    )(a, b)


def _sum_slots(x, *, name):
    n, r, c = x.shape
    tr = _pick(r, (256, 128, 64, 32, 16, 8))

    def body(x_ref, o_ref):
        acc = x_ref[0].astype(F32)
        for s in range(1, n):
            acc = acc + x_ref[s].astype(F32)
        o_ref[...] = acc

    return pl.pallas_call(
        body, name=name,
        grid=(r // tr,),
        in_specs=[pl.BlockSpec((n, tr, c), lambda i: (0, i, 0))],
        out_specs=pl.BlockSpec((tr, c), lambda i: (i, 0)),
        out_shape=jax.ShapeDtypeStruct((r, c), F32),
        compiler_params=_cparams(("parallel",)),
    )(x)


def _adamw(w, g, m, v, *, name):
    shape = w.shape
    c = shape[-1]
    r = w.size // c
    w2, g2, m2, v2 = (t.reshape(r, c) for t in (w, g, m, v))
    tr = _pick(r, (128, 64, 32, 16, 8))

    def body(w_ref, g_ref, m_ref, v_ref, d_ref, mo_ref, vo_ref):
        gv = g_ref[...]
        mn = ADAM_B1 * m_ref[...] + (1.0 - ADAM_B1) * gv
        vn = ADAM_B2 * v_ref[...] + (1.0 - ADAM_B2) * (gv * gv)
        m_hat = mn / (1.0 - ADAM_B1 ** ADAM_STEP)
        v_hat = vn / (1.0 - ADAM_B2 ** ADAM_STEP)
        d_ref[...] = -ADAM_LR * (m_hat / (jnp.sqrt(v_hat) + ADAM_EPS) + ADAM_WD * w_ref[...])
        mo_ref[...] = mn
        vo_ref[...] = vn

    spec = pl.BlockSpec((tr, c), lambda i: (i, 0))
    outs = pl.pallas_call(
        body, name=name,
        grid=(r // tr,),
        in_specs=[spec] * 4,
        out_specs=[spec] * 3,
        out_shape=[jax.ShapeDtypeStruct((r, c), F32)] * 3,
        compiler_params=_cparams(("parallel",)),
    )(w2, g2, m2, v2)
    return tuple(o.reshape(shape) for o in outs)


GDN_HEADS_PER_STEP = 16
_TN = (((0,), (0,)), ((), ()))


def _gdn_scan_specs(qd, u, attn, reverse):
    h, n, c, dk = qd.shape
    dv = u.shape[-1]
    g = GDN_HEADS_PER_STEP
    idx = (lambda hb, i: (hb, n - 1 - i, 0, 0)) if reverse else (lambda hb, i: (hb, i, 0, 0))
    return dict(
        qk=pl.BlockSpec((g, 1, c, dk), idx), uv=pl.BlockSpec((g, 1, c, dv), idx),
        attn=pl.BlockSpec((g, 1, c, c), idx), lane=pl.BlockSpec((g, 1, 1, dv), idx),
        state=pl.BlockSpec((g, 1, dk, dv), idx), grid=(h // g, n))


def _gdn_scan_fwd_call(qd, kd, u, w, attn, egl):
    h, n, c, dk = qd.shape
    dv = u.shape[-1]
    sp = _gdn_scan_specs(qd, u, attn, reverse=False)

    def body(q_ref, k_ref, u_ref, w_ref, a_ref, e_ref, o_ref, st_ref, s_sc):
        @pl.when(pl.program_id(1) == 0)
        def _():
            s_sc[...] = jnp.zeros_like(s_sc)

        for g in range(GDN_HEADS_PER_STEP):
            s = s_sc[g]
            st_ref[g, 0] = s
            sb = s.astype(BF16)
            v_new = u_ref[g, 0] - jnp.dot(w_ref[g, 0].astype(BF16), sb, preferred_element_type=F32)
            vb = v_new.astype(BF16)
            o_ref[g, 0] = (jnp.dot(q_ref[g, 0].astype(BF16), sb, preferred_element_type=F32)
                           + jnp.dot(a_ref[g, 0].astype(BF16), vb, preferred_element_type=F32))
            s_sc[g] = s * e_ref[g, 0] + lax.dot_general(k_ref[g, 0].astype(BF16), vb, _TN,
                                                        preferred_element_type=F32)

    return pl.pallas_call(
        body, name="gdn_scan_fwd",
        grid=sp["grid"],
        in_specs=[sp["qk"], sp["qk"], sp["uv"], sp["qk"], sp["attn"], sp["lane"]],
        out_specs=[sp["uv"], sp["state"]],
        out_shape=[jax.ShapeDtypeStruct((h, n, c, dv), F32), jax.ShapeDtypeStruct((h, n, dk, dv), F32)],
        scratch_shapes=[pltpu.VMEM((GDN_HEADS_PER_STEP, dk, dv), F32)],
        compiler_params=_cparams(("parallel", "arbitrary")),
    )(qd, kd, u, w, attn, egl)


def _gdn_scan_bwd_call(qd, kd, u, w, attn, egl, states, do):
    h, n, c, dk = qd.shape
    dv = u.shape[-1]
    sp = _gdn_scan_specs(qd, u, attn, reverse=True)

    def body(q_ref, k_ref, u_ref, w_ref, a_ref, e_ref, st_ref, do_ref,
             dq_ref, dk_ref, du_ref, dw_ref, da_ref, de_ref, ds_sc):
        @pl.when(pl.program_id(1) == 0)
        def _():
            ds_sc[...] = jnp.zeros_like(ds_sc)

        for g in range(GDN_HEADS_PER_STEP):
            s = st_ref[g, 0]
            sb = s.astype(BF16)
            ds = ds_sc[g]
            dsb = ds.astype(BF16)
            qb, kb, wb, ab = (r[g, 0].astype(BF16) for r in (q_ref, k_ref, w_ref, a_ref))
            dob = do_ref[g, 0].astype(BF16)
            vb = (u_ref[g, 0] - jnp.dot(wb, sb, preferred_element_type=F32)).astype(BF16)
            dv_new = (lax.dot_general(ab, dob, _TN, preferred_element_type=F32)
                      + jnp.dot(kb, dsb, preferred_element_type=F32))
            dvb = dv_new.astype(BF16)
            dq_ref[g, 0] = lax.dot_general(dob, sb, _NT, preferred_element_type=F32)
            da_ref[g, 0] = lax.dot_general(dob, vb, _NT, preferred_element_type=F32)
            dk_ref[g, 0] = lax.dot_general(vb, dsb, _NT, preferred_element_type=F32)
            du_ref[g, 0] = dv_new
            dw_ref[g, 0] = -lax.dot_general(dvb, sb, _NT, preferred_element_type=F32)
            de_ref[g, 0] = jnp.sum(ds * s, axis=0, keepdims=True)
            ds_sc[g] = (ds * e_ref[g, 0] + lax.dot_general(qb, dob, _TN, preferred_element_type=F32)
                        - lax.dot_general(wb, dvb, _TN, preferred_element_type=F32))

    return pl.pallas_call(
        body, name="gdn_scan_bwd",
        grid=sp["grid"],
        in_specs=[sp["qk"], sp["qk"], sp["uv"], sp["qk"], sp["attn"], sp["lane"], sp["state"], sp["uv"]],
        out_specs=[sp["qk"], sp["qk"], sp["uv"], sp["qk"], sp["attn"], sp["lane"]],
        out_shape=[jax.ShapeDtypeStruct((h, n, c, dk), F32), jax.ShapeDtypeStruct((h, n, c, dk), F32),
                   jax.ShapeDtypeStruct((h, n, c, dv), F32), jax.ShapeDtypeStruct((h, n, c, dk), F32),
                   jax.ShapeDtypeStruct((h, n, c, c), F32), jax.ShapeDtypeStruct((h, n, 1, dv), F32)],
        scratch_shapes=[pltpu.VMEM((GDN_HEADS_PER_STEP, dk, dv), F32)],
        compiler_params=_cparams(("parallel", "arbitrary")),
    )(qd, kd, u, w, attn, egl, states, do)


GDN_CHUNKS_PER_STEP = 16
GDN_DIAG = 16


def _tri_solve_fwd_call(a, rhs):
    h, n, c, _ = a.shape
    r = rhs.shape[-1]
    g_n = GDN_CHUNKS_PER_STEP

    batched = (((2,), (1,)), ((0,), (0,)))

    def dot(x, y):
        return lax.dot_general(x.astype(BF16), y.astype(BF16), batched, preferred_element_type=F32)

    def body(a_ref, rhs_ref, sol_ref, t_ref):
        ri = lax.broadcasted_iota(jnp.int32, (g_n, c, c), 1)
        ci = lax.broadcasted_iota(jnp.int32, (g_n, c, c), 2)
        eye = (ri == ci).astype(F32)
        av = a_ref[0]
        d1 = jnp.where(ri // GDN_DIAG == ci // GDN_DIAG, av, 0.0)
        d2 = dot(d1, d1)
        d4 = dot(d2, d2)
        d8 = dot(d4, d4)
        t_inv = dot(dot(eye - d1, eye + d2), dot(eye + d4, eye + d8))
        size = GDN_DIAG
        while size < c:
            below = jnp.where((ri // (2 * size) == ci // (2 * size)) & (ri // size != ci // size), av, 0.0)
            t_inv = t_inv - dot(t_inv, dot(below, t_inv))
            size *= 2
        t_ref[0] = t_inv
        sol_ref[0] = dot(t_inv, rhs_ref[0])

    a_spec = pl.BlockSpec((1, g_n, c, c), lambda i, j: (i, j, 0, 0))
    r_spec = pl.BlockSpec((1, g_n, c, r), lambda i, j: (i, j, 0, 0))
    return pl.pallas_call(
        body, name="gdn_tri_solve",
        grid=(h, n // g_n),
        in_specs=[a_spec, r_spec],
        out_specs=[r_spec, a_spec],
        out_shape=[jax.ShapeDtypeStruct(rhs.shape, F32), jax.ShapeDtypeStruct(a.shape, F32)],
        compiler_params=_cparams(("parallel", "parallel")),
    )(a, rhs)


def _tri_solve_bwd_call(t_inv, sol, dsol):
    h, n, c, _ = t_inv.shape
    r = sol.shape[-1]
    g_n = GDN_CHUNKS_PER_STEP

    def body(t_ref, sol_ref, dsol_ref, drhs_ref, da_ref):
        for g in range(g_n):
            d_rhs = lax.dot_general(t_ref[0, g].astype(BF16), dsol_ref[0, g].astype(BF16), _TN,
                                    preferred_element_type=F32)
            drhs_ref[0, g] = d_rhs
            da_ref[0, g] = -lax.dot_general(d_rhs.astype(BF16), sol_ref[0, g].astype(BF16), _NT,
                                            preferred_element_type=F32)

    a_spec = pl.BlockSpec((1, g_n, c, c), lambda i, j: (i, j, 0, 0))
    r_spec = pl.BlockSpec((1, g_n, c, r), lambda i, j: (i, j, 0, 0))
    return pl.pallas_call(
        body, name="gdn_tri_solve_bwd",
        grid=(h, n // g_n),
        in_specs=[a_spec, r_spec, r_spec],
        out_specs=[r_spec, a_spec],
        out_shape=[jax.ShapeDtypeStruct(sol.shape, F32), jax.ShapeDtypeStruct(t_inv.shape, F32)],
        compiler_params=_cparams(("parallel", "parallel")),
    )(t_inv, sol, dsol)


@jax.custom_vjp
def _tri_solve(a, rhs):
    return _tri_solve_fwd_call(a, rhs)[0]


def _tri_solve_fwd(a, rhs):
    sol, t_inv = _tri_solve_fwd_call(a, rhs)
    return sol, (t_inv, sol)


def _tri_solve_bwd(res, dsol):
    t_inv, sol = res
    d_rhs, d_a = _tri_solve_bwd_call(t_inv, sol, dsol)
    return d_a, d_rhs


_tri_solve.defvjp(_tri_solve_fwd, _tri_solve_bwd)


@jax.custom_vjp
def _gdn_scan(qd, kd, u, w, attn, g_last):
    return _gdn_scan_fwd(qd, kd, u, w, attn, g_last)[0]


def _gdn_scan_fwd(qd, kd, u, w, attn, g_last):
    egl = jnp.broadcast_to(jnp.exp(g_last)[:, :, None, None], g_last.shape + (1, u.shape[-1]))
    o, states = _gdn_scan_fwd_call(qd, kd, u, w, attn, egl)
    return o, (qd, kd, u, w, attn, egl, states)


def _gdn_scan_bwd(res, do):
    qd, kd, u, w, attn, egl, states = res
    dq, dk, du, dw, da, de = _gdn_scan_bwd_call(qd, kd, u, w, attn, egl, states, do)
    dgl = jnp.sum(de[:, :, 0, :], axis=-1) * egl[:, :, 0, 0]
    return dq, dk, du, dw, da, dgl


_gdn_scan.defvjp(_gdn_scan_fwd, _gdn_scan_bwd)


@functools.partial(jax.custom_vjp, nondiff_argnums=(1,))
def _split_cols(x, sizes):
    bounds = [sum(sizes[:i]) for i in range(len(sizes) + 1)]
    return tuple(x[:, bounds[i]:bounds[i + 1]] for i in range(len(sizes)))


def _split_cols_fwd(x, sizes):
    return _split_cols(x, sizes), None


def _split_cols_bwd(sizes, _, parts):
    return (jnp.concatenate(parts, axis=1),)


_split_cols.defvjp(_split_cols_fwd, _split_cols_bwd)


def _rms_norm(x, w):
    y = x * lax.rsqrt(jnp.mean(x * x, axis=-1, keepdims=True) + NORM_EPS)
    return y * w


def _modulate(h, shift, scale):
    return h * (1 + scale) + shift


def _l2_normalize(x):
    return x * lax.rsqrt(jnp.sum(x * x, axis=-1, keepdims=True) + NORM_EPS)


def _causal_conv(x, w):
    width = w.shape[0]
    length = x.shape[0]
    xp = jnp.pad(x, ((width - 1, 0), (0, 0)))
    return sum(xp[i:i + length] * w[i] for i in range(width))


def _gated_delta_rule_chunked(q, k, v, g, beta):
    h, length, dk = q.shape
    dv = v.shape[-1]
    n = length // GDN_CHUNK
    blk = lambda t: t.reshape(h, n, GDN_CHUNK, *t.shape[2:])
    q = blk(q) * dk ** -0.5
    k = blk(k)
    v = blk(v)
    beta = blk(beta)
    g = jnp.cumsum(blk(g), axis=-1)
    causal = jnp.tril(jnp.ones((GDN_CHUNK, GDN_CHUNK), dtype=bool))
    strict = jnp.tril(jnp.ones((GDN_CHUNK, GDN_CHUNK), dtype=bool), k=-1)
    decay = jnp.exp(jnp.where(causal, g[..., :, None] - g[..., None, :], -jnp.inf))
    k_beta = k * beta[..., None]
    a_strict = jnp.where(strict, jnp.einsum('hnid,hnjd->hnij', k_beta, k) * decay, 0.0)
    rhs = jnp.concatenate([v * beta[..., None], k_beta * jnp.exp(g)[..., None]], axis=-1)
    sol = _tri_solve(a_strict, rhs)
    u, w = sol[..., :dv], sol[..., dv:]
    attn = jnp.where(causal, jnp.einsum('hnid,hnjd->hnij', q, k) * decay, 0.0)
    g_last = g[..., -1]
    q_dec = q * jnp.exp(g)[..., None]
    k_dec = k * jnp.exp(g_last[..., None] - g)[..., None]

    return _gdn_scan(q_dec, k_dec, u, w, attn, g_last).reshape(h, length, dv)


_MM = {tag: _make_mm(tag) for tag in ("gdn_in", "gdn_ba", "gdn_out", "ffn_out0", "kv", "kv_f", "fox_out", "ffn_out1")}
_MM.update({tag: _make_mm(tag, dev_blocks=True) for tag in ("ffn_in0", "ffn_in1", "fox_in")})


def _swiglu(h, w_in, w_out, c_in, c_out, layer):
    gu = _MM[f"ffn_in{layer}"](h, w_in, c_in)
    gate, up = _split_cols(gu, (FFN_HIDDEN, FFN_HIDDEN))
    return _MM[f"ffn_out{layer}"](jax.nn.silu(gate) * up, w_out, c_out)


def _gated_deltanet(h, wts, car, conv_w, a_log, dt_bias, norm_w):
    length = h.shape[0]
    main = _MM["gdn_in"](h, wts["gdn_in"], car["gdn_in"])
    ba = _MM["gdn_ba"](h, wts["gdn_ba"], car["gdn_ba"])
    qkv, z = _split_cols(main, (CONV_DIM, GDN_V_DIM))
    beta_logit, a = ba[:, :GDN_V_HEADS], ba[:, GDN_V_HEADS:2 * GDN_V_HEADS]
    qkv = jax.nn.silu(_causal_conv(qkv, conv_w))
    q, k, v = _split_cols(qkv, (GDN_QK_DIM, GDN_QK_DIM, GDN_V_DIM))
    rep = GDN_V_HEADS // GDN_QK_HEADS
    heads = lambda t, nh: t.reshape(length, nh, GDN_HEAD_DIM)
    q = jnp.repeat(_l2_normalize(heads(q, GDN_QK_HEADS)), rep, axis=1)
    k = jnp.repeat(_l2_normalize(heads(k, GDN_QK_HEADS)), rep, axis=1)
    v = heads(v, GDN_V_HEADS)
    beta = jax.nn.sigmoid(beta_logit)
    g = -jnp.exp(a_log) * jax.nn.softplus(a + dt_bias)
    tr = lambda t: jnp.swapaxes(t, 0, 1)
    o = _gated_delta_rule_chunked(tr(q), tr(k), tr(v), tr(g), tr(beta))
    o = _rms_norm(tr(o), norm_w) * jax.nn.silu(heads(z, GDN_V_HEADS))
    return _MM["gdn_out"](o.reshape(length, GDN_V_DIM), wts["gdn_out"], car["gdn_out"])


def _local_loss(x, mods, small, car, wts, target):
    split6 = lambda m: tuple(m[i * D_MODEL:(i + 1) * D_MODEL][None, :] for i in range(6))
    sh_m, sc_m, g_m, sh_f, sc_f, g_f = split6(mods["ada0"])
    h = _modulate(_rms_norm(x, small["norm_mix"][0]), sh_m, sc_m)
    y = _gated_deltanet(h, wts, car, small["gdn_conv"], small["gdn_a_log"][0], small["gdn_dt_bias"][0],
                        small["gdn_norm"][0])
    x = x + g_m * y
    h = _modulate(_rms_norm(x, small["norm_ffn"][0]), sh_f, sc_f)
    x = x + g_f * _swiglu(h, wts["ffn_in0"], wts["ffn_out0"], car["ffn_in0"], car["ffn_out0"], 0)
    length = x.shape[0]
    kv_shift, kv_scale = mods["kvada"][None, :D_MODEL], mods["kvada"][None, D_MODEL:]
    hk = _modulate(_rms_norm(x, small["kv_norm"]), kv_shift, kv_scale)
    kv = _MM["kv"](hk, wts["kv"], car["kv"])
    f_logit = _MM["kv_f"](hk, wts["kv_f"], car["kv_f"])[:, :FOX_HEADS]
    k_raw, v_sh = _split_cols(kv, (FOX_KV_DIM, FOX_KV_DIM))
    k_sh = _rms_norm(k_raw.reshape(length, FOX_KV_HEADS, FOX_HEAD_DIM), small["k_norm"])
    k_sh = k_sh.reshape(length, FOX_KV_DIM)
    f_cum = jnp.cumsum(jax.nn.log_sigmoid(f_logit + small["forget_b"]), axis=0)
    sh_m, sc_m, g_m, sh_f, sc_f, g_f = split6(mods["ada1"])
    h = _modulate(_rms_norm(x, small["norm_mix"][1]), sh_m, sc_m)
    qg = _MM["fox_in"](h, wts["fox_in"], car["fox_in"])
    q_raw, q_gate = _split_cols(qg, (FOX_Q_DIM, FOX_Q_DIM))
    q = _rms_norm(q_raw.reshape(length, FOX_HEADS, FOX_HEAD_DIM), small["q_norm"][0])
    o = _fox_attention(q.reshape(length, FOX_Q_DIM), k_sh, v_sh, f_cum)
    y = _MM["fox_out"](o * jax.nn.sigmoid(q_gate), wts["fox_out"], car["fox_out"])
    x = x + g_m * y
    h = _modulate(_rms_norm(x, small["norm_ffn"][1]), sh_f, sc_f)
    x = x + g_f * _swiglu(h, wts["ffn_in1"], wts["ffn_out1"], car["ffn_in1"], car["ffn_out1"], 1)
    out = _modulate(_rms_norm(x, small["out_norm"]), mods["outada"][None, :D_MODEL], mods["outada"][None, D_MODEL:])
    err = jnp.square(out - target)
    return 0.5 * jnp.sum(jnp.mean(err, axis=-1))


def _pad_rows(a, rows):
    return jnp.pad(a, ((0, rows - a.shape[0]), (0, 0)))


def _pack_flat(parts, total):
    flat = jnp.concatenate([p.reshape(-1) for p in parts])
    return jnp.pad(flat, (0, total - flat.shape[0]))


_SMALL_NAMES = ("ada_b", "norm_mix", "norm_ffn", "gdn_a_log", "gdn_dt_bias", "gdn_norm", "kv_ada_b", "kv_norm",
                "k_norm", "forget_b", "q_norm", "out_ada_b", "out_norm")


def kernel(x, c, ada_w, ada_b, norm_mix, norm_ffn, ffn_w_in, ffn_w_out, gdn_w_in, gdn_conv, gdn_a_log, gdn_dt_bias, gdn_norm, gdn_w_out, kv_ada_w, kv_ada_b, kv_norm, kv_w, k_norm, forget_b, fox_w_in, q_norm, fox_w_out, out_ada_w, out_ada_b, out_norm, loss_target, m_ada_w, m_ada_b, m_norm_mix, m_norm_ffn, m_ffn_w_in, m_ffn_w_out, m_gdn_w_in, m_gdn_conv, m_gdn_a_log, m_gdn_dt_bias, m_gdn_norm, m_gdn_w_out, m_kv_ada_w, m_kv_ada_b, m_kv_norm, m_kv_w, m_k_norm, m_forget_b, m_fox_w_in, m_q_norm, m_fox_w_out, m_out_ada_w, m_out_ada_b, m_out_norm, v_ada_w, v_ada_b, v_norm_mix, v_norm_ffn, v_ffn_w_in, v_ffn_w_out, v_gdn_w_in, v_gdn_conv, v_gdn_a_log, v_gdn_dt_bias, v_gdn_norm, v_gdn_w_out, v_kv_ada_w, v_kv_ada_b, v_kv_norm, v_kv_w, v_k_norm, v_forget_b, v_fox_w_in, v_q_norm, v_fox_w_out, v_out_ada_w, v_out_ada_b, v_out_norm):
    w_in = dict(ada_w=ada_w, ada_b=ada_b, norm_mix=norm_mix, norm_ffn=norm_ffn, ffn_w_in=ffn_w_in, ffn_w_out=ffn_w_out, gdn_w_in=gdn_w_in, gdn_conv=gdn_conv, gdn_a_log=gdn_a_log, gdn_dt_bias=gdn_dt_bias, gdn_norm=gdn_norm, gdn_w_out=gdn_w_out, kv_ada_w=kv_ada_w, kv_ada_b=kv_ada_b, kv_norm=kv_norm, kv_w=kv_w, k_norm=k_norm, forget_b=forget_b, fox_w_in=fox_w_in, q_norm=q_norm, fox_w_out=fox_w_out, out_ada_w=out_ada_w, out_ada_b=out_ada_b, out_norm=out_norm)
    m_in = dict(ada_w=m_ada_w, ada_b=m_ada_b, norm_mix=m_norm_mix, norm_ffn=m_norm_ffn, ffn_w_in=m_ffn_w_in, ffn_w_out=m_ffn_w_out, gdn_w_in=m_gdn_w_in, gdn_conv=m_gdn_conv, gdn_a_log=m_gdn_a_log, gdn_dt_bias=m_gdn_dt_bias, gdn_norm=m_gdn_norm, gdn_w_out=m_gdn_w_out, kv_ada_w=m_kv_ada_w, kv_ada_b=m_kv_ada_b, kv_norm=m_kv_norm, kv_w=m_kv_w, k_norm=m_k_norm, forget_b=m_forget_b, fox_w_in=m_fox_w_in, q_norm=m_q_norm, fox_w_out=m_fox_w_out, out_ada_w=m_out_ada_w, out_ada_b=m_out_ada_b, out_norm=m_out_norm)
    v_in = dict(ada_w=v_ada_w, ada_b=v_ada_b, norm_mix=v_norm_mix, norm_ffn=v_norm_ffn, ffn_w_in=v_ffn_w_in, ffn_w_out=v_ffn_w_out, gdn_w_in=v_gdn_w_in, gdn_conv=v_gdn_conv, gdn_a_log=v_gdn_a_log, gdn_dt_bias=v_gdn_dt_bias, gdn_norm=v_gdn_norm, gdn_w_out=v_gdn_w_out, kv_ada_w=v_kv_ada_w, kv_ada_b=v_kv_ada_b, kv_norm=v_kv_norm, kv_w=v_kv_w, k_norm=v_k_norm, forget_b=v_forget_b, fox_w_in=v_fox_w_in, q_norm=v_q_norm, fox_w_out=v_fox_w_out, out_ada_w=v_out_ada_w, out_ada_b=v_out_ada_b, out_norm=v_out_norm)
    names = list(w_in)
    me = 4 * lax.axis_index("x") + 2 * lax.axis_index("y") + lax.axis_index("c")
    d = D_MODEL

    n_conv = CONV_K * (CONV_DIM // N_DEV)
    first = _pack_flat([c, gdn_conv], d + n_conv).reshape(-1, LANES)
    first_all = _all_gather(first, name="ag_cond_conv", in_vmem=True).reshape(N_DEV, d + n_conv)
    cond_all = jax.nn.silu(first_all[:, :d])
    conv_full = first_all[:, d:].reshape(N_DEV, CONV_K, CONV_DIM // N_DEV)
    conv_full = jnp.transpose(conv_full, (1, 0, 2)).reshape(CONV_K, CONV_DIM)

    ada_cat = jnp.concatenate([ada_w[0], ada_w[1], kv_ada_w, out_ada_w], axis=1)
    n_ada = ada_cat.shape[1]
    cond_pad = _pad_rows(cond_all, LANES)
    mods_part = _matmul(cond_pad, ada_cat, name="mm_ada")[:N_DEV]
    mods_all = _all_gather(mods_part.reshape(-1, LANES), name="ag_mods", in_vmem=True)
    mods_all = mods_all.reshape(N_DEV, N_DEV, n_ada)
    mine = lax.dynamic_index_in_dim(mods_all, me, axis=1, keepdims=False)
    s6 = 6 * d // N_DEV
    s2 = 2 * d // N_DEV
    mods = {
        "ada0": mine[:, :s6].reshape(-1) + ada_b[0],
        "ada1": mine[:, s6:2 * s6].reshape(-1) + ada_b[1],
        "kvada": mine[:, 2 * s6:2 * s6 + s2].reshape(-1) + kv_ada_b,
        "outada": mine[:, 2 * s6 + s2:].reshape(-1) + out_ada_b,
    }

    big = [ffn_w_in[0], ffn_w_in[1], ffn_w_out[0], ffn_w_out[1], gdn_w_in[0], gdn_w_out[0], kv_w, fox_w_in[0],
           fox_w_out[0]]
    g_ffn_in0, g_ffn_in1, g_ffn_out0, g_ffn_out1, g_gdn_in, g_gdn_out, g_kv, g_fox_in, g_fox_out = (
        _all_gather_by_chip([b.astype(BF16) for b in big], name="ag_weights"))
    rows = lambda t: t.reshape(N_DEV * t.shape[1], t.shape[2])
    per_gdn = GDN_PROJ // N_DEV
    last_main = GDN_MAIN - (N_DEV - 1) * per_gdn
    gdn_main = jnp.concatenate([g_gdn_in[dv] for dv in range(N_DEV - 1)] + [g_gdn_in[N_DEV - 1][:, :last_main]],
                               axis=1)
    gdn_ba = jnp.pad(g_gdn_in[N_DEV - 1][:, last_main:], ((0, 0), (0, LANES - 2 * GDN_V_HEADS)))
    kv_full = rows(g_kv)
    wts = {
        "ffn_in0": g_ffn_in0, "ffn_in1": g_ffn_in1, "ffn_out0": rows(g_ffn_out0), "ffn_out1": rows(g_ffn_out1),
        "gdn_in": gdn_main, "gdn_ba": gdn_ba, "gdn_out": rows(g_gdn_out),
        "kv": kv_full[:, :2 * FOX_KV_DIM],
        "kv_f": jnp.pad(kv_full[:, 2 * FOX_KV_DIM:], ((0, 0), (0, LANES - FOX_HEADS))),
        "fox_in": g_fox_in, "fox_out": rows(g_fox_out),
    }
    car = {k: jnp.zeros(w.shape, BF16) for k, w in wts.items()}
    small = {n: w_in[n] for n in _SMALL_NAMES if n not in ("ada_b", "kv_ada_b", "out_ada_b")}
    small["gdn_conv"] = conv_full

    loss_fn = functools.partial(_local_loss, wts=wts, target=loss_target[0])
    loss_local, vjp = jax.vjp(loss_fn, x[0], mods, small, car)
    gx, gmods, gsmall, gcar = vjp(jnp.ones((), F32))
    loss = lax.psum(loss_local, ("x", "y", "c"))

    gsm = dict(gsmall)
    gsm["ada_b"] = jnp.stack([gmods["ada0"], gmods["ada1"]])
    gsm["kv_ada_b"] = gmods["kvada"]
    gsm["out_ada_b"] = gmods["outada"]
    small_sizes = [w_in[n].size for n in _SMALL_NAMES]
    n_small = sum(small_sizes) + CONV_K * CONV_DIM
    small_rows = -(-n_small // (LANES * 8)) * 8
    vec = _pack_flat([gsm[n] for n in _SMALL_NAMES] + [gsm["gdn_conv"]], small_rows * LANES)
    vec_all = _all_gather(vec.reshape(small_rows, LANES), name="ag_small_grads", in_vmem=True)
    vec_sum = _sum_slots(vec_all, name="sum_small_grads").reshape(-1)
    grads = {}
    off = 0
    for n, sz in zip(_SMALL_NAMES, small_sizes):
        grads[n] = vec_sum[off:off + sz].reshape(w_in[n].shape)
        off += sz
    conv_sum = vec_sum[off:off + CONV_K * CONV_DIM].reshape(CONV_K, N_DEV, CONV_DIM // N_DEV)
    grads["gdn_conv"] = lax.dynamic_index_in_dim(conv_sum, me, axis=1, keepdims=False)[None]

    vec_flat = vec_all.reshape(N_DEV, -1)
    d_ada = vec_flat[:, :2 * 6 * d].reshape(N_DEV, 2, N_DEV, s6)
    o_kv = sum(small_sizes[:6])
    d_kvada = vec_flat[:, o_kv:o_kv + 2 * d].reshape(N_DEV, N_DEV, s2)
    o_out = sum(small_sizes[:11])
    d_outada = vec_flat[:, o_out:o_out + 2 * d].reshape(N_DEV, N_DEV, s2)
    pick = lambda t, axis: lax.dynamic_index_in_dim(t, me, axis=axis, keepdims=False)
    d_cat = jnp.concatenate([pick(d_ada[:, 0], 1), pick(d_ada[:, 1], 1), pick(d_kvada, 1), pick(d_outada, 1)],
                            axis=1)
    g_ada_cat = _matmul(cond_pad, _pad_rows(d_cat, LANES), trans_a=True, name="mm_ada_dw")
    grads["ada_w"] = jnp.stack([g_ada_cat[:, :s6], g_ada_cat[:, s6:2 * s6]])
    grads["kv_ada_w"] = g_ada_cat[:, 2 * s6:2 * s6 + s2]
    grads["out_ada_w"] = g_ada_cat[:, 2 * s6 + s2:]

    d_main, d_ba = gcar["gdn_in"], gcar["gdn_ba"]
    d_gdn_in = jnp.stack(
        [d_main[:, dv * per_gdn:(dv + 1) * per_gdn] for dv in range(N_DEV - 1)]
        + [jnp.concatenate([d_main[:, (N_DEV - 1) * per_gdn:], d_ba[:, :2 * GDN_V_HEADS]], axis=1)])
    d_kv = jnp.concatenate([gcar["kv"], gcar["kv_f"][:, :FOX_HEADS]], axis=1)
    owner_rows = lambda t: t.reshape(N_DEV, t.shape[0] // N_DEV, t.shape[1])
    by_owner = [gcar["ffn_in0"], gcar["ffn_in1"], owner_rows(gcar["ffn_out0"]), owner_rows(gcar["ffn_out1"]),
                d_gdn_in, owner_rows(gcar["gdn_out"]), owner_rows(d_kv), gcar["fox_in"],
                owner_rows(gcar["fox_out"])]
    by_chip_core = [t.reshape((N_CHIP, 2) + t.shape[1:]) for t in by_owner]
    my_c = lax.axis_index("c")
    from_sibling = _exchange_sibling(by_chip_core, name="rs_sibling")
    shard_grads = []
    pair_sums = []
    for idx, (t, got) in enumerate(zip(by_chip_core, from_sibling)):
        for_my_core = lax.dynamic_index_in_dim(t, my_c, axis=1, keepdims=False)
        pair_sums.append(_add_pair(for_my_core, got, name=f"rs_pair_sum{idx}"))
    by_chip = _exchange_chips(pair_sums, name="rs_chips")
    for idx, t in enumerate(by_chip):
        shard_grads.append(_sum_slots(t, name=f"sum_grads{idx}"))
    grads["ffn_w_in"] = jnp.stack(shard_grads[0:2])
    grads["ffn_w_out"] = jnp.stack(shard_grads[2:4])
    grads["gdn_w_in"] = shard_grads[4][None]
    grads["gdn_w_out"] = shard_grads[5][None]
    grads["kv_w"] = shard_grads[6]
    grads["fox_w_in"] = shard_grads[7][None]
    grads["fox_w_out"] = shard_grads[8][None]

    delta, new_m, new_v = {}, {}, {}
    for n in names:
        if n in _SMALL_NAMES:
            continue
        delta[n], new_m[n], new_v[n] = _adamw(w_in[n], grads[n], m_in[n], v_in[n], name=f"adamw_{n}")
    sm_rows = -(-sum(small_sizes) // (LANES * 8)) * 8
    pk = lambda src: _pack_flat([src[n] for n in _SMALL_NAMES], sm_rows * LANES).reshape(sm_rows, LANES)
    d_pk, m_pk, vn_pk = _adamw(pk(w_in), pk(grads), pk(m_in), pk(v_in), name="adamw_small")
    off = 0
    for n, sz in zip(_SMALL_NAMES, small_sizes):
        shp = w_in[n].shape
        delta[n] = d_pk.reshape(-1)[off:off + sz].reshape(shp)
        new_m[n] = m_pk.reshape(-1)[off:off + sz].reshape(shp)
        new_v[n] = vn_pk.reshape(-1)[off:off + sz].reshape(shp)
        off += sz

    return (loss, gx[None], *[grads[n] for n in names], *[delta[n] for n in names],
            *[new_m[n] for n in names], *[new_v[n] for n in names])
```
